```python
import math
import jax
import jax.numpy as jnp
from jax import lax
import numpy as np

D_MODEL = 2048
BATCH = 2
SEQ = 4096
DEPTH = 4
DEC_BATCH = 8
DEC_SEQ = 4096
PAST_LEN = 128

N_MIXERS = 4
GRID_W = 64
PLE_DIM = 256
EPS = 1e-6
NEG_INF = -1e30

D_FF = 5632
FFN_CONV = 3

NA_HEADS = 16
NA_HEAD_DIM = D_MODEL // NA_HEADS
NA_WIN_R = 8
NA_WIN_C = 16

SG_CHUNK = 128
SG_WIDTH = D_MODEL
SG_GROUPS = 16
SG_GROUP_DIM = SG_WIDTH // SG_GROUPS

GDN_QK_HEADS = 16
GDN_V_HEADS = 32
GDN_DK = 128
GDN_DV = 128
GDN_CONV = 3
GDN_CHUNK = 64
GDN_QK_WIDTH = GDN_QK_HEADS * GDN_DK
GDN_V_WIDTH = GDN_V_HEADS * GDN_DV
GDN_CONV_WIDTH = 2 * GDN_QK_WIDTH + GDN_V_WIDTH
GDN_IN_WIDTH = GDN_CONV_WIDTH + GDN_V_WIDTH + 4 * GDN_V_HEADS

S5_GROUP_DIM = 16
S5_GROUPS = D_MODEL // S5_GROUP_DIM
S5_STATE = 64
S5_CHUNK = 128

N_NA = len(range(0, DEPTH, N_MIXERS))
N_SG = len(range(1, DEPTH, N_MIXERS))
N_GDN = len(range(2, DEPTH, N_MIXERS))
N_S5 = len(range(3, DEPTH, N_MIXERS))

kernel_name = 'hybrid_bidir_encoder_na_sgu_gdn_s5'


def rmsnorm(x, g):
    xf = x.astype(jnp.float32)
    y = xf * lax.rsqrt(jnp.mean(xf * xf, axis=-1, keepdims=True) + EPS)
    return (y * g.astype(jnp.float32)).astype(x.dtype)


def l2norm(t):
    return t * lax.rsqrt(jnp.sum(t * t, axis=-1, keepdims=True) + EPS)


def dwconv_centred(x, w):
    width = w.shape[0]
    half = width // 2
    seq = x.shape[1]
    xp = jnp.pad(x, ((0, 0), (half, half), (0, 0)))
    return sum(xp[:, k:k + seq] * w[k] for k in range(width))


def neighbourhood_attention(h, w_qkv, w_o, rpb):
    bsz, seq, _ = h.shape
    rows = seq // GRID_W
    win_r = min(NA_WIN_R, rows)
    q, k, v = jnp.split(h @ w_qkv, 3, axis=-1)
    grid = lambda t: t.reshape(bsz, rows, GRID_W, NA_HEADS, NA_HEAD_DIM)
    q, k, v = grid(q) * NA_HEAD_DIM ** -0.5, grid(k), grid(v)
    row_start = np.clip(np.arange(rows) - win_r // 2, 0, rows - win_r)
    cols = np.arange(GRID_W)
    col_start = np.clip(cols - NA_WIN_C // 2, 0, GRID_W - NA_WIN_C)
    col_valid = (cols[None, :] >= col_start[:, None]) & (cols[None, :] < col_start[:, None] + NA_WIN_C)
    dc_idx = np.clip(cols[None, :] - cols[:, None] + NA_WIN_C - 1, 0, 2 * NA_WIN_C - 2)
    bias_c = jnp.where(col_valid, rpb[:, :, dc_idx].astype(jnp.float32), NEG_INF)
    bias_c = jnp.transpose(bias_c, (0, 2, 1, 3))
    dr_idx = row_start[:, None] + np.arange(win_r)[None, :] - np.arange(rows)[:, None] + NA_WIN_R - 1

    def one_row(args):
        q_r, r0, dri = args
        k_w = lax.dynamic_slice_in_dim(k, r0, win_r, axis=1)
        v_w = lax.dynamic_slice_in_dim(v, r0, win_r, axis=1)
        s = jnp.einsum('bqhd,bajhd->bhqaj', q_r, k_w, preferred_element_type=jnp.float32)
        s = s + jnp.take(bias_c, dri, axis=2)
        pr = jax.nn.softmax(s.reshape(bsz, NA_HEADS, GRID_W, win_r * GRID_W), axis=-1).reshape(s.shape)
        return jnp.einsum('bhqaj,bajhd->bqhd', pr.astype(v.dtype), v_w)

    out = lax.map(one_row, (jnp.moveaxis(q, 1, 0), jnp.asarray(row_start, jnp.int32), jnp.asarray(dr_idx, jnp.int32)))
    out = jnp.moveaxis(out, 0, 1).reshape(bsz, seq, D_MODEL)
    return out @ w_o


def spatial_gating(h, w_in, sg_norm, w_s, b_s, w_o):
    bsz, seq, _ = h.shape
    n = seq // SG_CHUNK
    u, v = jnp.split(jax.nn.gelu(h @ w_in), 2, axis=-1)
    v = rmsnorm(v, sg_norm).reshape(bsz, n, SG_CHUNK, SG_GROUPS, SG_GROUP_DIM)
    mixed = jnp.einsum('gts,bnsgc->bntgc', w_s, v) + b_s.T[:, :, None]
    return (u * mixed.reshape(bsz, seq, SG_WIDTH)) @ w_o


def gated_delta_scan(q, k, v, g, beta):
    bsz, seq = q.shape[:2]
    n = seq // GDN_CHUNK
    rep = GDN_V_HEADS // GDN_QK_HEADS
    tri_incl = np.tril(np.ones((GDN_CHUNK, GDN_CHUNK), bool))
    tri_strict = np.tril(np.ones((GDN_CHUNK, GDN_CHUNK), bool), -1)
    eye = jnp.eye(GDN_CHUNK, dtype=jnp.float32)

    def chunks(t):
        return jnp.moveaxis(t.reshape(bsz, n, GDN_CHUNK, *t.shape[2:]), 1, 0)

    def tr(t):
        return jnp.swapaxes(t, -1, -2)

    def step(state, inp):
        qc, kc, vc, gc, bc = inp
        qh = jnp.swapaxes(jnp.repeat(qc, rep, axis=2), 1, 2)
        kh = jnp.swapaxes(jnp.repeat(kc, rep, axis=2), 1, 2)
        vh = jnp.swapaxes(vc, 1, 2)
        gam = jnp.cumsum(jnp.swapaxes(gc, 1, 2), axis=-1)
        bet = jnp.swapaxes(bc, 1, 2)
        decay = jnp.exp(jnp.where(tri_incl, gam[..., :, None] - gam[..., None, :], -jnp.inf))
        m = jnp.where(tri_strict, (kh @ tr(kh)) * decay, 0.0) * bet[..., :, None]
        e_gam = jnp.exp(gam)
        rhs = jnp.concatenate([kh * (bet * e_gam)[..., None], vh * bet[..., None]], axis=-1)
        sol = lax.linalg.triangular_solve(eye + m, rhs, left_side=True, lower=True, unit_diagonal=True)
        w_mat, u_val = sol[..., :GDN_DK], sol[..., GDN_DK:]
        u = u_val - w_mat @ state
        o = (qh * e_gam[..., None]) @ state + ((qh @ tr(kh)) * decay) @ u
        k_dec = kh * jnp.exp(gam[..., -1:] - gam)[..., None]
        new_state = e_gam[..., -1][..., None, None] * state + tr(k_dec) @ u
        return new_state, o

    state0 = jnp.zeros((bsz, GDN_V_HEADS, GDN_DK, GDN_DV), jnp.float32)
    _, o = lax.scan(step, state0, (chunks(q), chunks(k), chunks(v), chunks(g), chunks(beta)))
    return jnp.transpose(o, (1, 0, 3, 2, 4)).reshape(bsz, seq, GDN_V_HEADS, GDN_DV)


def gated_deltanet(h, w_in, conv_w, a_log, dt_bias, out_norm, w_o):
    bsz, seq, _ = h.shape
    proj = h @ w_in
    qkv = jax.nn.silu(dwconv_centred(proj[..., :GDN_CONV_WIDTH], conv_w)).astype(jnp.float32)
    z = proj[..., GDN_CONV_WIDTH:GDN_CONV_WIDTH + GDN_V_WIDTH]
    ab = proj[..., GDN_CONV_WIDTH + GDN_V_WIDTH:].astype(jnp.float32).reshape(bsz, seq, 2, 2, GDN_V_HEADS)
    q = l2norm(qkv[..., :GDN_QK_WIDTH].reshape(bsz, seq, GDN_QK_HEADS, GDN_DK)) * GDN_DK ** -0.5
    k = l2norm(qkv[..., GDN_QK_WIDTH:2 * GDN_QK_WIDTH].reshape(bsz, seq, GDN_QK_HEADS, GDN_DK))
    v = qkv[..., 2 * GDN_QK_WIDTH:].reshape(bsz, seq, GDN_V_HEADS, GDN_DV)
    decay_rate = jnp.exp(a_log.astype(jnp.float32))
    g = -decay_rate * jax.nn.softplus(ab[:, :, :, 0] + dt_bias.astype(jnp.float32))
    beta = jax.nn.sigmoid(ab[:, :, :, 1])
    o_fwd = gated_delta_scan(q, k, v, g[:, :, 0], beta[:, :, 0])
    o_bwd = jnp.flip(gated_delta_scan(jnp.flip(q, 1), jnp.flip(k, 1), jnp.flip(v, 1),
                                      jnp.flip(g[:, :, 1], 1), jnp.flip(beta[:, :, 1], 1)), 1)
    zg = jax.nn.silu(z.astype(jnp.float32).reshape(bsz, seq, GDN_V_HEADS, GDN_DV))
    o = rmsnorm(o_fwd + o_bwd, out_norm) * zg
    return o.reshape(bsz, seq, GDN_V_WIDTH).astype(h.dtype) @ w_o


def s5_direction(u, a_re, a_im, log_dt, b_re, b_im, c_re, c_im):
    bsz, seq = u.shape[:2]
    n = seq // S5_CHUNK
    f32 = jnp.float32
    lam = lax.complex(a_re.astype(f32), a_im.astype(f32))
    dt = jnp.exp(log_dt.astype(f32))[:, None]
    a_bar = jnp.exp(lam * dt)
    b_bar = ((a_bar - 1.0) / lam)[..., None] * lax.complex(b_re.astype(f32), b_im.astype(f32))
    c = lax.complex(c_re.astype(f32), c_im.astype(f32))
    bu = jnp.einsum('gpc,bsgc->bsgp', b_bar, u.astype(jnp.complex64))
    bu = jnp.moveaxis(bu.reshape(bsz, n, S5_CHUNK, S5_GROUPS, S5_STATE), 1, 0)
    a_elems = jnp.broadcast_to(a_bar, (bsz, S5_CHUNK, S5_GROUPS, S5_STATE))
    powers = jnp.exp(lam[None] * dt[None] * jnp.arange(1, S5_CHUNK + 1, dtype=f32)[:, None, None])

    def binop(e1, e2):
        return (e2[0] * e1[0], e2[0] * e1[1] + e2[1])

    def step(x_prev, bu_c):
        _, xs = lax.associative_scan(binop, (a_elems, bu_c), axis=1)
        xs = xs + powers[None] * x_prev[:, None]
        y = jnp.einsum('gcp,blgp->blgc', c, xs).real
        return xs[:, -1], y

    x0 = jnp.zeros((bsz, S5_GROUPS, S5_STATE), jnp.complex64)
    _, ys = lax.scan(step, x0, bu)
    return jnp.moveaxis(ys, 0, 1).reshape(bsz, seq, S5_GROUPS, S5_GROUP_DIM)


def s5_mixer(h, a_re, a_im, log_dt, b_re, b_im, c_re, c_im, d_skip, w_glu):
    bsz, seq, _ = h.shape
    hf = h.astype(jnp.float32)
    u = hf.reshape(bsz, seq, S5_GROUPS, S5_GROUP_DIM)
    y_f = s5_direction(u, a_re[0], a_im[0], log_dt[0], b_re[0], b_im[0], c_re[0], c_im[0])
    y_b = jnp.flip(s5_direction(jnp.flip(u, 1), a_re[1], a_im[1], log_dt[1], b_re[1], b_im[1], c_re[1], c_im[1]), 1)
    y = (y_f + y_b).reshape(bsz, seq, D_MODEL) + d_skip * hf
    y = jax.nn.gelu(y).astype(h.dtype)
    a, b = jnp.split(y @ w_glu, 2, axis=-1)
    return a * jax.nn.sigmoid(b)


def conv_glu_ffn(h, w_gu, conv_w, conv_b, w_down):
    gate, up = jnp.split(h @ w_gu, 2, axis=-1)
    gate = dwconv_centred(gate, conv_w) + conv_b
    return (jax.nn.silu(gate) * up) @ w_down


def trunk(x, p, w):
    for i in range(DEPTH):
        kind, j = i % N_MIXERS, i // N_MIXERS
        h = rmsnorm(x, w['norm_mix'][i])
        if kind == 0:
            mix = neighbourhood_attention(h, w['na_w_qkv'][j], w['na_w_o'][j], w['na_rpb'][j])
        elif kind == 1:
            mix = spatial_gating(h, w['sg_w_in'][j], w['sg_norm'][j], w['sg_w_s'][j], w['sg_b_s'][j], w['sg_w_o'][j])
        elif kind == 2:
            mix = gated_deltanet(h, w['gdn_w_in'][j], w['gdn_conv_w'][j], w['gdn_a_log'][j], w['gdn_dt_bias'][j],
                                 w['gdn_out_norm'][j], w['gdn_w_o'][j])
        else:
            mix = s5_mixer(h, w['s5_a_re'][j], w['s5_a_im'][j], w['s5_log_dt'][j], w['s5_b_re'][j], w['s5_b_im'][j],
                           w['s5_c_re'][j], w['s5_c_im'][j], w['s5_d'][j], w['s5_w_glu'][j])
        x = x + mix
        x = x + conv_glu_ffn(rmsnorm(x, w['norm_ffn'][i]), w['ffn_w_gu'][i], w['ffn_conv_w'][i],
                             w['ffn_conv_b'][i], w['ffn_w_down'][i])
        gate = jax.nn.sigmoid(rmsnorm(x, w['norm_ple'][i]) @ w['ple_w_gate'][i])
        x = x + gate * (p[i] @ w['ple_w_proj'][i])
    return rmsnorm(x, w['final_norm'])


def setup_inputs(seed: int = 0) -> dict:
    key = jax.random.key(seed)
    keys = iter(list(jax.random.split(key, 64)))
    f32 = jnp.float32

    def normal(shape, scale):
        return jax.random.normal(next(keys), shape, f32) * scale

    def uniform(shape, lo, hi):
        return jax.random.uniform(next(keys), shape, f32, lo, hi)

    def gain(shape):
        return 1.0 + normal(shape, 0.02)

    d = D_MODEL
    inp = {}
    inp['x_prompt'] = normal((BATCH, SEQ, d), 1.0)
    inp['x_sample'] = normal((DEC_BATCH, DEC_SEQ, d), 1.0)
    inp['p_prompt'] = normal((DEPTH, BATCH, SEQ, PLE_DIM), 1.0)
    inp['p_sample'] = normal((DEPTH, DEC_BATCH, DEC_SEQ, PLE_DIM), 1.0)
    inp['norm_mix'] = gain((DEPTH, d))
    inp['norm_ffn'] = gain((DEPTH, d))
    inp['norm_ple'] = gain((DEPTH, d))
    inp['final_norm'] = gain((d,))
    inp['na_w_qkv'] = normal((N_NA, d, 3 * d), d ** -0.5)
    inp['na_w_o'] = normal((N_NA, d, d), d ** -0.5)
    inp['na_rpb'] = normal((N_NA, NA_HEADS, 2 * NA_WIN_R - 1, 2 * NA_WIN_C - 1), 0.1)
    inp['sg_w_in'] = normal((N_SG, d, 2 * SG_WIDTH), d ** -0.5)
    inp['sg_norm'] = gain((N_SG, SG_WIDTH))
    inp['sg_w_s'] = normal((N_SG, SG_GROUPS, SG_CHUNK, SG_CHUNK), SG_CHUNK ** -0.5)
    inp['sg_b_s'] = gain((N_SG, SG_GROUPS, SG_CHUNK))
    inp['sg_w_o'] = normal((N_SG, SG_WIDTH, d), SG_WIDTH ** -0.5)
    inp['gdn_w_in'] = normal((N_GDN, d, GDN_IN_WIDTH), d ** -0.5)
    inp['gdn_conv_w'] = normal((N_GDN, GDN_CONV, GDN_CONV_WIDTH), GDN_CONV ** -0.5)
    inp['gdn_a_log'] = jnp.log(uniform((N_GDN, 2, GDN_V_HEADS), 1.0, 16.0))
    dt = jnp.exp(uniform((N_GDN, 2, GDN_V_HEADS), math.log(1e-3), math.log(1e-1)))
    inp['gdn_dt_bias'] = dt + jnp.log(-jnp.expm1(-dt))
    inp['gdn_out_norm'] = gain((N_GDN, GDN_DV))
    inp['gdn_w_o'] = normal((N_GDN, GDN_V_WIDTH, d), GDN_V_WIDTH ** -0.5)
    a_shape = (N_S5, 2, S5_GROUPS, S5_STATE)
    inp['s5_a_re'] = -0.5 + normal(a_shape, 0.01)
    inp['s5_a_im'] = jnp.broadcast_to(math.pi * jnp.arange(S5_STATE, dtype=f32), a_shape) + normal(a_shape, 0.01)
    inp['s5_log_dt'] = uniform((N_S5, 2, S5_GROUPS), math.log(1e-3), math.log(1e-1))
    inp['s5_b_re'] = normal((N_S5, 2, S5_GROUPS, S5_STATE, S5_GROUP_DIM), (2 * S5_GROUP_DIM) ** -0.5)
    inp['s5_b_im'] = normal((N_S5, 2, S5_GROUPS, S5_STATE, S5_GROUP_DIM), (2 * S5_GROUP_DIM) ** -0.5)
    inp['s5_c_re'] = normal((N_S5, 2, S5_GROUPS, S5_GROUP_DIM, S5_STATE), (2 * S5_STATE) ** -0.5)
    inp['s5_c_im'] = normal((N_S5, 2, S5_GROUPS, S5_GROUP_DIM, S5_STATE), (2 * S5_STATE) ** -0.5)
    inp['s5_d'] = normal((N_S5, d), 1.0)
    inp['s5_w_glu'] = normal((N_S5, d, 2 * d), d ** -0.5)
    inp['ffn_w_gu'] = normal((DEPTH, d, 2 * D_FF), d ** -0.5)
    inp['ffn_conv_w'] = normal((DEPTH, FFN_CONV, D_FF), FFN_CONV ** -0.5)
    inp['ffn_conv_b'] = normal((DEPTH, D_FF), 0.02)
    inp['ffn_w_down'] = normal((DEPTH, D_FF, d), D_FF ** -0.5)
    inp['ple_w_proj'] = normal((DEPTH, PLE_DIM, d), PLE_DIM ** -0.5)
    inp['ple_w_gate'] = normal((DEPTH, d, d), d ** -0.5)
    return inp


def reference(x_prompt, x_sample, p_prompt, p_sample,
              norm_mix, norm_ffn, norm_ple, final_norm,
              na_w_qkv, na_w_o, na_rpb,
              sg_w_in, sg_norm, sg_w_s, sg_b_s, sg_w_o,
              gdn_w_in, gdn_conv_w, gdn_a_log, gdn_dt_bias, gdn_out_norm, gdn_w_o,
              s5_a_re, s5_a_im, s5_log_dt, s5_b_re, s5_b_im, s5_c_re, s5_c_im, s5_d, s5_w_glu,
              ffn_w_gu, ffn_conv_w, ffn_conv_b, ffn_w_down,
              ple_w_proj, ple_w_gate):
    w = dict(norm_mix=norm_mix, norm_ffn=norm_ffn, norm_ple=norm_ple, final_norm=final_norm,
             na_w_qkv=na_w_qkv, na_w_o=na_w_o, na_rpb=na_rpb,
             sg_w_in=sg_w_in, sg_norm=sg_norm, sg_w_s=sg_w_s, sg_b_s=sg_b_s, sg_w_o=sg_w_o,
             gdn_w_in=gdn_w_in, gdn_conv_w=gdn_conv_w, gdn_a_log=gdn_a_log, gdn_dt_bias=gdn_dt_bias,
             gdn_out_norm=gdn_out_norm, gdn_w_o=gdn_w_o,
             s5_a_re=s5_a_re, s5_a_im=s5_a_im, s5_log_dt=s5_log_dt, s5_b_re=s5_b_re, s5_b_im=s5_b_im,
             s5_c_re=s5_c_re, s5_c_im=s5_c_im, s5_d=s5_d, s5_w_glu=s5_w_glu,
             ffn_w_gu=ffn_w_gu, ffn_conv_w=ffn_conv_w, ffn_conv_b=ffn_conv_b, ffn_w_down=ffn_w_down,
             ple_w_proj=ple_w_proj, ple_w_gate=ple_w_gate)
    y_prompt = trunk(x_prompt, p_prompt, w)
    y_sample = trunk(x_sample, p_sample, w)
    return (y_prompt, y_sample)
```

```python
import functools
import math

import jax
import jax.numpy as jnp
import numpy as np
from jax import lax
from jax.experimental import pallas as pl
from jax.experimental.pallas import tpu as pltpu

F32 = jnp.float32
BF16 = jnp.bfloat16

EPS = 1e-6
NEG_INF = -1e30
GRID_W = 64
NA_HEADS = 16
NA_WIN_R = 8
NA_WIN_C = 16
SG_CHUNK = 128
SG_GROUPS = 16
GDN_QK_HEADS = 16
GDN_V_HEADS = 32
GDN_DK = 128
GDN_DV = 128
GDN_CHUNK = 64
S5_GROUP_DIM = 16
S5_STATE = 64
S5_CHUNK = 128
N_MIXERS = 4

VMEM_LIMIT_BYTES = 56 * 1024 * 1024
HALO = 16


def _params(*sem):
    return pltpu.CompilerParams(dimension_semantics=sem, vmem_limit_bytes=VMEM_LIMIT_BYTES)


def _rms(x, g):
    return x * lax.rsqrt(jnp.mean(x * x, axis=-1, keepdims=True) + EPS) * g


def _gelu_tanh(x):
    return 0.5 * x * (1.0 + jnp.tanh(math.sqrt(2.0 / math.pi) * (x + 0.044715 * (x * x * x))))


def _sigmoid(x):
    return 1.0 / (1.0 + jnp.exp(-x))


def _rmsnorm_kernel(x_ref, g_ref, o_ref):
    o_ref[...] = _rms(x_ref[...], g_ref[...]).astype(o_ref.dtype)


def rmsnorm(x, g, out_dtype=F32, tm=512, row0=0, rows=None):
    m, d = x.shape
    rows = m if rows is None else rows
    off = row0 // tm
    return pl.pallas_call(
        _rmsnorm_kernel,
        out_shape=jax.ShapeDtypeStruct((rows, d), out_dtype),
        grid=(rows // tm,),
        in_specs=[pl.BlockSpec((tm, d), lambda i: (i + off, 0)), pl.BlockSpec((1, d), lambda i: (0, 0))],
        out_specs=pl.BlockSpec((tm, d), lambda i: (i, 0)),
        compiler_params=_params("parallel"),
        name="rmsnorm",
    )(x, g.reshape(1, d))


def _norm_matmul_kernel(x_ref, g_ref, w_ref, o_ref, hn_ref, *, act):
    @pl.when(pl.program_id(1) == 0)
    def _():
        hn_ref[...] = _rms(x_ref[...], g_ref[...]).astype(BF16)

    y = jnp.dot(hn_ref[...], w_ref[...], preferred_element_type=F32)
    if act == "gelu":
        y = _gelu_tanh(y)
    o_ref[...] = y.astype(o_ref.dtype)


def norm_matmul(x, g, w, act=None, out_dtype=F32, tm=512, tn=512):
    m, d = x.shape
    n = w.shape[1]
    if n % tn:
        tn = 128
    return pl.pallas_call(
        functools.partial(_norm_matmul_kernel, act=act),
        out_shape=jax.ShapeDtypeStruct((m, n), out_dtype),
        grid=(m // tm, n // tn),
        in_specs=[
            pl.BlockSpec((tm, d), lambda i, j: (i, 0)),
            pl.BlockSpec((1, d), lambda i, j: (0, 0)),
            pl.BlockSpec((d, tn), lambda i, j: (0, j)),
        ],
        out_specs=pl.BlockSpec((tm, tn), lambda i, j: (i, j)),
        scratch_shapes=[pltpu.VMEM((tm, d), BF16)],
        compiler_params=_params("parallel", "arbitrary"),
        name="norm_matmul",
    )(x, g.reshape(1, d), w)


def _matmul_res_kernel(a_ref, w_ref, r_ref, o_ref):
    o_ref[...] = r_ref[...] + jnp.dot(a_ref[...].astype(BF16), w_ref[...], preferred_element_type=F32)


def matmul_res(a, w, res, tm=512, tn=512):
    m, k = a.shape
    n = w.shape[1]
    return pl.pallas_call(
        _matmul_res_kernel,
        out_shape=jax.ShapeDtypeStruct((m, n), F32),
        grid=(m // tm, n // tn),
        in_specs=[
            pl.BlockSpec((tm, k), lambda i, j: (i, 0)),
            pl.BlockSpec((k, tn), lambda i, j: (0, j)),
            pl.BlockSpec((tm, tn), lambda i, j: (i, j)),
        ],
        out_specs=pl.BlockSpec((tm, tn), lambda i, j: (i, j)),
        compiler_params=_params("parallel", "arbitrary"),
        name="matmul_res",
    )(a, w, res)


def _glu_res_kernel(a_ref, wa_ref, wb_ref, r_ref, o_ref):
    a = a_ref[...].astype(BF16)
    ya = jnp.dot(a, wa_ref[...], preferred_element_type=F32)
    yb = jnp.dot(a, wb_ref[...], preferred_element_type=F32)
    o_ref[...] = r_ref[...] + ya * _sigmoid(yb)


def glu_res(a, w, res, tm=512, tn=512):
    m, k = a.shape
    n = w.shape[1] // 2
    nb = n // tn
    return pl.pallas_call(
        _glu_res_kernel,
        out_shape=jax.ShapeDtypeStruct((m, n), F32),
        grid=(m // tm, nb),
        in_specs=[
            pl.BlockSpec((tm, k), lambda i, j: (i, 0)),
            pl.BlockSpec((k, tn), lambda i, j: (0, j)),
            pl.BlockSpec((k, tn), lambda i, j: (0, j + nb)),
            pl.BlockSpec((tm, tn), lambda i, j: (i, j)),
        ],
        out_specs=pl.BlockSpec((tm, tn), lambda i, j: (i, j)),
        compiler_params=_params("parallel", "arbitrary"),
        name="glu_res",
    )(a, w, w, res)


def _ffn_kernel(x_ref, xp_ref, xn_ref, g_ref, wg_ref, wu_ref, cw_ref, cb_ref, wd_ref, o_ref, hn_ref, acc_ref,
                *, tm, seq):
    i = pl.program_id(0)
    j = pl.program_id(1)

    @pl.when(j == 0)
    def _():
        g = g_ref[...]
        prev_ok = jnp.where((i * tm) % seq != 0, 1.0, 0.0)
        next_ok = jnp.where(((i + 1) * tm) % seq != 0, 1.0, 0.0)
        hn_ref[0:HALO, :] = (_rms(xp_ref[...], g) * prev_ok).astype(BF16)
        hn_ref[HALO:HALO + tm, :] = _rms(x_ref[...], g).astype(BF16)
        hn_ref[HALO + tm:, :] = (_rms(xn_ref[...], g) * next_ok).astype(BF16)
        acc_ref[...] = jnp.zeros_like(acc_ref)

    rows = tm + 2 * HALO
    gate = jnp.dot(hn_ref[...], wg_ref[...], preferred_element_type=F32)
    up = jnp.dot(hn_ref[HALO:HALO + tm, :], wu_ref[...], preferred_element_type=F32)
    cw = cw_ref[...]
    g_prev = pltpu.roll(gate, 1, 0)[HALO:HALO + tm]
    g_next = pltpu.roll(gate, rows - 1, 0)[HALO:HALO + tm]
    gc = cw[0:1] * g_prev + cw[1:2] * gate[HALO:HALO + tm] + cw[2:3] * g_next + cb_ref[...]
    act = (gc * _sigmoid(gc) * up).astype(BF16)
    acc_ref[...] += jnp.dot(act, wd_ref[...], preferred_element_type=F32)

    @pl.when(j == pl.num_programs(1) - 1)
    def _():
        o_ref[...] = x_ref[...] + acc_ref[...]


def ffn(x, g, w_gu, conv_w, conv_b, w_down, seq, tm=512, tf=512):
    m, d = x.shape
    f = w_down.shape[0]
    nf = f // tf
    hb = tm // HALO
    last = m // HALO - 1
    return pl.pallas_call(
        functools.partial(_ffn_kernel, tm=tm, seq=seq),
        out_shape=jax.ShapeDtypeStruct((m, d), F32),
        grid=(m // tm, nf),
        in_specs=[
            pl.BlockSpec((tm, d), lambda i, j: (i, 0)),
            pl.BlockSpec((HALO, d), lambda i, j: (jnp.maximum(i * hb - 1, 0), 0)),
            pl.BlockSpec((HALO, d), lambda i, j: (jnp.minimum((i + 1) * hb, last), 0)),
            pl.BlockSpec((1, d), lambda i, j: (0, 0)),
            pl.BlockSpec((d, tf), lambda i, j: (0, j)),
            pl.BlockSpec((d, tf), lambda i, j: (0, j + nf)),
            pl.BlockSpec((3, tf), lambda i, j: (0, j)),
            pl.BlockSpec((1, tf), lambda i, j: (0, j)),
            pl.BlockSpec((tf, d), lambda i, j: (j, 0)),
        ],
        out_specs=pl.BlockSpec((tm, d), lambda i, j: (i, 0)),
        scratch_shapes=[pltpu.VMEM((tm + 2 * HALO, d), BF16), pltpu.VMEM((tm, d), F32)],
        compiler_params=_params("parallel", "arbitrary"),
        name="ffn",
    )(x, x, x, g.reshape(1, d), w_gu, w_gu, conv_w, conv_b.reshape(1, f), w_down)


def _ple_kernel(x_ref, p_ref, g_ref, wg_ref, wp_ref, o_ref):
    x = x_ref[...]
    hn = _rms(x, g_ref[...]).astype(BF16)
    gate = _sigmoid(jnp.dot(hn, wg_ref[...], preferred_element_type=F32))
    proj = jnp.dot(p_ref[...].astype(BF16), wp_ref[...], preferred_element_type=F32)
    o_ref[...] = x + gate * proj


def ple(x, p, g, w_gate, w_proj, tm=512):
    m, d = x.shape
    pd = p.shape[1]
    return pl.pallas_call(
        _ple_kernel,
        out_shape=jax.ShapeDtypeStruct((m, d), F32),
        grid=(m // tm,),
        in_specs=[
            pl.BlockSpec((tm, d), lambda i: (i, 0)),
            pl.BlockSpec((tm, pd), lambda i: (i, 0)),
            pl.BlockSpec((1, d), lambda i: (0, 0)),
            pl.BlockSpec((d, d), lambda i: (0, 0)),
            pl.BlockSpec((pd, d), lambda i: (0, 0)),
        ],
        out_specs=pl.BlockSpec((tm, d), lambda i: (i, 0)),
        compiler_params=_params("parallel"),
        name="ple",
    )(x, p, g.reshape(1, d), w_gate, w_proj)


GDN_BLOCK = 256


def _conv_in_kernel(x_ref, xp_ref, xn_ref, g_ref, w_ref, cw_ref, o_ref, hn_ref, *, tm, seq, nq, nqk, scale):
    i = pl.program_id(0)
    j = pl.program_id(1)

    @pl.when(j == 0)
    def _():
        g = g_ref[...]
        prev_ok = jnp.where((i * tm) % seq != 0, 1.0, 0.0)
        next_ok = jnp.where(((i + 1) * tm) % seq != 0, 1.0, 0.0)
        hn_ref[0:HALO, :] = (_rms(xp_ref[...], g) * prev_ok).astype(BF16)
        hn_ref[HALO:HALO + tm, :] = _rms(x_ref[...], g).astype(BF16)
        hn_ref[HALO + tm:, :] = (_rms(xn_ref[...], g) * next_ok).astype(BF16)

    rows = tm + 2 * HALO
    y = jnp.dot(hn_ref[...], w_ref[...], preferred_element_type=F32)
    cw = cw_ref[...]
    y_prev = pltpu.roll(y, 1, 0)[HALO:HALO + tm]
    y_next = pltpu.roll(y, rows - 1, 0)[HALO:HALO + tm]
    c = cw[0:1] * y_prev + cw[1:2] * y[HALO:HALO + tm] + cw[2:3] * y_next
    c = c * _sigmoid(c)
    tn = c.shape[1]
    for s in range(tn // 128):
        cs = c[:, s * 128:(s + 1) * 128]
        inv = lax.rsqrt(jnp.sum(cs * cs, axis=-1, keepdims=True) + EPS)
        f = jnp.where(j < nq, inv * scale, jnp.where(j < nqk, inv, 1.0))
        o_ref[:, s * 128:(s + 1) * 128] = (cs * f).astype(o_ref.dtype)


def conv_in(x, g, w, conv_w, seq, n_q, n_qk, scale, tm=512, tn=256):
    m, d = x.shape
    n = w.shape[1]
    hb = tm // HALO
    last = m // HALO - 1
    return pl.pallas_call(
        functools.partial(_conv_in_kernel, tm=tm, seq=seq, nq=n_q // tn, nqk=n_qk // tn, scale=scale),
        out_shape=jax.ShapeDtypeStruct((m, n), BF16),
        grid=(m // tm, n // tn),
        in_specs=[
            pl.BlockSpec((tm, d), lambda i, j: (i, 0)),
            pl.BlockSpec((HALO, d), lambda i, j: (jnp.maximum(i * hb - 1, 0), 0)),
            pl.BlockSpec((HALO, d), lambda i, j: (jnp.minimum((i + 1) * hb, last), 0)),
            pl.BlockSpec((1, d), lambda i, j: (0, 0)),
            pl.BlockSpec((d, tn), lambda i, j: (0, j)),
            pl.BlockSpec((3, tn), lambda i, j: (0, j)),
        ],
        out_specs=pl.BlockSpec((tm, tn), lambda i, j: (i, j)),
        scratch_shapes=[pltpu.VMEM((tm + 2 * HALO, d), BF16)],
        compiler_params=_params("parallel", "arbitrary"),
        name="gdn_conv_in",
    )(x, x, x, g.reshape(1, d), w, conv_w)


def _gdn_gates_kernel(x_ref, g_ref, w_ref, alog_ref, bias_ref, isg_ref, o_ref):
    hn = _rms(x_ref[...], g_ref[...]).astype(BF16)
    y = jnp.dot(hn, w_ref[...], preferred_element_type=F32)
    t = y + bias_ref[...]
    softplus = jnp.maximum(t, 0.0) + jnp.log1p(jnp.exp(-jnp.abs(t)))
    gval = -jnp.exp(alog_ref[...]) * softplus
    o_ref[...] = jnp.where(isg_ref[...] > 0.5, gval, _sigmoid(y))


def gdn_gates(x, g, w_ab, a_log, dt_bias, tm=512):
    m, d = x.shape
    n = w_ab.shape[1]
    nh = a_log.shape[-1]
    zeros = jnp.zeros((2, 1, nh), F32)
    arrange = lambda t: jnp.concatenate([t.reshape(2, 1, nh).astype(F32), zeros], axis=1).reshape(1, n)
    isg = jnp.concatenate([jnp.ones((2, 1, nh), F32), zeros], axis=1).reshape(1, n)
    return pl.pallas_call(
        _gdn_gates_kernel,
        out_shape=jax.ShapeDtypeStruct((m, n), F32),
        grid=(m // tm,),
        in_specs=[
            pl.BlockSpec((tm, d), lambda i: (i, 0)),
            pl.BlockSpec((1, d), lambda i: (0, 0)),
            pl.BlockSpec((d, n), lambda i: (0, 0)),
            pl.BlockSpec((1, n), lambda i: (0, 0)),
            pl.BlockSpec((1, n), lambda i: (0, 0)),
            pl.BlockSpec((1, n), lambda i: (0, 0)),
        ],
        out_specs=pl.BlockSpec((tm, n), lambda i: (i, 0)),
        compiler_params=_params("parallel"),
        name="gdn_gates",
    )(x, g.reshape(1, d), w_ab, arrange(a_log), arrange(dt_bias), isg)


def _gdn_direction(q_ref, k_ref, kt_ref, v_ref, gc_ref, gr_ref, o_ref, s_ref, rev):
    c_sz, lc = GDN_BLOCK, GDN_CHUNK
    nchunk = c_sz // lc
    hp = lax.Precision.HIGHEST
    d = 1 if rev else 0
    ii = lax.broadcasted_iota(jnp.int32, (c_sz, c_sz), 0)
    jj = lax.broadcasted_iota(jnp.int32, (c_sz, c_sz), 1)
    sh = lc.bit_length() - 1
    same = (ii >> sh) == (jj >> sh)
    incl = jnp.logical_and(same, (jj >= ii) if rev else (jj <= ii))
    eye = ii == jj
    cum = jnp.where(incl, 1.0, 0.0)
    blk = jnp.where(same, 1.0, 0.0)
    q = q_ref[...]
    k = k_ref[...]
    kf = k.astype(F32)
    qf = q.astype(F32)
    ktf = kt_ref[0].astype(F32)
    gc = gc_ref[0, 0][:, 4 * d:4 * d + 4]
    gr = gr_ref[0, 0][4 * d:4 * d + 4, :]
    gam_c = jnp.dot(cum, gc[:, 0:2], precision=hp, preferred_element_type=F32)
    gam_r = lax.dot_general(gr[0:2], cum, (((1,), (1,)), ((), ())), precision=hp,
                            preferred_element_type=F32)
    end_r = jnp.dot(gr[0:2], blk, precision=hp, preferred_element_type=F32)
    gram = lax.dot_general(k, k, (((1,), (1,)), ((), ())), preferred_element_type=F32)
    qk = lax.dot_general(q, k, (((1,), (1,)), ((), ())), preferred_element_type=F32)
    jrow = lax.broadcasted_iota(jnp.int32, (1, c_sz), 1) >> sh
    for hs in range(2):
        gcol = gam_c[:, hs:hs + 1]
        grow = gam_r[hs:hs + 1, :]
        bcol = gc[:, 2 + hs:3 + hs]
        dec = jnp.exp(jnp.where(incl, gcol - grow, NEG_INF))
        n = -(gram * jnp.where(eye, 0.0, dec)) * bcol
        p = jnp.where(eye, 1.0, 0.0) + n
        nb = n.astype(BF16)
        npow = jnp.dot(nb, nb, preferred_element_type=F32)
        for lvl in range(5):
            nb = npow.astype(BF16)
            p = p + jnp.dot(nb, p.astype(BF16), preferred_element_type=F32)
            if lvl < 4:
                npow = jnp.dot(nb, nb, preferred_element_type=F32)
        e_g = jnp.exp(gcol)
        vh = v_ref[:, hs * GDN_DV:(hs + 1) * GDN_DV].astype(F32)
        rhs = jnp.concatenate([kf * (bcol * e_g), vh * bcol], axis=1).astype(BF16)
        wu = jnp.dot(p.astype(BF16), rhs, preferred_element_type=F32)
        wub = wu.astype(BF16)
        awu = jnp.dot((qk * dec).astype(BF16), wub, preferred_element_type=F32)
        qeff = (qf * e_g - awu[:, :GDN_DK]).astype(BF16)
        o_in = awu[:, GDN_DK:]
        erow = jnp.exp(end_r[hs:hs + 1, :] - grow)
        kdt = ktf * erow
        s = s_ref[2 * d + hs]
        order = range(nchunk - 1, -1, -1) if rev else range(nchunk)
        for c in order:
            r0 = c * lc
            kdt_c = jnp.where(jrow == c, kdt, 0.0).astype(BF16)
            kwu = jnp.dot(kdt_c, wub, preferred_element_type=F32)
            sb = s.astype(BF16)
            o_c = o_in[r0:r0 + lc] + jnp.dot(qeff[r0:r0 + lc], sb, preferred_element_type=F32)
            o_ref[r0:r0 + lc, hs * GDN_DV:(hs + 1) * GDN_DV] = o_c.astype(o_ref.dtype)
            e_end = jnp.exp(end_r[hs:hs + 1, r0:r0 + 1])
            s = e_end * s - jnp.dot(kwu[:, :GDN_DK].astype(BF16), sb, preferred_element_type=F32) + kwu[:, GDN_DK:]
        s_ref[2 * d + hs] = s


def _gdn_scan_kernel(qf_ref, kf_ref, ktf_ref, vf_ref, gcf_ref, grf_ref,
                     qb_ref, kb_ref, ktb_ref, vb_ref, gcb_ref, grb_ref, of_ref, ob_ref, s_ref):
    @pl.when(pl.program_id(2) == 0)
    def _():
        s_ref[...] = jnp.zeros_like(s_ref)

    _gdn_direction(qf_ref, kf_ref, ktf_ref, vf_ref, gcf_ref, grf_ref, of_ref, s_ref, False)
    _gdn_direction(qb_ref, kb_ref, ktb_ref, vb_ref, gcb_ref, grb_ref, ob_ref, s_ref, True)


def gdn_scan(qkv, kt, gcol, grow, bsz, seq):
    c_sz = GDN_BLOCK
    nb = seq // c_sz
    hq = GDN_QK_HEADS
    kcol = hq
    vcol = (2 * hq * GDN_DK) // (2 * GDN_DV)
    fwd = lambda b, h, c: c
    bwd = lambda b, h, c: nb - 1 - c

    def specs(pos):
        return [
            pl.BlockSpec((c_sz, GDN_DK), lambda b, h, c: (b * nb + pos(b, h, c), h)),
            pl.BlockSpec((c_sz, GDN_DK), lambda b, h, c: (b * nb + pos(b, h, c), kcol + h)),
            pl.BlockSpec((1, GDN_DK, c_sz), lambda b, h, c: (b, h, pos(b, h, c))),
            pl.BlockSpec((c_sz, 2 * GDN_DV), lambda b, h, c: (b * nb + pos(b, h, c), vcol + h)),
            pl.BlockSpec((1, 1, c_sz, 8), lambda b, h, c: (b, h, pos(b, h, c), 0)),
            pl.BlockSpec((1, 1, 8, c_sz), lambda b, h, c: (b, h, 0, pos(b, h, c))),
        ]

    out = jax.ShapeDtypeStruct((bsz * seq, GDN_V_HEADS * GDN_DV), BF16)
    return pl.pallas_call(
        _gdn_scan_kernel,
        out_shape=(out, out),
        grid=(bsz, hq, nb),
        in_specs=specs(fwd) + specs(bwd),
        out_specs=(
            pl.BlockSpec((c_sz, 2 * GDN_DV), lambda b, h, c: (b * nb + c, h)),
            pl.BlockSpec((c_sz, 2 * GDN_DV), lambda b, h, c: (b * nb + nb - 1 - c, h)),
        ),
        scratch_shapes=[pltpu.VMEM((4, GDN_DK, GDN_DV), F32)],
        compiler_params=_params("parallel", "parallel", "arbitrary"),
        name="gdn_scan",
    )(qkv, qkv, kt, qkv, gcol, grow, qkv, qkv, kt, qkv, gcol, grow)


def _gdn_out_kernel(of_ref, ob_ref, z_ref, gn_ref, w_ref, r_ref, o_ref, a_ref):
    @pl.when(pl.program_id(1) == 0)
    def _():
        gn = gn_ref[...]
        for h in range(GDN_V_HEADS):
            sl = slice(h * GDN_DV, (h + 1) * GDN_DV)
            o = of_ref[:, sl].astype(F32) + ob_ref[:, sl].astype(F32)
            z = z_ref[:, sl].astype(F32)
            a_ref[:, sl] = (_rms(o, gn) * (z * _sigmoid(z))).astype(BF16)

    o_ref[...] = r_ref[...] + jnp.dot(a_ref[...], w_ref[...], preferred_element_type=F32)


def gdn_out(o_f, o_b, z, out_norm, w_o, res, tm=512, tn=512):
    m, k = o_f.shape
    n = w_o.shape[1]
    return pl.pallas_call(
        _gdn_out_kernel,
        out_shape=jax.ShapeDtypeStruct((m, n), F32),
        grid=(m // tm, n // tn),
        in_specs=[
            pl.BlockSpec((tm, k), lambda i, j: (i, 0)),
            pl.BlockSpec((tm, k), lambda i, j: (i, 0)),
            pl.BlockSpec((tm, k), lambda i, j: (i, 0)),
            pl.BlockSpec((1, GDN_DV), lambda i, j: (0, 0)),
            pl.BlockSpec((k, tn), lambda i, j: (0, j)),
            pl.BlockSpec((tm, tn), lambda i, j: (i, j)),
        ],
        out_specs=pl.BlockSpec((tm, tn), lambda i, j: (i, j)),
        scratch_shapes=[pltpu.VMEM((tm, k), BF16)],
        compiler_params=_params("parallel", "arbitrary"),
        name="gdn_out",
    )(o_f, o_b, z, out_norm.reshape(1, GDN_DV), w_o, res)


def gated_deltanet(x, gmix, w_in, conv_w, a_log, dt_bias, out_norm, w_o, bsz, seq):
    qk_w = GDN_QK_HEADS * GDN_DK
    v_w = GDN_V_HEADS * GDN_DV
    cw = 2 * qk_w + v_w
    w_in = w_in.astype(BF16)
    qkv = conv_in(x, gmix, w_in[:, :cw], conv_w, seq, qk_w, 2 * qk_w, GDN_DK ** -0.5)
    z = norm_matmul(x, gmix, w_in[:, cw:cw + v_w], out_dtype=BF16)
    gb = gdn_gates(x, gmix, w_in[:, cw + v_w:], a_log, dt_bias)
    gcol = gb.reshape(bsz, seq, 2, 2, GDN_QK_HEADS, 2).transpose(0, 4, 1, 2, 3, 5).reshape(bsz, GDN_QK_HEADS, seq, 8)
    grow = jnp.swapaxes(gcol, 2, 3)
    kt = jnp.swapaxes(qkv[:, qk_w:2 * qk_w].reshape(bsz, seq, qk_w), 1, 2)
    o_f, o_b = gdn_scan(qkv, kt, gcol, grow, bsz, seq)
    return gdn_out(o_f, o_b, z, out_norm, w_o.astype(BF16), x)


def _na_core(qkv, rpb, bsz, seq):
    d = qkv.shape[-1] // 3
    dh = d // NA_HEADS
    rows = seq // GRID_W
    win_r = min(NA_WIN_R, rows)
    q, k, v = jnp.split(qkv.astype(F32), 3, axis=-1)
    grid = lambda t: t.reshape(bsz, rows, GRID_W, NA_HEADS, dh)
    q, k, v = grid(q) * dh ** -0.5, grid(k), grid(v)
    row_start = np.clip(np.arange(rows) - win_r // 2, 0, rows - win_r)
    cols = np.arange(GRID_W)
    col_start = np.clip(cols - NA_WIN_C // 2, 0, GRID_W - NA_WIN_C)
    col_valid = (cols[None, :] >= col_start[:, None]) & (cols[None, :] < col_start[:, None] + NA_WIN_C)
    dc_idx = np.clip(cols[None, :] - cols[:, None] + NA_WIN_C - 1, 0, 2 * NA_WIN_C - 2)
    bias_c = jnp.where(col_valid, rpb[:, :, dc_idx].astype(F32), NEG_INF)
    bias_c = jnp.transpose(bias_c, (0, 2, 1, 3))
    dr_idx = row_start[:, None] + np.arange(win_r)[None, :] - np.arange(rows)[:, None] + NA_WIN_R - 1

    def one_row(args):
        q_r, r0, dri = args
        k_w = lax.dynamic_slice_in_dim(k, r0, win_r, axis=1)
        v_w = lax.dynamic_slice_in_dim(v, r0, win_r, axis=1)
        s = jnp.einsum('bqhd,bajhd->bhqaj', q_r, k_w, preferred_element_type=F32)
        s = s + jnp.take(bias_c, dri, axis=2)
        pr = jax.nn.softmax(s.reshape(bsz, NA_HEADS, GRID_W, win_r * GRID_W), axis=-1).reshape(s.shape)
        return jnp.einsum('bhqaj,bajhd->bqhd', pr, v_w)

    out = lax.map(one_row, (jnp.moveaxis(q, 1, 0), jnp.asarray(row_start, jnp.int32), jnp.asarray(dr_idx, jnp.int32)))
    return jnp.moveaxis(out, 0, 1).reshape(bsz * seq, d)


def _sg_core(uv, sg_norm, w_s, b_s, bsz, seq):
    width = uv.shape[-1] // 2
    n = seq // SG_CHUNK
    u, v = uv[:, :width], uv[:, width:]
    vf = v.astype(F32)
    vn = vf * lax.rsqrt(jnp.mean(vf * vf, axis=-1, keepdims=True) + EPS) * sg_norm
    vn = vn.reshape(bsz, n, SG_CHUNK, SG_GROUPS, width // SG_GROUPS)
    mixed = jnp.einsum('gts,bnsgc->bntgc', w_s, vn) + b_s.T[:, :, None]
    return u * mixed.reshape(bsz * seq, width)


def _l2norm(t):
    return t * lax.rsqrt(jnp.sum(t * t, axis=-1, keepdims=True) + EPS)


def _dwconv(x, w):
    width = w.shape[0]
    half = width // 2
    seq = x.shape[1]
    xp = jnp.pad(x, ((0, 0), (half, half), (0, 0)))
    return sum(xp[:, k:k + seq] * w[k] for k in range(width))


def _gdn_scan(q, k, v, g, beta):
    bsz, seq = q.shape[:2]
    n = seq // GDN_CHUNK
    rep = GDN_V_HEADS // GDN_QK_HEADS
    tri_incl = np.tril(np.ones((GDN_CHUNK, GDN_CHUNK), bool))
    tri_strict = np.tril(np.ones((GDN_CHUNK, GDN_CHUNK), bool), -1)
    eye = jnp.eye(GDN_CHUNK, dtype=F32)
    hp = lax.Precision.HIGHEST

    def chunks(t):
        return jnp.moveaxis(t.reshape(bsz, n, GDN_CHUNK, *t.shape[2:]), 1, 0)

    def tr(t):
        return jnp.swapaxes(t, -1, -2)

    def step(state, inp):
        qc, kc, vc, gc, bc = inp
        qh = jnp.swapaxes(jnp.repeat(qc, rep, axis=2), 1, 2)
        kh = jnp.swapaxes(jnp.repeat(kc, rep, axis=2), 1, 2)
        vh = jnp.swapaxes(vc, 1, 2)
        gam = jnp.cumsum(jnp.swapaxes(gc, 1, 2), axis=-1)
        bet = jnp.swapaxes(bc, 1, 2)
        decay = jnp.exp(jnp.where(tri_incl, gam[..., :, None] - gam[..., None, :], -jnp.inf))
        m = jnp.where(tri_strict, jnp.matmul(kh, tr(kh), precision=hp) * decay, 0.0) * bet[..., :, None]
        e_gam = jnp.exp(gam)
        rhs = jnp.concatenate([kh * (bet * e_gam)[..., None], vh * bet[..., None]], axis=-1)
        sol = lax.linalg.triangular_solve(eye + m, rhs, left_side=True, lower=True, unit_diagonal=True)
        w_mat, u_val = sol[..., :GDN_DK], sol[..., GDN_DK:]
        u = u_val - jnp.matmul(w_mat, state, precision=hp)
        o = jnp.matmul(qh * e_gam[..., None], state, precision=hp) + jnp.matmul(
            jnp.matmul(qh, tr(kh), precision=hp) * decay, u, precision=hp)
        k_dec = kh * jnp.exp(gam[..., -1:] - gam)[..., None]
        new_state = e_gam[..., -1][..., None, None] * state + jnp.matmul(tr(k_dec), u, precision=hp)
        return new_state, o

    state0 = jnp.zeros((bsz, GDN_V_HEADS, GDN_DK, GDN_DV), F32)
    _, o = lax.scan(step, state0, (chunks(q), chunks(k), chunks(v), chunks(g), chunks(beta)))
    return jnp.transpose(o, (1, 0, 3, 2, 4)).reshape(bsz, seq, GDN_V_HEADS, GDN_DV)


def _gdn_core(proj, conv_w, a_log, dt_bias, out_norm, bsz, seq):
    qk_w = GDN_QK_HEADS * GDN_DK
    v_w = GDN_V_HEADS * GDN_DV
    conv_width = 2 * qk_w + v_w
    proj = proj.reshape(bsz, seq, -1).astype(F32)
    qkv = jax.nn.silu(_dwconv(proj[..., :conv_width], conv_w))
    z = proj[..., conv_width:conv_width + v_w]
    ab = proj[..., conv_width + v_w:].reshape(bsz, seq, 2, 2, GDN_V_HEADS)
    q = _l2norm(qkv[..., :qk_w].reshape(bsz, seq, GDN_QK_HEADS, GDN_DK)) * GDN_DK ** -0.5
    k = _l2norm(qkv[..., qk_w:2 * qk_w].reshape(bsz, seq, GDN_QK_HEADS, GDN_DK))
    v = qkv[..., 2 * qk_w:].reshape(bsz, seq, GDN_V_HEADS, GDN_DV)
    decay_rate = jnp.exp(a_log.astype(F32))
    g = -decay_rate * jax.nn.softplus(ab[:, :, :, 0] + dt_bias.astype(F32))
    beta = jax.nn.sigmoid(ab[:, :, :, 1])
    o_fwd = _gdn_scan(q, k, v, g[:, :, 0], beta[:, :, 0])
    o_bwd = jnp.flip(_gdn_scan(jnp.flip(q, 1), jnp.flip(k, 1), jnp.flip(v, 1),
                               jnp.flip(g[:, :, 1], 1), jnp.flip(beta[:, :, 1], 1)), 1)
    zg = jax.nn.silu(z.reshape(bsz, seq, GDN_V_HEADS, GDN_DV))
    o = o_fwd + o_bwd
    o = o * lax.rsqrt(jnp.mean(o * o, axis=-1, keepdims=True) + EPS) * out_norm * zg
    return o.reshape(bsz * seq, v_w)


def _s5_direction(u, a_re, a_im, log_dt, b_re, b_im, c_re, c_im):
    bsz, seq, groups = u.shape[:3]
    n = seq // S5_CHUNK
    lam = lax.complex(a_re.astype(F32), a_im.astype(F32))
    dt = jnp.exp(log_dt.astype(F32))[:, None]
    a_bar = jnp.exp(lam * dt)
    b_bar = ((a_bar - 1.0) / lam)[..., None] * lax.complex(b_re.astype(F32), b_im.astype(F32))
    c = lax.complex(c_re.astype(F32), c_im.astype(F32))
    bu = jnp.einsum('gpc,bsgc->bsgp', b_bar, u.astype(jnp.complex64), precision=lax.Precision.HIGHEST)
    bu = jnp.moveaxis(bu.reshape(bsz, n, S5_CHUNK, groups, S5_STATE), 1, 0)
    a_elems = jnp.broadcast_to(a_bar, (bsz, S5_CHUNK, groups, S5_STATE))
    powers = jnp.exp(lam[None] * dt[None] * jnp.arange(1, S5_CHUNK + 1, dtype=F32)[:, None, None])

    def binop(e1, e2):
        return (e2[0] * e1[0], e2[0] * e1[1] + e2[1])

    def step(x_prev, bu_c):
        _, xs = lax.associative_scan(binop, (a_elems, bu_c), axis=1)
        xs = xs + powers[None] * x_prev[:, None]
        y = jnp.einsum('gcp,blgp->blgc', c, xs, precision=lax.Precision.HIGHEST).real
        return xs[:, -1], y

    x0 = jnp.zeros((bsz, groups, S5_STATE), jnp.complex64)
    _, ys = lax.scan(step, x0, bu)
    return jnp.moveaxis(ys, 0, 1).reshape(bsz, seq, groups, S5_GROUP_DIM)


def _s5_core(h, a_re, a_im, log_dt, b_re, b_im, c_re, c_im, d_skip, bsz, seq):
    d = h.shape[-1]
    groups = d // S5_GROUP_DIM
    u = h.reshape(bsz, seq, groups, S5_GROUP_DIM)
    y_f = _s5_direction(u, a_re[0], a_im[0], log_dt[0], b_re[0], b_im[0], c_re[0], c_im[0])
    y_b = jnp.flip(_s5_direction(jnp.flip(u, 1), a_re[1], a_im[1], log_dt[1], b_re[1], b_im[1], c_re[1], c_im[1]), 1)
    y = (y_f + y_b).reshape(bsz * seq, d) + d_skip * h
    return jax.nn.gelu(y)


def _trunk(x, p, w, bsz, seq):
    depth = w['norm_mix'].shape[0]
    bf = lambda t: t.astype(BF16)
    for i in range(depth):
        kind, j = i % N_MIXERS, i // N_MIXERS
        gmix = w['norm_mix'][i]
        if kind == 0:
            qkv = norm_matmul(x, gmix, bf(w['na_w_qkv'][j]))
            att = _na_core(qkv, w['na_rpb'][j], bsz, seq)
            x = matmul_res(att, bf(w['na_w_o'][j]), x)
        elif kind == 1:
            uv = norm_matmul(x, gmix, bf(w['sg_w_in'][j]), act="gelu")
            gated = _sg_core(uv, w['sg_norm'][j], w['sg_w_s'][j], w['sg_b_s'][j], bsz, seq)
            x = matmul_res(gated, bf(w['sg_w_o'][j]), x)
        elif kind == 2:
            x = gated_deltanet(x, gmix, w['gdn_w_in'][j], w['gdn_conv_w'][j], w['gdn_a_log'][j],
                               w['gdn_dt_bias'][j], w['gdn_out_norm'][j], w['gdn_w_o'][j], bsz, seq)
        else:
            h = rmsnorm(x, gmix)
            y = _s5_core(h, w['s5_a_re'][j], w['s5_a_im'][j], w['s5_log_dt'][j], w['s5_b_re'][j], w['s5_b_im'][j],
                         w['s5_c_re'][j], w['s5_c_im'][j], w['s5_d'][j], bsz, seq)
            x = glu_res(y, bf(w['s5_w_glu'][j]), x)
        x = ffn(x, w['norm_ffn'][i], bf(w['ffn_w_gu'][i]), w['ffn_conv_w'][i], w['ffn_conv_b'][i],
                bf(w['ffn_w_down'][i]), seq)
        x = ple(x, p[i], w['norm_ple'][i], bf(w['ple_w_gate'][i]), bf(w['ple_w_proj'][i]))
    return x


def kernel(x_prompt, x_sample, p_prompt, p_sample, norm_mix, norm_ffn, norm_ple, final_norm, na_w_qkv, na_w_o, na_rpb, sg_w_in, sg_norm, sg_w_s, sg_b_s, sg_w_o, gdn_w_in, gdn_conv_w, gdn_a_log, gdn_dt_bias, gdn_out_norm, gdn_w_o, s5_a_re, s5_a_im, s5_log_dt, s5_b_re, s5_b_im, s5_c_re, s5_c_im, s5_d, s5_w_glu, ffn_w_gu, ffn_conv_w, ffn_conv_b, ffn_w_down, ple_w_proj, ple_w_gate):
    w = dict(norm_mix=norm_mix, norm_ffn=norm_ffn, norm_ple=norm_ple, final_norm=final_norm,
             na_w_qkv=na_w_qkv, na_w_o=na_w_o, na_rpb=na_rpb,
             sg_w_in=sg_w_in, sg_norm=sg_norm, sg_w_s=sg_w_s, sg_b_s=sg_b_s, sg_w_o=sg_w_o,
             gdn_w_in=gdn_w_in, gdn_conv_w=gdn_conv_w, gdn_a_log=gdn_a_log, gdn_dt_bias=gdn_dt_bias,
             gdn_out_norm=gdn_out_norm, gdn_w_o=gdn_w_o,
             s5_a_re=s5_a_re, s5_a_im=s5_a_im, s5_log_dt=s5_log_dt, s5_b_re=s5_b_re, s5_b_im=s5_b_im,
             s5_c_re=s5_c_re, s5_c_im=s5_c_im, s5_d=s5_d, s5_w_glu=s5_w_glu,
             ffn_w_gu=ffn_w_gu, ffn_conv_w=ffn_conv_w, ffn_conv_b=ffn_conv_b, ffn_w_down=ffn_w_down,
             ple_w_proj=ple_w_proj, ple_w_gate=ple_w_gate)
    b1, seq, d = x_prompt.shape
    b2 = x_sample.shape[0]
    bsz = b1 + b2
    x = jnp.concatenate([x_prompt, x_sample], axis=0).reshape(bsz * seq, d)
    p = jnp.concatenate([p_prompt, p_sample], axis=1).reshape(p_prompt.shape[0], bsz * seq, -1)
    x = _trunk(x, p, w, bsz, seq)
    y1 = rmsnorm(x, final_norm, row0=0, rows=b1 * seq).reshape(b1, seq, d)
    y2 = rmsnorm(x, final_norm, row0=b1 * seq, rows=b2 * seq).reshape(b2, seq, d)
    return (y1, y2)
```

```python
import functools
import math

import jax
import jax.numpy as jnp
import numpy as np
from jax import lax
from jax.experimental import pallas as pl
from jax.experimental.pallas import tpu as pltpu

F32 = jnp.float32
BF16 = jnp.bfloat16

EPS = 1e-6
NEG_INF = -1e30
GRID_W = 64
NA_HEADS = 16
NA_WIN_R = 8
NA_WIN_C = 16
SG_CHUNK = 128
SG_GROUPS = 16
GDN_QK_HEADS = 16
GDN_V_HEADS = 32
GDN_DK = 128
GDN_DV = 128
GDN_CHUNK = 64
S5_GROUP_DIM = 16
S5_STATE = 64
S5_CHUNK = 128
N_MIXERS = 4

VMEM_LIMIT_BYTES = 56 * 1024 * 1024
HALO = 16


def _params(*sem):
    return pltpu.CompilerParams(dimension_semantics=sem, vmem_limit_bytes=VMEM_LIMIT_BYTES)


def _rms(x, g):
    return x * lax.rsqrt(jnp.mean(x * x, axis=-1, keepdims=True) + EPS) * g


def _gelu_tanh(x):
    return 0.5 * x * (1.0 + jnp.tanh(math.sqrt(2.0 / math.pi) * (x + 0.044715 * (x * x * x))))


def _sigmoid(x):
    return 1.0 / (1.0 + jnp.exp(-x))


def _rmsnorm_kernel(x_ref, g_ref, o_ref):
    o_ref[...] = _rms(x_ref[...], g_ref[...]).astype(o_ref.dtype)


def rmsnorm(x, g, out_dtype=F32, tm=512, row0=0, rows=None):
    m, d = x.shape
    rows = m if rows is None else rows
    off = row0 // tm
    return pl.pallas_call(
        _rmsnorm_kernel,
        out_shape=jax.ShapeDtypeStruct((rows, d), out_dtype),
        grid=(rows // tm,),
        in_specs=[pl.BlockSpec((tm, d), lambda i: (i + off, 0)), pl.BlockSpec((1, d), lambda i: (0, 0))],
        out_specs=pl.BlockSpec((tm, d), lambda i: (i, 0)),
        compiler_params=_params("parallel"),
        name="rmsnorm",
    )(x, g.reshape(1, d))


def _norm_matmul_kernel(x_ref, g_ref, w_ref, o_ref, hn_ref, *, act):
    @pl.when(pl.program_id(1) == 0)
    def _():
        hn_ref[...] = _rms(x_ref[...], g_ref[...]).astype(BF16)

    y = jnp.dot(hn_ref[...], w_ref[...], preferred_element_type=F32)
    if act == "gelu":
        y = _gelu_tanh(y)
    o_ref[...] = y.astype(o_ref.dtype)


def norm_matmul(x, g, w, act=None, out_dtype=F32, tm=512, tn=512):
    m, d = x.shape
    n = w.shape[1]
    if n % tn:
        tn = 128
    return pl.pallas_call(
        functools.partial(_norm_matmul_kernel, act=act),
        out_shape=jax.ShapeDtypeStruct((m, n), out_dtype),
        grid=(m // tm, n // tn),
        in_specs=[
            pl.BlockSpec((tm, d), lambda i, j: (i, 0)),
            pl.BlockSpec((1, d), lambda i, j: (0, 0)),
            pl.BlockSpec((d, tn), lambda i, j: (0, j)),
        ],
        out_specs=pl.BlockSpec((tm, tn), lambda i, j: (i, j)),
        scratch_shapes=[pltpu.VMEM((tm, d), BF16)],
        compiler_params=_params("parallel", "arbitrary"),
        name="norm_matmul",
    )(x, g.reshape(1, d), w)


def _matmul_res_kernel(a_ref, w_ref, r_ref, o_ref):
    o_ref[...] = r_ref[...] + jnp.dot(a_ref[...].astype(BF16), w_ref[...], preferred_element_type=F32)


def matmul_res(a, w, res, tm=512, tn=512):
    m, k = a.shape
    n = w.shape[1]
    return pl.pallas_call(
        _matmul_res_kernel,
        out_shape=jax.ShapeDtypeStruct((m, n), F32),
        grid=(m // tm, n // tn),
        in_specs=[
            pl.BlockSpec((tm, k), lambda i, j: (i, 0)),
            pl.BlockSpec((k, tn), lambda i, j: (0, j)),
            pl.BlockSpec((tm, tn), lambda i, j: (i, j)),
        ],
        out_specs=pl.BlockSpec((tm, tn), lambda i, j: (i, j)),
        compiler_params=_params("parallel", "arbitrary"),
        name="matmul_res",
    )(a, w, res)


def _glu_res_kernel(a_ref, wa_ref, wb_ref, r_ref, o_ref):
    a = a_ref[...].astype(BF16)
    ya = jnp.dot(a, wa_ref[...], preferred_element_type=F32)
    yb = jnp.dot(a, wb_ref[...], preferred_element_type=F32)
    o_ref[...] = r_ref[...] + ya * _sigmoid(yb)


def glu_res(a, w, res, tm=512, tn=512):
    m, k = a.shape
    n = w.shape[1] // 2
    nb = n // tn
    return pl.pallas_call(
        _glu_res_kernel,
        out_shape=jax.ShapeDtypeStruct((m, n), F32),
        grid=(m // tm, nb),
        in_specs=[
            pl.BlockSpec((tm, k), lambda i, j: (i, 0)),
            pl.BlockSpec((k, tn), lambda i, j: (0, j)),
            pl.BlockSpec((k, tn), lambda i, j: (0, j + nb)),
            pl.BlockSpec((tm, tn), lambda i, j: (i, j)),
        ],
        out_specs=pl.BlockSpec((tm, tn), lambda i, j: (i, j)),
        compiler_params=_params("parallel", "arbitrary"),
        name="glu_res",
    )(a, w, w, res)


def _ffn_kernel(x_ref, xp_ref, xn_ref, g_ref, wg_ref, wu_ref, cw_ref, cb_ref, wd_ref, o_ref, hn_ref, acc_ref,
                *, tm, seq):
    i = pl.program_id(0)
    j = pl.program_id(1)

    @pl.when(j == 0)
    def _():
        g = g_ref[...]
        prev_ok = jnp.where((i * tm) % seq != 0, 1.0, 0.0)
        next_ok = jnp.where(((i + 1) * tm) % seq != 0, 1.0, 0.0)
        hn_ref[0:HALO, :] = (_rms(xp_ref[...], g) * prev_ok).astype(BF16)
        hn_ref[HALO:HALO + tm, :] = _rms(x_ref[...], g).astype(BF16)
        hn_ref[HALO + tm:, :] = (_rms(xn_ref[...], g) * next_ok).astype(BF16)
        acc_ref[...] = jnp.zeros_like(acc_ref)

    rows = tm + 2 * HALO
    gate = jnp.dot(hn_ref[...], wg_ref[...], preferred_element_type=F32)
    up = jnp.dot(hn_ref[HALO:HALO + tm, :], wu_ref[...], preferred_element_type=F32)
    cw = cw_ref[...]
    g_prev = pltpu.roll(gate, 1, 0)[HALO:HALO + tm]
    g_next = pltpu.roll(gate, rows - 1, 0)[HALO:HALO + tm]
    gc = cw[0:1] * g_prev + cw[1:2] * gate[HALO:HALO + tm] + cw[2:3] * g_next + cb_ref[...]
    act = (gc * _sigmoid(gc) * up).astype(BF16)
    acc_ref[...] += jnp.dot(act, wd_ref[...], preferred_element_type=F32)

    @pl.when(j == pl.num_programs(1) - 1)
    def _():
        o_ref[...] = x_ref[...] + acc_ref[...]


def ffn(x, g, w_gu, conv_w, conv_b, w_down, seq, tm=512, tf=512):
    m, d = x.shape
    f = w_down.shape[0]
    nf = f // tf
    hb = tm // HALO
    last = m // HALO - 1
    return pl.pallas_call(
        functools.partial(_ffn_kernel, tm=tm, seq=seq),
        out_shape=jax.ShapeDtypeStruct((m, d), F32),
        grid=(m // tm, nf),
        in_specs=[
            pl.BlockSpec((tm, d), lambda i, j: (i, 0)),
            pl.BlockSpec((HALO, d), lambda i, j: (jnp.maximum(i * hb - 1, 0), 0)),
            pl.BlockSpec((HALO, d), lambda i, j: (jnp.minimum((i + 1) * hb, last), 0)),
            pl.BlockSpec((1, d), lambda i, j: (0, 0)),
            pl.BlockSpec((d, tf), lambda i, j: (0, j)),
            pl.BlockSpec((d, tf), lambda i, j: (0, j + nf)),
            pl.BlockSpec((3, tf), lambda i, j: (0, j)),
            pl.BlockSpec((1, tf), lambda i, j: (0, j)),
            pl.BlockSpec((tf, d), lambda i, j: (j, 0)),
        ],
        out_specs=pl.BlockSpec((tm, d), lambda i, j: (i, 0)),
        scratch_shapes=[pltpu.VMEM((tm + 2 * HALO, d), BF16), pltpu.VMEM((tm, d), F32)],
        compiler_params=_params("parallel", "arbitrary"),
        name="ffn",
    )(x, x, x, g.reshape(1, d), w_gu, w_gu, conv_w, conv_b.reshape(1, f), w_down)


def _ple_kernel(x_ref, p_ref, g_ref, wg_ref, wp_ref, o_ref):
    x = x_ref[...]
    hn = _rms(x, g_ref[...]).astype(BF16)
    gate = _sigmoid(jnp.dot(hn, wg_ref[...], preferred_element_type=F32))
    proj = jnp.dot(p_ref[...].astype(BF16), wp_ref[...], preferred_element_type=F32)
    o_ref[...] = x + gate * proj


def ple(x, p, g, w_gate, w_proj, tm=512):
    m, d = x.shape
    pd = p.shape[1]
    return pl.pallas_call(
        _ple_kernel,
        out_shape=jax.ShapeDtypeStruct((m, d), F32),
        grid=(m // tm,),
        in_specs=[
            pl.BlockSpec((tm, d), lambda i: (i, 0)),
            pl.BlockSpec((tm, pd), lambda i: (i, 0)),
            pl.BlockSpec((1, d), lambda i: (0, 0)),
            pl.BlockSpec((d, d), lambda i: (0, 0)),
            pl.BlockSpec((pd, d), lambda i: (0, 0)),
        ],
        out_specs=pl.BlockSpec((tm, d), lambda i: (i, 0)),
        compiler_params=_params("parallel"),
        name="ple",
    )(x, p, g.reshape(1, d), w_gate, w_proj)


GDN_BLOCK = 256


def _conv_in_kernel(x_ref, xp_ref, xn_ref, g_ref, w_ref, cw_ref, o_ref, hn_ref, *, tm, seq, nq, nqk, scale):
    i = pl.program_id(0)
    j = pl.program_id(1)

    @pl.when(j == 0)
    def _():
        g = g_ref[...]
        prev_ok = jnp.where((i * tm) % seq != 0, 1.0, 0.0)
        next_ok = jnp.where(((i + 1) * tm) % seq != 0, 1.0, 0.0)
        hn_ref[0:HALO, :] = (_rms(xp_ref[...], g) * prev_ok).astype(BF16)
        hn_ref[HALO:HALO + tm, :] = _rms(x_ref[...], g).astype(BF16)
        hn_ref[HALO + tm:, :] = (_rms(xn_ref[...], g) * next_ok).astype(BF16)

    rows = tm + 2 * HALO
    y = jnp.dot(hn_ref[...], w_ref[...], preferred_element_type=F32)
    cw = cw_ref[...]
    y_prev = pltpu.roll(y, 1, 0)[HALO:HALO + tm]
    y_next = pltpu.roll(y, rows - 1, 0)[HALO:HALO + tm]
    c = cw[0:1] * y_prev + cw[1:2] * y[HALO:HALO + tm] + cw[2:3] * y_next
    c = c * _sigmoid(c)
    tn = c.shape[1]
    for s in range(tn // 128):
        cs = c[:, s * 128:(s + 1) * 128]
        inv = lax.rsqrt(jnp.sum(cs * cs, axis=-1, keepdims=True) + EPS)
        f = jnp.where(j < nq, inv * scale, jnp.where(j < nqk, inv, 1.0))
        o_ref[:, s * 128:(s + 1) * 128] = (cs * f).astype(o_ref.dtype)


def conv_in(x, g, w, conv_w, seq, n_q, n_qk, scale, tm=512, tn=256):
    m, d = x.shape
    n = w.shape[1]
    hb = tm // HALO
    last = m // HALO - 1
    return pl.pallas_call(
        functools.partial(_conv_in_kernel, tm=tm, seq=seq, nq=n_q // tn, nqk=n_qk // tn, scale=scale),
        out_shape=jax.ShapeDtypeStruct((m, n), BF16),
        grid=(m // tm, n // tn),
        in_specs=[
            pl.BlockSpec((tm, d), lambda i, j: (i, 0)),
            pl.BlockSpec((HALO, d), lambda i, j: (jnp.maximum(i * hb - 1, 0), 0)),
            pl.BlockSpec((HALO, d), lambda i, j: (jnp.minimum((i + 1) * hb, last), 0)),
            pl.BlockSpec((1, d), lambda i, j: (0, 0)),
            pl.BlockSpec((d, tn), lambda i, j: (0, j)),
            pl.BlockSpec((3, tn), lambda i, j: (0, j)),
        ],
        out_specs=pl.BlockSpec((tm, tn), lambda i, j: (i, j)),
        scratch_shapes=[pltpu.VMEM((tm + 2 * HALO, d), BF16)],
        compiler_params=_params("parallel", "arbitrary"),
        name="gdn_conv_in",
    )(x, x, x, g.reshape(1, d), w, conv_w)


def _gdn_gates_kernel(x_ref, g_ref, w_ref, alog_ref, bias_ref, isg_ref, o_ref):
    hn = _rms(x_ref[...], g_ref[...]).astype(BF16)
    y = jnp.dot(hn, w_ref[...], preferred_element_type=F32)
    t = y + bias_ref[...]
    softplus = jnp.maximum(t, 0.0) + jnp.log1p(jnp.exp(-jnp.abs(t)))
    gval = -jnp.exp(alog_ref[...]) * softplus
    o_ref[...] = jnp.where(isg_ref[...] > 0.5, gval, _sigmoid(y))


def gdn_gates(x, g, w_ab, a_log, dt_bias, tm=512):
    m, d = x.shape
    n = w_ab.shape[1]
    nh = a_log.shape[-1]
    zeros = jnp.zeros((2, 1, nh), F32)
    arrange = lambda t: jnp.concatenate([t.reshape(2, 1, nh).astype(F32), zeros], axis=1).reshape(1, n)
    isg = jnp.concatenate([jnp.ones((2, 1, nh), F32), zeros], axis=1).reshape(1, n)
    return pl.pallas_call(
        _gdn_gates_kernel,
        out_shape=jax.ShapeDtypeStruct((m, n), F32),
        grid=(m // tm,),
        in_specs=[
            pl.BlockSpec((tm, d), lambda i: (i, 0)),
            pl.BlockSpec((1, d), lambda i: (0, 0)),
            pl.BlockSpec((d, n), lambda i: (0, 0)),
            pl.BlockSpec((1, n), lambda i: (0, 0)),
            pl.BlockSpec((1, n), lambda i: (0, 0)),
            pl.BlockSpec((1, n), lambda i: (0, 0)),
        ],
        out_specs=pl.BlockSpec((tm, n), lambda i: (i, 0)),
        compiler_params=_params("parallel"),
        name="gdn_gates",
    )(x, g.reshape(1, d), w_ab, arrange(a_log), arrange(dt_bias), isg)


def _gdn_direction(q_ref, k_ref, kt_ref, v_ref, gc_ref, gr_ref, o_ref, s_ref, rev):
    c_sz, lc = GDN_BLOCK, GDN_CHUNK
    nchunk = c_sz // lc
    hp = lax.Precision.HIGHEST
    d = 1 if rev else 0
    ii = lax.broadcasted_iota(jnp.int32, (c_sz, c_sz), 0)
    jj = lax.broadcasted_iota(jnp.int32, (c_sz, c_sz), 1)
    sh = lc.bit_length() - 1
    same = (ii >> sh) == (jj >> sh)
    incl = jnp.logical_and(same, (jj >= ii) if rev else (jj <= ii))
    eye = ii == jj
    cum = jnp.where(incl, 1.0, 0.0)
    blk = jnp.where(same, 1.0, 0.0)
    q = q_ref[...]
    k = k_ref[...]
    kf = k.astype(F32)
    qf = q.astype(F32)
    ktf = kt_ref[0].astype(F32)
    gc = gc_ref[0, 0][:, 4 * d:4 * d + 4]
    gr = gr_ref[0, 0][4 * d:4 * d + 4, :]
    gam_c = jnp.dot(cum, gc[:, 0:2], precision=hp, preferred_element_type=F32)
    gam_r = lax.dot_general(gr[0:2], cum, (((1,), (1,)), ((), ())), precision=hp,
                            preferred_element_type=F32)
    end_r = jnp.dot(gr[0:2], blk, precision=hp, preferred_element_type=F32)
    gram = lax.dot_general(k, k, (((1,), (1,)), ((), ())), preferred_element_type=F32)
    qk = lax.dot_general(q, k, (((1,), (1,)), ((), ())), preferred_element_type=F32)
    jrow = lax.broadcasted_iota(jnp.int32, (1, c_sz), 1) >> sh
    for hs in range(2):
        gcol = gam_c[:, hs:hs + 1]
        grow = gam_r[hs:hs + 1, :]
        bcol = gc[:, 2 + hs:3 + hs]
        dec = jnp.exp(jnp.where(incl, gcol - grow, NEG_INF))
        n = -(gram * jnp.where(eye, 0.0, dec)) * bcol
        p = jnp.where(eye, 1.0, 0.0) + n
        nb = n.astype(BF16)
        npow = jnp.dot(nb, nb, preferred_element_type=F32)
        for lvl in range(5):
            nb = npow.astype(BF16)
            p = p + jnp.dot(nb, p.astype(BF16), preferred_element_type=F32)
            if lvl < 4:
                npow = jnp.dot(nb, nb, preferred_element_type=F32)
        e_g = jnp.exp(gcol)
        vh = v_ref[:, hs * GDN_DV:(hs + 1) * GDN_DV].astype(F32)
        rhs = jnp.concatenate([kf * (bcol * e_g), vh * bcol], axis=1).astype(BF16)
        wu = jnp.dot(p.astype(BF16), rhs, preferred_element_type=F32)
        wub = wu.astype(BF16)
        awu = jnp.dot((qk * dec).astype(BF16), wub, preferred_element_type=F32)
        qeff = (qf * e_g - awu[:, :GDN_DK]).astype(BF16)
        o_in = awu[:, GDN_DK:]
        erow = jnp.exp(end_r[hs:hs + 1, :] - grow)
        kdt = ktf * erow
        s = s_ref[2 * d + hs]
        order = range(nchunk - 1, -1, -1) if rev else range(nchunk)
        for c in order:
            r0 = c * lc
            kdt_c = jnp.where(jrow == c, kdt, 0.0).astype(BF16)
            kwu = jnp.dot(kdt_c, wub, preferred_element_type=F32)
            sb = s.astype(BF16)
            o_c = o_in[r0:r0 + lc] + jnp.dot(qeff[r0:r0 + lc], sb, preferred_element_type=F32)
            o_ref[r0:r0 + lc, hs * GDN_DV:(hs + 1) * GDN_DV] = o_c.astype(o_ref.dtype)
            e_end = jnp.exp(end_r[hs:hs + 1, r0:r0 + 1])
            s = e_end * s - jnp.dot(kwu[:, :GDN_DK].astype(BF16), sb, preferred_element_type=F32) + kwu[:, GDN_DK:]
        s_ref[2 * d + hs] = s


def _gdn_scan_kernel(qf_ref, kf_ref, ktf_ref, vf_ref, gcf_ref, grf_ref,
                     qb_ref, kb_ref, ktb_ref, vb_ref, gcb_ref, grb_ref, of_ref, ob_ref, s_ref):
    @pl.when(pl.program_id(2) == 0)
    def _():
        s_ref[...] = jnp.zeros_like(s_ref)

    _gdn_direction(qf_ref, kf_ref, ktf_ref, vf_ref, gcf_ref, grf_ref, of_ref, s_ref, False)
    _gdn_direction(qb_ref, kb_ref, ktb_ref, vb_ref, gcb_ref, grb_ref, ob_ref, s_ref, True)


def gdn_scan(qkv, kt, gcol, grow, bsz, seq):
    c_sz = GDN_BLOCK
    nb = seq // c_sz
    hq = GDN_QK_HEADS
    kcol = hq
    vcol = (2 * hq * GDN_DK) // (2 * GDN_DV)
    fwd = lambda b, h, c: c
    bwd = lambda b, h, c: nb - 1 - c

    def specs(pos):
        return [
            pl.BlockSpec((c_sz, GDN_DK), lambda b, h, c: (b * nb + pos(b, h, c), h)),
            pl.BlockSpec((c_sz, GDN_DK), lambda b, h, c: (b * nb + pos(b, h, c), kcol + h)),
            pl.BlockSpec((1, GDN_DK, c_sz), lambda b, h, c: (b, h, pos(b, h, c))),
            pl.BlockSpec((c_sz, 2 * GDN_DV), lambda b, h, c: (b * nb + pos(b, h, c), vcol + h)),
            pl.BlockSpec((1, 1, c_sz, 8), lambda b, h, c: (b, h, pos(b, h, c), 0)),
            pl.BlockSpec((1, 1, 8, c_sz), lambda b, h, c: (b, h, 0, pos(b, h, c))),
        ]

    out = jax.ShapeDtypeStruct((bsz * seq, GDN_V_HEADS * GDN_DV), BF16)
    return pl.pallas_call(
        _gdn_scan_kernel,
        out_shape=(out, out),
        grid=(bsz, hq, nb),
        in_specs=specs(fwd) + specs(bwd),
        out_specs=(
            pl.BlockSpec((c_sz, 2 * GDN_DV), lambda b, h, c: (b * nb + c, h)),
            pl.BlockSpec((c_sz, 2 * GDN_DV), lambda b, h, c: (b * nb + nb - 1 - c, h)),
        ),
        scratch_shapes=[pltpu.VMEM((4, GDN_DK, GDN_DV), F32)],
        compiler_params=_params("parallel", "parallel", "arbitrary"),
        name="gdn_scan",
    )(qkv, qkv, kt, qkv, gcol, grow, qkv, qkv, kt, qkv, gcol, grow)


def _gdn_out_kernel(of_ref, ob_ref, z_ref, gn_ref, w_ref, r_ref, o_ref, a_ref):
    @pl.when(pl.program_id(1) == 0)
    def _():
        gn = gn_ref[...]
        for h in range(GDN_V_HEADS):
            sl = slice(h * GDN_DV, (h + 1) * GDN_DV)
            o = of_ref[:, sl].astype(F32) + ob_ref[:, sl].astype(F32)
            z = z_ref[:, sl].astype(F32)
            a_ref[:, sl] = (_rms(o, gn) * (z * _sigmoid(z))).astype(BF16)

    o_ref[...] = r_ref[...] + jnp.dot(a_ref[...], w_ref[...], preferred_element_type=F32)


def gdn_out(o_f, o_b, z, out_norm, w_o, res, tm=512, tn=512):
    m, k = o_f.shape
    n = w_o.shape[1]
    return pl.pallas_call(
        _gdn_out_kernel,
        out_shape=jax.ShapeDtypeStruct((m, n), F32),
        grid=(m // tm, n // tn),
        in_specs=[
            pl.BlockSpec((tm, k), lambda i, j: (i, 0)),
            pl.BlockSpec((tm, k), lambda i, j: (i, 0)),
            pl.BlockSpec((tm, k), lambda i, j: (i, 0)),
            pl.BlockSpec((1, GDN_DV), lambda i, j: (0, 0)),
            pl.BlockSpec((k, tn), lambda i, j: (0, j)),
            pl.BlockSpec((tm, tn), lambda i, j: (i, j)),
        ],
        out_specs=pl.BlockSpec((tm, tn), lambda i, j: (i, j)),
        scratch_shapes=[pltpu.VMEM((tm, k), BF16)],
        compiler_params=_params("parallel", "arbitrary"),
        name="gdn_out",
    )(o_f, o_b, z, out_norm.reshape(1, GDN_DV), w_o, res)


def gated_deltanet(x, gmix, w_in, conv_w, a_log, dt_bias, out_norm, w_o, bsz, seq):
    qk_w = GDN_QK_HEADS * GDN_DK
    v_w = GDN_V_HEADS * GDN_DV
    cw = 2 * qk_w + v_w
    w_in = w_in.astype(BF16)
    qkv = conv_in(x, gmix, w_in[:, :cw], conv_w, seq, qk_w, 2 * qk_w, GDN_DK ** -0.5)
    z = norm_matmul(x, gmix, w_in[:, cw:cw + v_w], out_dtype=BF16)
    gb = gdn_gates(x, gmix, w_in[:, cw + v_w:], a_log, dt_bias)
    gcol = gb.reshape(bsz, seq, 2, 2, GDN_QK_HEADS, 2).transpose(0, 4, 1, 2, 3, 5).reshape(bsz, GDN_QK_HEADS, seq, 8)
    grow = jnp.swapaxes(gcol, 2, 3)
    kt = jnp.swapaxes(qkv[:, qk_w:2 * qk_w].reshape(bsz, seq, qk_w), 1, 2)
    o_f, o_b = gdn_scan(qkv, kt, gcol, grow, bsz, seq)
    return gdn_out(o_f, o_b, z, out_norm, w_o.astype(BF16), x)


def _na_bias_table(rpb, rows):
    win_r = min(NA_WIN_R, rows)
    cols = np.arange(GRID_W)
    col_start = np.clip(cols - NA_WIN_C // 2, 0, GRID_W - NA_WIN_C)
    col_valid = (cols[None, :] >= col_start[:, None]) & (cols[None, :] < col_start[:, None] + NA_WIN_C)
    dc_idx = np.clip(cols[None, :] - cols[:, None] + NA_WIN_C - 1, 0, 2 * NA_WIN_C - 2)
    bias_c = jnp.where(col_valid, rpb[:, :, dc_idx].astype(F32), NEG_INF)
    dr = np.arange(NA_WIN_R)[:, None] + np.arange(win_r)[None, :]
    tab = bias_c[:, dr]
    return jnp.transpose(tab, (0, 1, 3, 2, 4)).reshape(rpb.shape[0], NA_WIN_R, GRID_W, win_r * GRID_W)


def _na_kernel(q_ref, k_ref, v_ref, b_ref, o_ref, *, rows, win_r, scale):
    wk = win_r * GRID_W

    def body(r, carry):
        r0 = jnp.clip(r - win_r // 2, 0, rows - win_r)
        q = q_ref[pl.ds(pl.multiple_of(r * GRID_W, GRID_W), GRID_W), :]
        kw = k_ref[pl.ds(pl.multiple_of(r0 * GRID_W, GRID_W), wk), :]
        vw = v_ref[pl.ds(pl.multiple_of(r0 * GRID_W, GRID_W), wk), :]
        s = lax.dot_general(q, kw, (((1,), (1,)), ((), ())), preferred_element_type=F32) * scale
        s = s + b_ref[0, r0 - r + NA_WIN_R - 1]
        p = jnp.exp(s - jnp.max(s, axis=-1, keepdims=True))
        den = jnp.sum(p, axis=-1, keepdims=True)
        o = jnp.dot(p.astype(BF16), vw, preferred_element_type=F32) / den
        o_ref[pl.ds(pl.multiple_of(r * GRID_W, GRID_W), GRID_W), :] = o.astype(o_ref.dtype)
        return carry

    lax.fori_loop(0, rows, body, 0)


def na_attention(qkv, rpb, bsz, seq):
    d = qkv.shape[1] // 3
    dh = d // NA_HEADS
    rows = seq // GRID_W
    win_r = min(NA_WIN_R, rows)
    table = _na_bias_table(rpb, rows)
    blk = lambda off: pl.BlockSpec((seq, dh), lambda b, h: (b, off + h))
    return pl.pallas_call(
        functools.partial(_na_kernel, rows=rows, win_r=win_r, scale=dh ** -0.5),
        out_shape=jax.ShapeDtypeStruct((bsz * seq, d), BF16),
        grid=(bsz, NA_HEADS),
        in_specs=[blk(0), blk(NA_HEADS), blk(2 * NA_HEADS),
                  pl.BlockSpec((1, NA_WIN_R, GRID_W, win_r * GRID_W), lambda b, h: (h, 0, 0, 0))],
        out_specs=blk(0),
        compiler_params=_params("parallel", "parallel"),
        name="na_attention",
    )(qkv, qkv, qkv, table)


def _sg_out_kernel(u_ref, v_ref, gn_ref, ws_ref, bs_ref, w_ref, r_ref, o_ref, a_ref, *, tm):
    @pl.when(pl.program_id(1) == 0)
    def _():
        vn = _rms(v_ref[...].astype(F32), gn_ref[...]).astype(BF16)
        bs = bs_ref[...]
        gd = vn.shape[1] // SG_GROUPS
        for c in range(tm // SG_CHUNK):
            rs = slice(c * SG_CHUNK, (c + 1) * SG_CHUNK)
            for g in range(SG_GROUPS):
                cs = slice(g * gd, (g + 1) * gd)
                mixed = jnp.dot(ws_ref[g], vn[rs, cs], preferred_element_type=F32) + bs[:, g:g + 1]
                a_ref[rs, cs] = (u_ref[rs, cs].astype(F32) * mixed).astype(BF16)

    o_ref[...] = r_ref[...] + jnp.dot(a_ref[...], w_ref[...], preferred_element_type=F32)


def sg_out(uv, sg_norm, w_s, b_s, w_o, res, tm=512, tn=512):
    m = uv.shape[0]
    width = uv.shape[1] // 2
    n = w_o.shape[1]
    return pl.pallas_call(
        functools.partial(_sg_out_kernel, tm=tm),
        out_shape=jax.ShapeDtypeStruct((m, n), F32),
        grid=(m // tm, n // tn),
        in_specs=[
            pl.BlockSpec((tm, width), lambda i, j: (i, 0)),
            pl.BlockSpec((tm, width), lambda i, j: (i, 1)),
            pl.BlockSpec((1, width), lambda i, j: (0, 0)),
            pl.BlockSpec(w_s.shape, lambda i, j: (0, 0, 0)),
            pl.BlockSpec((SG_CHUNK, SG_GROUPS), lambda i, j: (0, 0)),
            pl.BlockSpec((width, tn), lambda i, j: (0, j)),
            pl.BlockSpec((tm, tn), lambda i, j: (i, j)),
        ],
        out_specs=pl.BlockSpec((tm, tn), lambda i, j: (i, j)),
        scratch_shapes=[pltpu.VMEM((tm, width), BF16)],
        compiler_params=_params("parallel", "arbitrary"),
        name="sg_out",
    )(uv, uv, sg_norm.reshape(1, width), w_s.astype(BF16), b_s.T.astype(F32), w_o, res)


S5_L = 32
S5_W = S5_L * S5_GROUP_DIM


def _cexp(are, aim, dt, e):
    mag = jnp.exp(are * dt * e)
    ang = aim * dt * e
    return mag * jnp.cos(ang), mag * jnp.sin(ang)


def _s5_prep_kernel(arc_ref, aic_ref, arr_ref, air_ref, ldt_ref, btr_ref, bti_ref, ctr_ref, cti_ref,
                    k_ref, bm_ref, cm_ref, ap_ref, *, rev):
    hp = lax.Precision.HIGHEST
    ll, cg, w, p = S5_L, S5_GROUP_DIM, S5_W, S5_STATE
    sh = cg.bit_length() - 1
    dt = jnp.exp(ldt_ref[0])
    arc, aic = arc_ref[0], aic_ref[0]
    arr, air = arr_ref[0], air_ref[0]
    abr, abi = _cexp(arr, air, dt, 1.0)
    nr, ni = abr - 1.0, abi
    den = arr * arr + air * air
    cr, ci = (nr * arr + ni * air) / den, (ni * arr - nr * air) / den
    btr, bti = btr_ref[0], bti_ref[0]
    bbr, bbi = cr * btr - ci * bti, cr * bti + ci * btr
    lane = lax.broadcasted_iota(jnp.int32, (cg, w), 1)
    sel = jnp.where((lane & (cg - 1)) == lax.broadcasted_iota(jnp.int32, (cg, w), 0), 1.0, 0.0)
    cer = jnp.dot(ctr_ref[0], sel, precision=hp, preferred_element_type=F32)
    cei = jnp.dot(cti_ref[0], sel, precision=hp, preferred_element_type=F32)
    tl = (lax.broadcasted_iota(jnp.int32, (p, w), 1) >> sh).astype(F32)

    def cz(e):
        zr, zi = _cexp(arc, aic, dt, e)
        return cer * zr - cei * zi, cer * zi + cei * zr

    czr, czi = cz((ll - 1.0 - tl) if rev else tl)
    r = jnp.dot(bbr, czr, precision=hp, preferred_element_type=F32) - jnp.dot(bbi, czi, precision=hp,
                                                                                preferred_element_type=F32)
    lane_r = lax.broadcasted_iota(jnp.int32, (cg, w), 1)
    for s in range(ll):
        if rev:
            blk = jnp.where(lane_r < cg * (s + 1), pltpu.roll(r, (w - cg * (ll - 1 - s)) % w, 1), 0.0)
        else:
            blk = jnp.where(lane_r >= cg * s, pltpu.roll(r, cg * s, 1), 0.0)
        k_ref[0, s * cg:(s + 1) * cg, :] = blk.astype(k_ref.dtype)
    c1r, c1i = cz((ll - tl) if rev else (tl + 1.0))
    cm_ref[0, 0:p, :] = c1r.astype(cm_ref.dtype)
    cm_ref[0, p:2 * p, :] = (-c1i).astype(cm_ref.dtype)
    srow = (lax.broadcasted_iota(jnp.int32, (w, p), 0) >> sh).astype(F32)
    zr, zi = _cexp(arr, air, dt, srow if rev else (ll - 1.0 - srow))
    tbr, tbi = jnp.tile(bbr, (ll, 1)), jnp.tile(bbi, (ll, 1))
    bm_ref[0, :, 0:p] = (zr * tbr - zi * tbi).astype(bm_ref.dtype)
    bm_ref[0, :, p:2 * p] = (zr * tbi + zi * tbr).astype(bm_ref.dtype)
    ek = (ll << lax.broadcasted_iota(jnp.int32, (8, p), 0)).astype(F32)
    pr, pi = _cexp(arr, air, dt, ek)
    ap_ref[0, :, 0:p] = pr
    ap_ref[0, :, p:2 * p] = pi


def s5_prep(a_re, a_im, log_dt, b_re, b_im, c_re, c_im, rev):
    g, p = a_re.shape
    cg, w = S5_GROUP_DIM, S5_W
    col = lambda t: t.reshape(g, p, 1).astype(F32)
    row = lambda t: t.reshape(g, 1, p).astype(F32)
    tr = lambda t: jnp.swapaxes(t, 1, 2).astype(F32)
    spec = lambda s: pl.BlockSpec((1,) + s, lambda i: (i, 0, 0))
    return pl.pallas_call(
        functools.partial(_s5_prep_kernel, rev=rev),
        out_shape=(jax.ShapeDtypeStruct((g, w, w), BF16), jax.ShapeDtypeStruct((g, w, 2 * p), BF16),
                   jax.ShapeDtypeStruct((g, 2 * p, w), BF16), jax.ShapeDtypeStruct((g, 8, 2 * p), F32)),
        grid=(g,),
        in_specs=[spec((p, 1)), spec((p, 1)), spec((1, p)), spec((1, p)), spec((1, 1)),
                  spec((cg, p)), spec((cg, p)), spec((p, cg)), spec((p, cg))],
        out_specs=(spec((w, w)), spec((w, 2 * p)), spec((2 * p, w)), spec((8, 2 * p))),
        compiler_params=_params("parallel"),
        name="s5_prep",
    )(col(a_re), col(a_im), row(a_re), row(a_im), log_dt.reshape(g, 1, 1).astype(F32),
      tr(b_re), tr(b_im), tr(c_re), tr(c_im))


def _s5_chunk_scan(s, ap, nchunk, rev):
    n, w2 = s.shape
    p = w2 // 2
    m = lax.broadcasted_iota(jnp.int32, (n, w2), 0) & (nchunk - 1)
    lane = lax.broadcasted_iota(jnp.int32, (1, w2), 1)

    def shift(x, k):
        if rev:
            return jnp.where(m + k <= nchunk - 1, pltpu.roll(x, n - k, 0), 0.0)
        return jnp.where(m >= k, pltpu.roll(x, k, 0), 0.0)

    x = s
    k, lvl = 1, 0
    while k < nchunk:
        a = ap[lvl:lvl + 1, :]
        a1 = jnp.where(lane < p, a, pltpu.roll(a, p, 1))
        a2 = jnp.where(lane < p, -pltpu.roll(a, p, 1), a)
        xs = shift(x, k)
        x = x + a1 * xs + a2 * pltpu.roll(xs, p, 1)
        k, lvl = 2 * k, lvl + 1
    return shift(x, 1)


def _s5_main_kernel(u_ref, kf_ref, kb_ref, bf_ref, bb_ref, cf_ref, cb_ref, af_ref, ab_ref, y_ref, *, nchunk):
    u = u_ref[0]
    y = jnp.dot(u, kf_ref[0], preferred_element_type=F32) + jnp.dot(u, kb_ref[0], preferred_element_type=F32)
    for bm_ref, cm_ref, ap_ref, rev in ((bf_ref, cf_ref, af_ref, False), (bb_ref, cb_ref, ab_ref, True)):
        s = jnp.dot(u, bm_ref[0], preferred_element_type=F32)
        xin = _s5_chunk_scan(s, ap_ref[0], nchunk, rev)
        y = y + jnp.dot(xin.astype(BF16), cm_ref[0], preferred_element_type=F32)
    y_ref[0] = y


def s5_main(u, prep_f, prep_b, nchunk):
    g, n, w = u.shape
    p2 = 2 * S5_STATE
    spec = lambda s: pl.BlockSpec((1,) + s, lambda i: (i, 0, 0))
    kf, bf, cf, af = prep_f
    kb, bb, cb, ab = prep_b
    return pl.pallas_call(
        functools.partial(_s5_main_kernel, nchunk=nchunk),
        out_shape=jax.ShapeDtypeStruct((g, n, w), F32),
        grid=(g,),
        in_specs=[spec((n, w)), spec((w, w)), spec((w, w)), spec((w, p2)), spec((w, p2)),
                  spec((p2, w)), spec((p2, w)), spec((8, p2)), spec((8, p2))],
        out_specs=spec((n, w)),
        compiler_params=_params("parallel"),
        name="s5_main",
    )(u, kf, kb, bf, bb, cf, cb, af, ab)


def _s5_out_kernel(x_ref, g_ref, y_ref, d_ref, wa_ref, wb_ref, o_ref, a_ref):
    @pl.when(pl.program_id(1) == 0)
    def _():
        h = _rms(x_ref[...], g_ref[...])
        a_ref[...] = _gelu_tanh(y_ref[...] + d_ref[...] * h).astype(BF16)

    a = a_ref[...]
    ya = jnp.dot(a, wa_ref[...], preferred_element_type=F32)
    yb = jnp.dot(a, wb_ref[...], preferred_element_type=F32)
    tn = ya.shape[1]
    j = pl.program_id(1)
    res = x_ref[:, pl.ds(pl.multiple_of(j * tn, tn), tn)]
    o_ref[...] = res + ya * _sigmoid(yb)


def s5_out(x, g, y, d_skip, w_glu, tm=512, tn=512):
    m, d = x.shape
    n = w_glu.shape[1] // 2
    nb = n // tn
    return pl.pallas_call(
        _s5_out_kernel,
        out_shape=jax.ShapeDtypeStruct((m, n), F32),
        grid=(m // tm, nb),
        in_specs=[
            pl.BlockSpec((tm, d), lambda i, j: (i, 0)),
            pl.BlockSpec((1, d), lambda i, j: (0, 0)),
            pl.BlockSpec((tm, d), lambda i, j: (i, 0)),
            pl.BlockSpec((1, d), lambda i, j: (0, 0)),
            pl.BlockSpec((d, tn), lambda i, j: (0, j)),
            pl.BlockSpec((d, tn), lambda i, j: (0, j + nb)),
        ],
        out_specs=pl.BlockSpec((tm, tn), lambda i, j: (i, j)),
        scratch_shapes=[pltpu.VMEM((tm, d), BF16)],
        compiler_params=_params("parallel", "arbitrary"),
        name="s5_out",
    )(x, g.reshape(1, d), y, d_skip.reshape(1, d), w_glu, w_glu)


def s5_mixer(x, gmix, a_re, a_im, log_dt, b_re, b_im, c_re, c_im, d_skip, w_glu, bsz, seq):
    m, d = x.shape
    groups = d // S5_GROUP_DIM
    nchunk = seq // S5_L
    h = rmsnorm(x, gmix, out_dtype=BF16)
    u = h.reshape(m // S5_L, S5_L, groups, S5_GROUP_DIM).transpose(2, 0, 1, 3).reshape(groups, m // S5_L, S5_W)
    prep_f = s5_prep(a_re[0], a_im[0], log_dt[0], b_re[0], b_im[0], c_re[0], c_im[0], rev=False)
    prep_b = s5_prep(a_re[1], a_im[1], log_dt[1], b_re[1], b_im[1], c_re[1], c_im[1], rev=True)
    y = s5_main(u, prep_f, prep_b, nchunk)
    y = y.reshape(groups, m // S5_L, S5_L, S5_GROUP_DIM).transpose(1, 2, 0, 3).reshape(m, d)
    return s5_out(x, gmix, y, d_skip, w_glu.astype(BF16))


def _na_core(qkv, rpb, bsz, seq):
    d = qkv.shape[-1] // 3
    dh = d // NA_HEADS
    rows = seq // GRID_W
    win_r = min(NA_WIN_R, rows)
    q, k, v = jnp.split(qkv.astype(F32), 3, axis=-1)
    grid = lambda t: t.reshape(bsz, rows, GRID_W, NA_HEADS, dh)
    q, k, v = grid(q) * dh ** -0.5, grid(k), grid(v)
    row_start = np.clip(np.arange(rows) - win_r // 2, 0, rows - win_r)
    cols = np.arange(GRID_W)
    col_start = np.clip(cols - NA_WIN_C // 2, 0, GRID_W - NA_WIN_C)
    col_valid = (cols[None, :] >= col_start[:, None]) & (cols[None, :] < col_start[:, None] + NA_WIN_C)
    dc_idx = np.clip(cols[None, :] - cols[:, None] + NA_WIN_C - 1, 0, 2 * NA_WIN_C - 2)
    bias_c = jnp.where(col_valid, rpb[:, :, dc_idx].astype(F32), NEG_INF)
    bias_c = jnp.transpose(bias_c, (0, 2, 1, 3))
    dr_idx = row_start[:, None] + np.arange(win_r)[None, :] - np.arange(rows)[:, None] + NA_WIN_R - 1

    def one_row(args):
        q_r, r0, dri = args
        k_w = lax.dynamic_slice_in_dim(k, r0, win_r, axis=1)
        v_w = lax.dynamic_slice_in_dim(v, r0, win_r, axis=1)
        s = jnp.einsum('bqhd,bajhd->bhqaj', q_r, k_w, preferred_element_type=F32)
        s = s + jnp.take(bias_c, dri, axis=2)
        pr = jax.nn.softmax(s.reshape(bsz, NA_HEADS, GRID_W, win_r * GRID_W), axis=-1).reshape(s.shape)
        return jnp.einsum('bhqaj,bajhd->bqhd', pr, v_w)

    out = lax.map(one_row, (jnp.moveaxis(q, 1, 0), jnp.asarray(row_start, jnp.int32), jnp.asarray(dr_idx, jnp.int32)))
    return jnp.moveaxis(out, 0, 1).reshape(bsz * seq, d)


def _sg_core(uv, sg_norm, w_s, b_s, bsz, seq):
    width = uv.shape[-1] // 2
    n = seq // SG_CHUNK
    u, v = uv[:, :width], uv[:, width:]
    vf = v.astype(F32)
    vn = vf * lax.rsqrt(jnp.mean(vf * vf, axis=-1, keepdims=True) + EPS) * sg_norm
    vn = vn.reshape(bsz, n, SG_CHUNK, SG_GROUPS, width // SG_GROUPS)
    mixed = jnp.einsum('gts,bnsgc->bntgc', w_s, vn) + b_s.T[:, :, None]
    return u * mixed.reshape(bsz * seq, width)


def _l2norm(t):
    return t * lax.rsqrt(jnp.sum(t * t, axis=-1, keepdims=True) + EPS)


def _dwconv(x, w):
    width = w.shape[0]
    half = width // 2
    seq = x.shape[1]
    xp = jnp.pad(x, ((0, 0), (half, half), (0, 0)))
    return sum(xp[:, k:k + seq] * w[k] for k in range(width))


def _gdn_scan(q, k, v, g, beta):
    bsz, seq = q.shape[:2]
    n = seq // GDN_CHUNK
    rep = GDN_V_HEADS // GDN_QK_HEADS
    tri_incl = np.tril(np.ones((GDN_CHUNK, GDN_CHUNK), bool))
    tri_strict = np.tril(np.ones((GDN_CHUNK, GDN_CHUNK), bool), -1)
    eye = jnp.eye(GDN_CHUNK, dtype=F32)
    hp = lax.Precision.HIGHEST

    def chunks(t):
        return jnp.moveaxis(t.reshape(bsz, n, GDN_CHUNK, *t.shape[2:]), 1, 0)

    def tr(t):
        return jnp.swapaxes(t, -1, -2)

    def step(state, inp):
        qc, kc, vc, gc, bc = inp
        qh = jnp.swapaxes(jnp.repeat(qc, rep, axis=2), 1, 2)
        kh = jnp.swapaxes(jnp.repeat(kc, rep, axis=2), 1, 2)
        vh = jnp.swapaxes(vc, 1, 2)
        gam = jnp.cumsum(jnp.swapaxes(gc, 1, 2), axis=-1)
        bet = jnp.swapaxes(bc, 1, 2)
        decay = jnp.exp(jnp.where(tri_incl, gam[..., :, None] - gam[..., None, :], -jnp.inf))
        m = jnp.where(tri_strict, jnp.matmul(kh, tr(kh), precision=hp) * decay, 0.0) * bet[..., :, None]
        e_gam = jnp.exp(gam)
        rhs = jnp.concatenate([kh * (bet * e_gam)[..., None], vh * bet[..., None]], axis=-1)
        sol = lax.linalg.triangular_solve(eye + m, rhs, left_side=True, lower=True, unit_diagonal=True)
        w_mat, u_val = sol[..., :GDN_DK], sol[..., GDN_DK:]
        u = u_val - jnp.matmul(w_mat, state, precision=hp)
        o = jnp.matmul(qh * e_gam[..., None], state, precision=hp) + jnp.matmul(
            jnp.matmul(qh, tr(kh), precision=hp) * decay, u, precision=hp)
        k_dec = kh * jnp.exp(gam[..., -1:] - gam)[..., None]
        new_state = e_gam[..., -1][..., None, None] * state + jnp.matmul(tr(k_dec), u, precision=hp)
        return new_state, o

    state0 = jnp.zeros((bsz, GDN_V_HEADS, GDN_DK, GDN_DV), F32)
    _, o = lax.scan(step, state0, (chunks(q), chunks(k), chunks(v), chunks(g), chunks(beta)))
    return jnp.transpose(o, (1, 0, 3, 2, 4)).reshape(bsz, seq, GDN_V_HEADS, GDN_DV)


def _gdn_core(proj, conv_w, a_log, dt_bias, out_norm, bsz, seq):
    qk_w = GDN_QK_HEADS * GDN_DK
    v_w = GDN_V_HEADS * GDN_DV
    conv_width = 2 * qk_w + v_w
    proj = proj.reshape(bsz, seq, -1).astype(F32)
    qkv = jax.nn.silu(_dwconv(proj[..., :conv_width], conv_w))
    z = proj[..., conv_width:conv_width + v_w]
    ab = proj[..., conv_width + v_w:].reshape(bsz, seq, 2, 2, GDN_V_HEADS)
    q = _l2norm(qkv[..., :qk_w].reshape(bsz, seq, GDN_QK_HEADS, GDN_DK)) * GDN_DK ** -0.5
    k = _l2norm(qkv[..., qk_w:2 * qk_w].reshape(bsz, seq, GDN_QK_HEADS, GDN_DK))
    v = qkv[..., 2 * qk_w:].reshape(bsz, seq, GDN_V_HEADS, GDN_DV)
    decay_rate = jnp.exp(a_log.astype(F32))
    g = -decay_rate * jax.nn.softplus(ab[:, :, :, 0] + dt_bias.astype(F32))
    beta = jax.nn.sigmoid(ab[:, :, :, 1])
    o_fwd = _gdn_scan(q, k, v, g[:, :, 0], beta[:, :, 0])
    o_bwd = jnp.flip(_gdn_scan(jnp.flip(q, 1), jnp.flip(k, 1), jnp.flip(v, 1),
                               jnp.flip(g[:, :, 1], 1), jnp.flip(beta[:, :, 1], 1)), 1)
    zg = jax.nn.silu(z.reshape(bsz, seq, GDN_V_HEADS, GDN_DV))
    o = o_fwd + o_bwd
    o = o * lax.rsqrt(jnp.mean(o * o, axis=-1, keepdims=True) + EPS) * out_norm * zg
    return o.reshape(bsz * seq, v_w)


def _s5_direction(u, a_re, a_im, log_dt, b_re, b_im, c_re, c_im):
    bsz, seq, groups = u.shape[:3]
    n = seq // S5_CHUNK
    lam = lax.complex(a_re.astype(F32), a_im.astype(F32))
    dt = jnp.exp(log_dt.astype(F32))[:, None]
    a_bar = jnp.exp(lam * dt)
    b_bar = ((a_bar - 1.0) / lam)[..., None] * lax.complex(b_re.astype(F32), b_im.astype(F32))
    c = lax.complex(c_re.astype(F32), c_im.astype(F32))
    bu = jnp.einsum('gpc,bsgc->bsgp', b_bar, u.astype(jnp.complex64), precision=lax.Precision.HIGHEST)
    bu = jnp.moveaxis(bu.reshape(bsz, n, S5_CHUNK, groups, S5_STATE), 1, 0)
    a_elems = jnp.broadcast_to(a_bar, (bsz, S5_CHUNK, groups, S5_STATE))
    powers = jnp.exp(lam[None] * dt[None] * jnp.arange(1, S5_CHUNK + 1, dtype=F32)[:, None, None])

    def binop(e1, e2):
        return (e2[0] * e1[0], e2[0] * e1[1] + e2[1])

    def step(x_prev, bu_c):
        _, xs = lax.associative_scan(binop, (a_elems, bu_c), axis=1)
        xs = xs + powers[None] * x_prev[:, None]
        y = jnp.einsum('gcp,blgp->blgc', c, xs, precision=lax.Precision.HIGHEST).real
        return xs[:, -1], y

    x0 = jnp.zeros((bsz, groups, S5_STATE), jnp.complex64)
    _, ys = lax.scan(step, x0, bu)
    return jnp.moveaxis(ys, 0, 1).reshape(bsz, seq, groups, S5_GROUP_DIM)


def _s5_core(h, a_re, a_im, log_dt, b_re, b_im, c_re, c_im, d_skip, bsz, seq):
    d = h.shape[-1]
    groups = d // S5_GROUP_DIM
    u = h.reshape(bsz, seq, groups, S5_GROUP_DIM)
    y_f = _s5_direction(u, a_re[0], a_im[0], log_dt[0], b_re[0], b_im[0], c_re[0], c_im[0])
    y_b = jnp.flip(_s5_direction(jnp.flip(u, 1), a_re[1], a_im[1], log_dt[1], b_re[1], b_im[1], c_re[1], c_im[1]), 1)
    y = (y_f + y_b).reshape(bsz * seq, d) + d_skip * h
    return jax.nn.gelu(y)


def _trunk(x, p, w, bsz, seq):
    depth = w['norm_mix'].shape[0]
    bf = lambda t: t.astype(BF16)
    for i in range(depth):
        kind, j = i % N_MIXERS, i // N_MIXERS
        gmix = w['norm_mix'][i]
        if kind == 0:
            qkv = norm_matmul(x, gmix, bf(w['na_w_qkv'][j]), out_dtype=BF16)
            att = na_attention(qkv, w['na_rpb'][j], bsz, seq)
            x = matmul_res(att, bf(w['na_w_o'][j]), x)
        elif kind == 1:
            uv = norm_matmul(x, gmix, bf(w['sg_w_in'][j]), act="gelu", out_dtype=BF16)
            x = sg_out(uv, w['sg_norm'][j], w['sg_w_s'][j], w['sg_b_s'][j], bf(w['sg_w_o'][j]), x)
        elif kind == 2:
            x = gated_deltanet(x, gmix, w['gdn_w_in'][j], w['gdn_conv_w'][j], w['gdn_a_log'][j],
                               w['gdn_dt_bias'][j], w['gdn_out_norm'][j], w['gdn_w_o'][j], bsz, seq)
        else:
            x = s5_mixer(x, gmix, w['s5_a_re'][j], w['s5_a_im'][j], w['s5_log_dt'][j], w['s5_b_re'][j],
                         w['s5_b_im'][j], w['s5_c_re'][j], w['s5_c_im'][j], w['s5_d'][j], w['s5_w_glu'][j], bsz, seq)
        x = ffn(x, w['norm_ffn'][i], bf(w['ffn_w_gu'][i]), w['ffn_conv_w'][i], w['ffn_conv_b'][i],
                bf(w['ffn_w_down'][i]), seq)
        x = ple(x, p[i], w['norm_ple'][i], bf(w['ple_w_gate'][i]), bf(w['ple_w_proj'][i]))
    return x


def kernel(x_prompt, x_sample, p_prompt, p_sample, norm_mix, norm_ffn, norm_ple, final_norm, na_w_qkv, na_w_o, na_rpb, sg_w_in, sg_norm, sg_w_s, sg_b_s, sg_w_o, gdn_w_in, gdn_conv_w, gdn_a_log, gdn_dt_bias, gdn_out_norm, gdn_w_o, s5_a_re, s5_a_im, s5_log_dt, s5_b_re, s5_b_im, s5_c_re, s5_c_im, s5_d, s5_w_glu, ffn_w_gu, ffn_conv_w, ffn_conv_b, ffn_w_down, ple_w_proj, ple_w_gate):
    w = dict(norm_mix=norm_mix, norm_ffn=norm_ffn, norm_ple=norm_ple, final_norm=final_norm,
             na_w_qkv=na_w_qkv, na_w_o=na_w_o, na_rpb=na_rpb,
             sg_w_in=sg_w_in, sg_norm=sg_norm, sg_w_s=sg_w_s, sg_b_s=sg_b_s, sg_w_o=sg_w_o,
             gdn_w_in=gdn_w_in, gdn_conv_w=gdn_conv_w, gdn_a_log=gdn_a_log, gdn_dt_bias=gdn_dt_bias,
             gdn_out_norm=gdn_out_norm, gdn_w_o=gdn_w_o,
             s5_a_re=s5_a_re, s5_a_im=s5_a_im, s5_log_dt=s5_log_dt, s5_b_re=s5_b_re, s5_b_im=s5_b_im,
             s5_c_re=s5_c_re, s5_c_im=s5_c_im, s5_d=s5_d, s5_w_glu=s5_w_glu,
             ffn_w_gu=ffn_w_gu, ffn_conv_w=ffn_conv_w, ffn_conv_b=ffn_conv_b, ffn_w_down=ffn_w_down,
             ple_w_proj=ple_w_proj, ple_w_gate=ple_w_gate)
    b1, seq, d = x_prompt.shape
    b2 = x_sample.shape[0]
    bsz = b1 + b2
    x = jnp.concatenate([x_prompt, x_sample], axis=0).reshape(bsz * seq, d)
    p = jnp.concatenate([p_prompt, p_sample], axis=1).reshape(p_prompt.shape[0], bsz * seq, -1)
    x = _trunk(x, p, w, bsz, seq)
    y1 = rmsnorm(x, final_norm, row0=0, rows=b1 * seq).reshape(b1, seq, d)
    y2 = rmsnorm(x, final_norm, row0=b1 * seq, rows=b2 * seq).reshape(b2, seq, d)
    return (y1, y2)
```

```python
import functools
import math

import jax
import jax.numpy as jnp
import numpy as np
from jax import lax
from jax.experimental import pallas as pl
from jax.experimental.pallas import tpu as pltpu

F32 = jnp.float32
BF16 = jnp.bfloat16

EPS = 1e-6
NEG_INF = -1e30
GRID_W = 64
NA_HEADS = 16
NA_WIN_R = 8
NA_WIN_C = 16
SG_CHUNK = 128
SG_GROUPS = 16
GDN_QK_HEADS = 16
GDN_V_HEADS = 32
GDN_DK = 128
GDN_DV = 128
GDN_CHUNK = 64
S5_GROUP_DIM = 16
S5_STATE = 64
S5_CHUNK = 128
N_MIXERS = 4

VMEM_LIMIT_BYTES = 56 * 1024 * 1024
HALO = 16


def _params(*sem):
    return pltpu.CompilerParams(dimension_semantics=sem, vmem_limit_bytes=VMEM_LIMIT_BYTES)


def _rms(x, g):
    return x * lax.rsqrt(jnp.mean(x * x, axis=-1, keepdims=True) + EPS) * g


def _gelu_tanh(x):
    return 0.5 * x * (1.0 + jnp.tanh(math.sqrt(2.0 / math.pi) * (x + 0.044715 * (x * x * x))))


def _sigmoid(x):
    return 1.0 / (1.0 + jnp.exp(-x))


def _rmsnorm_kernel(x_ref, g_ref, o_ref):
    o_ref[...] = _rms(x_ref[...], g_ref[...]).astype(o_ref.dtype)


def rmsnorm(x, g, out_dtype=F32, tm=512, row0=0, rows=None):
    m, d = x.shape
    rows = m if rows is None else rows
    off = row0 // tm
    return pl.pallas_call(
        _rmsnorm_kernel,
        out_shape=jax.ShapeDtypeStruct((rows, d), out_dtype),
        grid=(rows // tm,),
        in_specs=[pl.BlockSpec((tm, d), lambda i: (i + off, 0)), pl.BlockSpec((1, d), lambda i: (0, 0))],
        out_specs=pl.BlockSpec((tm, d), lambda i: (i, 0)),
        compiler_params=_params("parallel"),
        name="rmsnorm",
    )(x, g.reshape(1, d))


def _norm_matmul_kernel(x_ref, g_ref, w_ref, o_ref, hn_ref, *, act):
    @pl.when(pl.program_id(1) == 0)
    def _():
        hn_ref[...] = _rms(x_ref[...], g_ref[...]).astype(BF16)

    y = jnp.dot(hn_ref[...], w_ref[...], preferred_element_type=F32)
    if act == "gelu":
        y = _gelu_tanh(y)
    o_ref[...] = y.astype(o_ref.dtype)


def norm_matmul(x, g, w, act=None, out_dtype=F32, tm=512, tn=1024):
    m, d = x.shape
    n = w.shape[1]
    tn = next(t for t in (tn, 512, 256, 128) if n % t == 0)
    return pl.pallas_call(
        functools.partial(_norm_matmul_kernel, act=act),
        out_shape=jax.ShapeDtypeStruct((m, n), out_dtype),
        grid=(m // tm, n // tn),
        in_specs=[
            pl.BlockSpec((tm, d), lambda i, j: (i, 0)),
            pl.BlockSpec((1, d), lambda i, j: (0, 0)),
            pl.BlockSpec((d, tn), lambda i, j: (0, j)),
        ],
        out_specs=pl.BlockSpec((tm, tn), lambda i, j: (i, j)),
        scratch_shapes=[pltpu.VMEM((tm, d), BF16)],
        compiler_params=_params("parallel", "arbitrary"),
        name="norm_matmul",
    )(x, g.reshape(1, d), w)


def _matmul_res_kernel(a_ref, w_ref, r_ref, o_ref):
    o_ref[...] = r_ref[...] + jnp.dot(a_ref[...].astype(BF16), w_ref[...], preferred_element_type=F32)


def matmul_res(a, w, res, tm=512, tn=512):
    m, k = a.shape
    n = w.shape[1]
    return pl.pallas_call(
        _matmul_res_kernel,
        out_shape=jax.ShapeDtypeStruct((m, n), F32),
        grid=(m // tm, n // tn),
        in_specs=[
            pl.BlockSpec((tm, k), lambda i, j: (i, 0)),
            pl.BlockSpec((k, tn), lambda i, j: (0, j)),
            pl.BlockSpec((tm, tn), lambda i, j: (i, j)),
        ],
        out_specs=pl.BlockSpec((tm, tn), lambda i, j: (i, j)),
        compiler_params=_params("parallel", "arbitrary"),
        name="matmul_res",
    )(a, w, res)


def _glu_res_kernel(a_ref, wa_ref, wb_ref, r_ref, o_ref):
    a = a_ref[...].astype(BF16)
    ya = jnp.dot(a, wa_ref[...], preferred_element_type=F32)
    yb = jnp.dot(a, wb_ref[...], preferred_element_type=F32)
    o_ref[...] = r_ref[...] + ya * _sigmoid(yb)


def glu_res(a, w, res, tm=512, tn=512):
    m, k = a.shape
    n = w.shape[1] // 2
    nb = n // tn
    return pl.pallas_call(
        _glu_res_kernel,
        out_shape=jax.ShapeDtypeStruct((m, n), F32),
        grid=(m // tm, nb),
        in_specs=[
            pl.BlockSpec((tm, k), lambda i, j: (i, 0)),
            pl.BlockSpec((k, tn), lambda i, j: (0, j)),
            pl.BlockSpec((k, tn), lambda i, j: (0, j + nb)),
            pl.BlockSpec((tm, tn), lambda i, j: (i, j)),
        ],
        out_specs=pl.BlockSpec((tm, tn), lambda i, j: (i, j)),
        compiler_params=_params("parallel", "arbitrary"),
        name="glu_res",
    )(a, w, w, res)


def _ffn_kernel(x_ref, xp_ref, xn_ref, g_ref, wg_ref, wu_ref, cw_ref, cb_ref, wd_ref, o_ref, hn_ref, acc_ref,
                *, tm, seq):
    i = pl.program_id(0)
    j = pl.program_id(1)

    @pl.when(j == 0)
    def _():
        g = g_ref[...]
        prev_ok = jnp.where((i * tm) % seq != 0, 1.0, 0.0)
        next_ok = jnp.where(((i + 1) * tm) % seq != 0, 1.0, 0.0)
        hn_ref[0:HALO, :] = (_rms(xp_ref[...], g) * prev_ok).astype(BF16)
        hn_ref[HALO:HALO + tm, :] = _rms(x_ref[...], g).astype(BF16)
        hn_ref[HALO + tm:, :] = (_rms(xn_ref[...], g) * next_ok).astype(BF16)
        acc_ref[...] = jnp.zeros_like(acc_ref)

    rows = tm + 2 * HALO
    gate = jnp.dot(hn_ref[...], wg_ref[...], preferred_element_type=F32)
    up = jnp.dot(hn_ref[HALO:HALO + tm, :], wu_ref[...], preferred_element_type=F32)
    cw = cw_ref[...]
    g_prev = pltpu.roll(gate, 1, 0)[HALO:HALO + tm]
    g_next = pltpu.roll(gate, rows - 1, 0)[HALO:HALO + tm]
    gc = cw[0:1] * g_prev + cw[1:2] * gate[HALO:HALO + tm] + cw[2:3] * g_next + cb_ref[...]
    act = (gc * _sigmoid(gc) * up).astype(BF16)
    acc_ref[...] += jnp.dot(act, wd_ref[...], preferred_element_type=F32)

    @pl.when(j == pl.num_programs(1) - 1)
    def _():
        o_ref[...] = x_ref[...] + acc_ref[...]


def ffn(x, g, w_gu, conv_w, conv_b, w_down, seq, tm=512, tf=512):
    m, d = x.shape
    f = w_down.shape[0]
    nf = f // tf
    hb = tm // HALO
    last = m // HALO - 1
    return pl.pallas_call(
        functools.partial(_ffn_kernel, tm=tm, seq=seq),
        out_shape=jax.ShapeDtypeStruct((m, d), F32),
        grid=(m // tm, nf),
        in_specs=[
            pl.BlockSpec((tm, d), lambda i, j: (i, 0)),
            pl.BlockSpec((HALO, d), lambda i, j: (jnp.maximum(i * hb - 1, 0), 0)),
            pl.BlockSpec((HALO, d), lambda i, j: (jnp.minimum((i + 1) * hb, last), 0)),
            pl.BlockSpec((1, d), lambda i, j: (0, 0)),
            pl.BlockSpec((d, tf), lambda i, j: (0, j)),
            pl.BlockSpec((d, tf), lambda i, j: (0, j + nf)),
            pl.BlockSpec((3, tf), lambda i, j: (0, j)),
            pl.BlockSpec((1, tf), lambda i, j: (0, j)),
            pl.BlockSpec((tf, d), lambda i, j: (j, 0)),
        ],
        out_specs=pl.BlockSpec((tm, d), lambda i, j: (i, 0)),
        scratch_shapes=[pltpu.VMEM((tm + 2 * HALO, d), BF16), pltpu.VMEM((tm, d), F32)],
        compiler_params=_params("parallel", "arbitrary"),
        name="ffn",
    )(x, x, x, g.reshape(1, d), w_gu, w_gu, conv_w, conv_b.reshape(1, f), w_down)


def _ple_kernel(x_ref, p_ref, g_ref, wg_ref, wp_ref, o_ref):
    x = x_ref[...]
    hn = _rms(x, g_ref[...]).astype(BF16)
    gate = _sigmoid(jnp.dot(hn, wg_ref[...], preferred_element_type=F32))
    proj = jnp.dot(p_ref[...].astype(BF16), wp_ref[...], preferred_element_type=F32)
    o_ref[...] = x + gate * proj


def ple(x, p, g, w_gate, w_proj, tm=512):
    m, d = x.shape
    pd = p.shape[1]
    return pl.pallas_call(
        _ple_kernel,
        out_shape=jax.ShapeDtypeStruct((m, d), F32),
        grid=(m // tm,),
        in_specs=[
            pl.BlockSpec((tm, d), lambda i: (i, 0)),
            pl.BlockSpec((tm, pd), lambda i: (i, 0)),
            pl.BlockSpec((1, d), lambda i: (0, 0)),
            pl.BlockSpec((d, d), lambda i: (0, 0)),
            pl.BlockSpec((pd, d), lambda i: (0, 0)),
        ],
        out_specs=pl.BlockSpec((tm, d), lambda i: (i, 0)),
        compiler_params=_params("parallel"),
        name="ple",
    )(x, p, g.reshape(1, d), w_gate, w_proj)


GDN_BLOCK = 256


def _conv_in_kernel(x_ref, xp_ref, xn_ref, g_ref, w_ref, cw_ref, o_ref, hn_ref, *, tm, seq, nq, nqk, scale):
    i = pl.program_id(0)
    j = pl.program_id(1)

    @pl.when(j == 0)
    def _():
        g = g_ref[...]
        prev_ok = jnp.where((i * tm) % seq != 0, 1.0, 0.0)
        next_ok = jnp.where(((i + 1) * tm) % seq != 0, 1.0, 0.0)
        hn_ref[0:HALO, :] = (_rms(xp_ref[...], g) * prev_ok).astype(BF16)
        hn_ref[HALO:HALO + tm, :] = _rms(x_ref[...], g).astype(BF16)
        hn_ref[HALO + tm:, :] = (_rms(xn_ref[...], g) * next_ok).astype(BF16)

    rows = tm + 2 * HALO
    y = jnp.dot(hn_ref[...], w_ref[...], preferred_element_type=F32)
    cw = cw_ref[...]
    y_prev = pltpu.roll(y, 1, 0)[HALO:HALO + tm]
    y_next = pltpu.roll(y, rows - 1, 0)[HALO:HALO + tm]
    c = cw[0:1] * y_prev + cw[1:2] * y[HALO:HALO + tm] + cw[2:3] * y_next
    c = c * _sigmoid(c)
    tn = c.shape[1]
    for s in range(tn // 128):
        cs = c[:, s * 128:(s + 1) * 128]
        inv = lax.rsqrt(jnp.sum(cs * cs, axis=-1, keepdims=True) + EPS)
        f = jnp.where(j < nq, inv * scale, jnp.where(j < nqk, inv, 1.0))
        o_ref[:, s * 128:(s + 1) * 128] = (cs * f).astype(o_ref.dtype)


def conv_in(x, g, w, conv_w, seq, n_q, n_qk, scale, tm=512, tn=512):
    m, d = x.shape
    n = w.shape[1]
    hb = tm // HALO
    last = m // HALO - 1
    return pl.pallas_call(
        functools.partial(_conv_in_kernel, tm=tm, seq=seq, nq=n_q // tn, nqk=n_qk // tn, scale=scale),
        out_shape=jax.ShapeDtypeStruct((m, n), BF16),
        grid=(m // tm, n // tn),
        in_specs=[
            pl.BlockSpec((tm, d), lambda i, j: (i, 0)),
            pl.BlockSpec((HALO, d), lambda i, j: (jnp.maximum(i * hb - 1, 0), 0)),
            pl.BlockSpec((HALO, d), lambda i, j: (jnp.minimum((i + 1) * hb, last), 0)),
            pl.BlockSpec((1, d), lambda i, j: (0, 0)),
            pl.BlockSpec((d, tn), lambda i, j: (0, j)),
            pl.BlockSpec((3, tn), lambda i, j: (0, j)),
        ],
        out_specs=pl.BlockSpec((tm, tn), lambda i, j: (i, j)),
        scratch_shapes=[pltpu.VMEM((tm + 2 * HALO, d), BF16)],
        compiler_params=_params("parallel", "arbitrary"),
        name="gdn_conv_in",
    )(x, x, x, g.reshape(1, d), w, conv_w)


def _gdn_gates_kernel(x_ref, g_ref, w_ref, alog_ref, bias_ref, isg_ref, o_ref):
    hn = _rms(x_ref[...], g_ref[...]).astype(BF16)
    y = jnp.dot(hn, w_ref[...], preferred_element_type=F32)
    t = y + bias_ref[...]
    softplus = jnp.maximum(t, 0.0) + jnp.log1p(jnp.exp(-jnp.abs(t)))
    gval = -jnp.exp(alog_ref[...]) * softplus
    o_ref[...] = jnp.where(isg_ref[...] > 0.5, gval, _sigmoid(y))


def gdn_gates(x, g, w_ab, a_log, dt_bias, tm=512):
    m, d = x.shape
    n = w_ab.shape[1]
    nh = a_log.shape[-1]
    zeros = jnp.zeros((2, 1, nh), F32)
    arrange = lambda t: jnp.concatenate([t.reshape(2, 1, nh).astype(F32), zeros], axis=1).reshape(1, n)
    isg = jnp.concatenate([jnp.ones((2, 1, nh), F32), zeros], axis=1).reshape(1, n)
    return pl.pallas_call(
        _gdn_gates_kernel,
        out_shape=jax.ShapeDtypeStruct((m, n), F32),
        grid=(m // tm,),
        in_specs=[
            pl.BlockSpec((tm, d), lambda i: (i, 0)),
            pl.BlockSpec((1, d), lambda i: (0, 0)),
            pl.BlockSpec((d, n), lambda i: (0, 0)),
            pl.BlockSpec((1, n), lambda i: (0, 0)),
            pl.BlockSpec((1, n), lambda i: (0, 0)),
            pl.BlockSpec((1, n), lambda i: (0, 0)),
        ],
        out_specs=pl.BlockSpec((tm, n), lambda i: (i, 0)),
        compiler_params=_params("parallel"),
        name="gdn_gates",
    )(x, g.reshape(1, d), w_ab, arrange(a_log), arrange(dt_bias), isg)


def _gdn_chains(q_ref, k_ref, kt_ref, v_ref, gc_ref, gr_ref, o_ref, rev):
    c_sz, lc = GDN_BLOCK, GDN_CHUNK
    hp = lax.Precision.HIGHEST
    d = 1 if rev else 0
    ii = lax.broadcasted_iota(jnp.int32, (c_sz, c_sz), 0)
    jj = lax.broadcasted_iota(jnp.int32, (c_sz, c_sz), 1)
    sh = lc.bit_length() - 1
    same = (ii >> sh) == (jj >> sh)
    incl = jnp.logical_and(same, (jj >= ii) if rev else (jj <= ii))
    eye = ii == jj
    cum = jnp.where(incl, 1.0, 0.0)
    blk = jnp.where(same, 1.0, 0.0)
    q = q_ref[...]
    k = k_ref[...]
    kf = k.astype(F32)
    qf = q.astype(F32)
    ktf = kt_ref[0].astype(F32)
    gc = gc_ref[0, 0][:, 4 * d:4 * d + 4]
    gr = gr_ref[0, 0][4 * d:4 * d + 4, :]
    gam_c = jnp.dot(cum, gc[:, 0:2], precision=hp, preferred_element_type=F32)
    gam_r = lax.dot_general(gr[0:2], cum, (((1,), (1,)), ((), ())), precision=hp,
                            preferred_element_type=F32)
    end_r = jnp.dot(gr[0:2], blk, precision=hp, preferred_element_type=F32)
    gram = lax.dot_general(k, k, (((1,), (1,)), ((), ())), preferred_element_type=F32)
    qk = lax.dot_general(q, k, (((1,), (1,)), ((), ())), preferred_element_type=F32)
    chains = []
    for hs in range(2):
        gcol = gam_c[:, hs:hs + 1]
        grow = gam_r[hs:hs + 1, :]
        bcol = gc[:, 2 + hs:3 + hs]
        dec = jnp.exp(jnp.where(incl, gcol - grow, NEG_INF))
        e_g = jnp.exp(gcol)
        vh = v_ref[:, hs * GDN_DV:(hs + 1) * GDN_DV].astype(F32)
        chains.append(dict(
            n=(-(gram * jnp.where(eye, 0.0, dec)) * bcol).astype(BF16),
            x=jnp.concatenate([kf * (bcol * e_g), vh * bcol], axis=1),
            attn=(qk * dec).astype(BF16),
            qe=qf * e_g,
            kdt=ktf * jnp.exp(end_r[hs:hs + 1, :] - grow),
            end=end_r[hs:hs + 1, :],
            o_ref=o_ref, hs=hs, idx=2 * d + hs, rev=rev))
    return chains


def _gdn_scan_kernel(qf_ref, kf_ref, ktf_ref, vf_ref, gcf_ref, grf_ref,
                     qb_ref, kb_ref, ktb_ref, vb_ref, gcb_ref, grb_ref, of_ref, ob_ref, s_ref):
    @pl.when(pl.program_id(2) == 0)
    def _():
        s_ref[...] = jnp.zeros_like(s_ref)

    c_sz, lc = GDN_BLOCK, GDN_CHUNK
    nchunk = c_sz // lc
    width = GDN_DK + GDN_DV
    chains = (_gdn_chains(qf_ref, kf_ref, ktf_ref, vf_ref, gcf_ref, grf_ref, of_ref, False)
              + _gdn_chains(qb_ref, kb_ref, ktb_ref, vb_ref, gcb_ref, grb_ref, ob_ref, True))
    nlev = lc.bit_length() - 1
    for lvl in range(nlev):
        for ch in chains:
            nb = ch['n']
            xb = ch['x'].astype(BF16)
            if lvl < nlev - 1:
                r = jnp.dot(nb, jnp.concatenate([xb, nb], axis=1), preferred_element_type=F32)
                ch['x'] = ch['x'] + r[:, :width]
                ch['n'] = r[:, width:].astype(BF16)
            else:
                ch['x'] = ch['x'] + jnp.dot(nb, xb, preferred_element_type=F32)
    jrow = lax.broadcasted_iota(jnp.int32, (1, c_sz), 1) >> nlev
    for ch in chains:
        wub = ch['x'].astype(BF16)
        awu = jnp.dot(ch['attn'], wub, preferred_element_type=F32)
        ch['qeff'] = (ch['qe'] - awu[:, :GDN_DK]).astype(BF16)
        ch['o_in'] = awu[:, GDN_DK:]
        ch['kwu'] = [jnp.dot(jnp.where(jrow == c, ch['kdt'], 0.0).astype(BF16), wub, preferred_element_type=F32)
                     for c in range(nchunk)]
        ch['s'] = s_ref[ch['idx']]
    for step in range(nchunk):
        for ch in chains:
            c = nchunk - 1 - step if ch['rev'] else step
            r0 = c * lc
            s = ch['s']
            sb = s.astype(BF16)
            o_c = ch['o_in'][r0:r0 + lc] + jnp.dot(ch['qeff'][r0:r0 + lc], sb, preferred_element_type=F32)
            ch['o_ref'][r0:r0 + lc, ch['hs'] * GDN_DV:(ch['hs'] + 1) * GDN_DV] = o_c.astype(ch['o_ref'].dtype)
            kwu = ch['kwu'][c]
            e_end = jnp.exp(ch['end'][:, r0:r0 + 1])
            ch['s'] = (e_end * s - jnp.dot(kwu[:, :GDN_DK].astype(BF16), sb, preferred_element_type=F32)
                       + kwu[:, GDN_DK:])
    for ch in chains:
        s_ref[ch['idx']] = ch['s']


def gdn_scan(qkv, kt, gcol, grow, bsz, seq):
    c_sz = GDN_BLOCK
    nb = seq // c_sz
    hq = GDN_QK_HEADS
    kcol = hq
    vcol = (2 * hq * GDN_DK) // (2 * GDN_DV)
    fwd = lambda b, h, c: c
    bwd = lambda b, h, c: nb - 1 - c

    def specs(pos):
        return [
            pl.BlockSpec((c_sz, GDN_DK), lambda b, h, c: (b * nb + pos(b, h, c), h)),
            pl.BlockSpec((c_sz, GDN_DK), lambda b, h, c: (b * nb + pos(b, h, c), kcol + h)),
            pl.BlockSpec((1, GDN_DK, c_sz), lambda b, h, c: (b, h, pos(b, h, c))),
            pl.BlockSpec((c_sz, 2 * GDN_DV), lambda b, h, c: (b * nb + pos(b, h, c), vcol + h)),
            pl.BlockSpec((1, 1, c_sz, 8), lambda b, h, c: (b, h, pos(b, h, c), 0)),
            pl.BlockSpec((1, 1, 8, c_sz), lambda b, h, c: (b, h, 0, pos(b, h, c))),
        ]

    out = jax.ShapeDtypeStruct((bsz * seq, GDN_V_HEADS * GDN_DV), BF16)
    return pl.pallas_call(
        _gdn_scan_kernel,
        out_shape=(out, out),
        grid=(bsz, hq, nb),
        in_specs=specs(fwd) + specs(bwd),
        out_specs=(
            pl.BlockSpec((c_sz, 2 * GDN_DV), lambda b, h, c: (b * nb + c, h)),
            pl.BlockSpec((c_sz, 2 * GDN_DV), lambda b, h, c: (b * nb + nb - 1 - c, h)),
        ),
        scratch_shapes=[pltpu.VMEM((4, GDN_DK, GDN_DV), F32)],
        compiler_params=_params("parallel", "parallel", "arbitrary"),
        name="gdn_scan",
    )(qkv, qkv, kt, qkv, gcol, grow, qkv, qkv, kt, qkv, gcol, grow)


def _gdn_out_kernel(of_ref, ob_ref, z_ref, gn_ref, w_ref, r_ref, o_ref, a_ref):
    @pl.when(pl.program_id(1) == 0)
    def _():
        gn = gn_ref[...]
        for h in range(GDN_V_HEADS):
            sl = slice(h * GDN_DV, (h + 1) * GDN_DV)
            o = of_ref[:, sl].astype(F32) + ob_ref[:, sl].astype(F32)
            z = z_ref[:, sl].astype(F32)
            a_ref[:, sl] = (_rms(o, gn) * (z * _sigmoid(z))).astype(BF16)

    o_ref[...] = r_ref[...] + jnp.dot(a_ref[...], w_ref[...], preferred_element_type=F32)


def gdn_out(o_f, o_b, z, out_norm, w_o, res, tm=512, tn=512):
    m, k = o_f.shape
    n = w_o.shape[1]
    return pl.pallas_call(
        _gdn_out_kernel,
        out_shape=jax.ShapeDtypeStruct((m, n), F32),
        grid=(m // tm, n // tn),
        in_specs=[
            pl.BlockSpec((tm, k), lambda i, j: (i, 0)),
            pl.BlockSpec((tm, k), lambda i, j: (i, 0)),
            pl.BlockSpec((tm, k), lambda i, j: (i, 0)),
            pl.BlockSpec((1, GDN_DV), lambda i, j: (0, 0)),
            pl.BlockSpec((k, tn), lambda i, j: (0, j)),
            pl.BlockSpec((tm, tn), lambda i, j: (i, j)),
        ],
        out_specs=pl.BlockSpec((tm, tn), lambda i, j: (i, j)),
        scratch_shapes=[pltpu.VMEM((tm, k), BF16)],
        compiler_params=_params("parallel", "arbitrary"),
        name="gdn_out",
    )(o_f, o_b, z, out_norm.reshape(1, GDN_DV), w_o, res)


def gated_deltanet(x, gmix, w_in, conv_w, a_log, dt_bias, out_norm, w_o, bsz, seq):
    qk_w = GDN_QK_HEADS * GDN_DK
    v_w = GDN_V_HEADS * GDN_DV
    cw = 2 * qk_w + v_w
    w_in = w_in.astype(BF16)
    qkv = conv_in(x, gmix, w_in[:, :cw], conv_w, seq, qk_w, 2 * qk_w, GDN_DK ** -0.5)
    z = norm_matmul(x, gmix, w_in[:, cw:cw + v_w], out_dtype=BF16)
    gb = gdn_gates(x, gmix, w_in[:, cw + v_w:], a_log, dt_bias)
    gcol = gb.reshape(bsz, seq, 2, 2, GDN_QK_HEADS, 2).transpose(0, 4, 1, 2, 3, 5).reshape(bsz, GDN_QK_HEADS, seq, 8)
    grow = jnp.swapaxes(gcol, 2, 3)
    kt = jnp.swapaxes(qkv[:, qk_w:2 * qk_w].reshape(bsz, seq, qk_w), 1, 2)
    o_f, o_b = gdn_scan(qkv, kt, gcol, grow, bsz, seq)
    return gdn_out(o_f, o_b, z, out_norm, w_o.astype(BF16), x)


def _na_bias_table(rpb, rows):
    win_r = min(NA_WIN_R, rows)
    cols = np.arange(GRID_W)
    col_start = np.clip(cols - NA_WIN_C // 2, 0, GRID_W - NA_WIN_C)
    col_valid = (cols[None, :] >= col_start[:, None]) & (cols[None, :] < col_start[:, None] + NA_WIN_C)
    dc_idx = np.clip(cols[None, :] - cols[:, None] + NA_WIN_C - 1, 0, 2 * NA_WIN_C - 2)
    bias_c = jnp.where(col_valid, rpb[:, :, dc_idx].astype(F32), NEG_INF)
    dr = np.arange(NA_WIN_R)[:, None] + np.arange(win_r)[None, :]
    tab = bias_c[:, dr]
    return jnp.transpose(tab, (0, 1, 3, 2, 4)).reshape(rpb.shape[0], NA_WIN_R, GRID_W, win_r * GRID_W)


def _na_kernel(q_ref, k_ref, v_ref, b_ref, o_ref, *, rows, win_r, scale):
    wk = win_r * GRID_W

    def body(r, carry):
        r0 = jnp.clip(r - win_r // 2, 0, rows - win_r)
        q = q_ref[pl.ds(pl.multiple_of(r * GRID_W, GRID_W), GRID_W), :]
        kw = k_ref[pl.ds(pl.multiple_of(r0 * GRID_W, GRID_W), wk), :]
        vw = v_ref[pl.ds(pl.multiple_of(r0 * GRID_W, GRID_W), wk), :]
        s = lax.dot_general(q, kw, (((1,), (1,)), ((), ())), preferred_element_type=F32) * scale
        s = s + b_ref[0, r0 - r + NA_WIN_R - 1]
        p = jnp.exp(s - jnp.max(s, axis=-1, keepdims=True))
        den = jnp.sum(p, axis=-1, keepdims=True)
        o = jnp.dot(p.astype(BF16), vw, preferred_element_type=F32) / den
        o_ref[pl.ds(pl.multiple_of(r * GRID_W, GRID_W), GRID_W), :] = o.astype(o_ref.dtype)
        return carry

    lax.fori_loop(0, rows, body, 0, unroll=4 if rows % 4 == 0 else 1)


def na_attention(qkv, rpb, bsz, seq):
    d = qkv.shape[1] // 3
    dh = d // NA_HEADS
    rows = seq // GRID_W
    win_r = min(NA_WIN_R, rows)
    table = _na_bias_table(rpb, rows)
    blk = lambda off: pl.BlockSpec((seq, dh), lambda b, h: (b, off + h))
    return pl.pallas_call(
        functools.partial(_na_kernel, rows=rows, win_r=win_r, scale=dh ** -0.5),
        out_shape=jax.ShapeDtypeStruct((bsz * seq, d), BF16),
        grid=(bsz, NA_HEADS),
        in_specs=[blk(0), blk(NA_HEADS), blk(2 * NA_HEADS),
                  pl.BlockSpec((1, NA_WIN_R, GRID_W, win_r * GRID_W), lambda b, h: (h, 0, 0, 0))],
        out_specs=blk(0),
        compiler_params=_params("parallel", "parallel"),
        name="na_attention",
    )(qkv, qkv, qkv, table)


def _sg_out_kernel(u_ref, v_ref, gn_ref, ws_ref, bs_ref, w_ref, r_ref, o_ref, a_ref, *, tm):
    @pl.when(pl.program_id(1) == 0)
    def _():
        vn = _rms(v_ref[...].astype(F32), gn_ref[...]).astype(BF16)
        bs = bs_ref[...]
        gd = vn.shape[1] // SG_GROUPS
        for c in range(tm // SG_CHUNK):
            rs = slice(c * SG_CHUNK, (c + 1) * SG_CHUNK)
            for g in range(SG_GROUPS):
                cs = slice(g * gd, (g + 1) * gd)
                mixed = jnp.dot(ws_ref[g], vn[rs, cs], preferred_element_type=F32) + bs[:, g:g + 1]
                a_ref[rs, cs] = (u_ref[rs, cs].astype(F32) * mixed).astype(BF16)

    o_ref[...] = r_ref[...] + jnp.dot(a_ref[...], w_ref[...], preferred_element_type=F32)


def sg_out(uv, sg_norm, w_s, b_s, w_o, res, tm=512, tn=512):
    m = uv.shape[0]
    width = uv.shape[1] // 2
    n = w_o.shape[1]
    return pl.pallas_call(
        functools.partial(_sg_out_kernel, tm=tm),
        out_shape=jax.ShapeDtypeStruct((m, n), F32),
        grid=(m // tm, n // tn),
        in_specs=[
            pl.BlockSpec((tm, width), lambda i, j: (i, 0)),
            pl.BlockSpec((tm, width), lambda i, j: (i, 1)),
            pl.BlockSpec((1, width), lambda i, j: (0, 0)),
            pl.BlockSpec(w_s.shape, lambda i, j: (0, 0, 0)),
            pl.BlockSpec((SG_CHUNK, SG_GROUPS), lambda i, j: (0, 0)),
            pl.BlockSpec((width, tn), lambda i, j: (0, j)),
            pl.BlockSpec((tm, tn), lambda i, j: (i, j)),
        ],
        out_specs=pl.BlockSpec((tm, tn), lambda i, j: (i, j)),
        scratch_shapes=[pltpu.VMEM((tm, width), BF16)],
        compiler_params=_params("parallel", "arbitrary"),
        name="sg_out",
    )(uv, uv, sg_norm.reshape(1, width), w_s.astype(BF16), b_s.T.astype(F32), w_o, res)


S5_L = 32
S5_W = S5_L * S5_GROUP_DIM


def _cexp(are, aim, dt, e):
    mag = jnp.exp(are * dt * e)
    ang = aim * dt * e
    return mag * jnp.cos(ang), mag * jnp.sin(ang)


def _s5_prep_kernel(arc_ref, aic_ref, arr_ref, air_ref, ldt_ref, btr_ref, bti_ref, ctr_ref, cti_ref,
                    k_ref, bm_ref, cm_ref, ap_ref, *, rev):
    hp = lax.Precision.HIGHEST
    ll, cg, w, p = S5_L, S5_GROUP_DIM, S5_W, S5_STATE
    sh = cg.bit_length() - 1
    dt = jnp.exp(ldt_ref[0])
    arc, aic = arc_ref[0], aic_ref[0]
    arr, air = arr_ref[0], air_ref[0]
    abr, abi = _cexp(arr, air, dt, 1.0)
    nr, ni = abr - 1.0, abi
    den = arr * arr + air * air
    cr, ci = (nr * arr + ni * air) / den, (ni * arr - nr * air) / den
    btr, bti = btr_ref[0], bti_ref[0]
    bbr, bbi = cr * btr - ci * bti, cr * bti + ci * btr
    lane = lax.broadcasted_iota(jnp.int32, (cg, w), 1)
    sel = jnp.where((lane & (cg - 1)) == lax.broadcasted_iota(jnp.int32, (cg, w), 0), 1.0, 0.0)
    cer = jnp.dot(ctr_ref[0], sel, precision=hp, preferred_element_type=F32)
    cei = jnp.dot(cti_ref[0], sel, precision=hp, preferred_element_type=F32)
    tl = (lax.broadcasted_iota(jnp.int32, (p, w), 1) >> sh).astype(F32)

    def cz(e):
        zr, zi = _cexp(arc, aic, dt, e)
        return cer * zr - cei * zi, cer * zi + cei * zr

    czr, czi = cz((ll - 1.0 - tl) if rev else tl)
    r = jnp.dot(bbr, czr, precision=hp, preferred_element_type=F32) - jnp.dot(bbi, czi, precision=hp,
                                                                                preferred_element_type=F32)
    lane_r = lax.broadcasted_iota(jnp.int32, (cg, w), 1)
    for s in range(ll):
        if rev:
            blk = jnp.where(lane_r < cg * (s + 1), pltpu.roll(r, (w - cg * (ll - 1 - s)) % w, 1), 0.0)
        else:
            blk = jnp.where(lane_r >= cg * s, pltpu.roll(r, cg * s, 1), 0.0)
        k_ref[0, s * cg:(s + 1) * cg, :] = blk.astype(k_ref.dtype)
    c1r, c1i = cz((ll - tl) if rev else (tl + 1.0))
    cm_ref[0, 0:p, :] = c1r.astype(cm_ref.dtype)
    cm_ref[0, p:2 * p, :] = (-c1i).astype(cm_ref.dtype)
    srow = (lax.broadcasted_iota(jnp.int32, (w, p), 0) >> sh).astype(F32)
    zr, zi = _cexp(arr, air, dt, srow if rev else (ll - 1.0 - srow))
    tbr, tbi = jnp.tile(bbr, (ll, 1)), jnp.tile(bbi, (ll, 1))
    bm_ref[0, :, 0:p] = (zr * tbr - zi * tbi).astype(bm_ref.dtype)
    bm_ref[0, :, p:2 * p] = (zr * tbi + zi * tbr).astype(bm_ref.dtype)
    ek = (ll << lax.broadcasted_iota(jnp.int32, (8, p), 0)).astype(F32)
    pr, pi = _cexp(arr, air, dt, ek)
    ap_ref[0, :, 0:p] = pr
    ap_ref[0, :, p:2 * p] = pi


def s5_prep(a_re, a_im, log_dt, b_re, b_im, c_re, c_im, rev):
    g, p = a_re.shape
    cg, w = S5_GROUP_DIM, S5_W
    col = lambda t: t.reshape(g, p, 1).astype(F32)
    row = lambda t: t.reshape(g, 1, p).astype(F32)
    tr = lambda t: jnp.swapaxes(t, 1, 2).astype(F32)
    spec = lambda s: pl.BlockSpec((1,) + s, lambda i: (i, 0, 0))
    return pl.pallas_call(
        functools.partial(_s5_prep_kernel, rev=rev),
        out_shape=(jax.ShapeDtypeStruct((g, w, w), BF16), jax.ShapeDtypeStruct((g, w, 2 * p), BF16),
                   jax.ShapeDtypeStruct((g, 2 * p, w), BF16), jax.ShapeDtypeStruct((g, 8, 2 * p), F32)),
        grid=(g,),
        in_specs=[spec((p, 1)), spec((p, 1)), spec((1, p)), spec((1, p)), spec((1, 1)),
                  spec((cg, p)), spec((cg, p)), spec((p, cg)), spec((p, cg))],
        out_specs=(spec((w, w)), spec((w, 2 * p)), spec((2 * p, w)), spec((8, 2 * p))),
        compiler_params=_params("parallel"),
        name="s5_prep",
    )(col(a_re), col(a_im), row(a_re), row(a_im), log_dt.reshape(g, 1, 1).astype(F32),
      tr(b_re), tr(b_im), tr(c_re), tr(c_im))


def _s5_chunk_scan(s, ap, nchunk, rev):
    n, w2 = s.shape
    p = w2 // 2
    m = lax.broadcasted_iota(jnp.int32, (n, w2), 0) & (nchunk - 1)
    lane = lax.broadcasted_iota(jnp.int32, (1, w2), 1)

    def shift(x, k):
        if rev:
            return jnp.where(m + k <= nchunk - 1, pltpu.roll(x, n - k, 0), 0.0)
        return jnp.where(m >= k, pltpu.roll(x, k, 0), 0.0)

    x = s
    k, lvl = 1, 0
    while k < nchunk:
        a = ap[lvl:lvl + 1, :]
        a1 = jnp.where(lane < p, a, pltpu.roll(a, p, 1))
        a2 = jnp.where(lane < p, -pltpu.roll(a, p, 1), a)
        xs = shift(x, k)
        x = x + a1 * xs + a2 * pltpu.roll(xs, p, 1)
        k, lvl = 2 * k, lvl + 1
    return shift(x, 1)


def _s5_main_kernel(u_ref, kf_ref, kb_ref, bf_ref, bb_ref, cf_ref, cb_ref, af_ref, ab_ref, y_ref, *, nchunk):
    u = u_ref[0]
    y = jnp.dot(u, kf_ref[0], preferred_element_type=F32) + jnp.dot(u, kb_ref[0], preferred_element_type=F32)
    for bm_ref, cm_ref, ap_ref, rev in ((bf_ref, cf_ref, af_ref, False), (bb_ref, cb_ref, ab_ref, True)):
        s = jnp.dot(u, bm_ref[0], preferred_element_type=F32)
        xin = _s5_chunk_scan(s, ap_ref[0], nchunk, rev)
        y = y + jnp.dot(xin.astype(BF16), cm_ref[0], preferred_element_type=F32)
    y_ref[0] = y


def s5_main(u, prep_f, prep_b, nchunk):
    g, n, w = u.shape
    p2 = 2 * S5_STATE
    spec = lambda s: pl.BlockSpec((1,) + s, lambda i: (i, 0, 0))
    kf, bf, cf, af = prep_f
    kb, bb, cb, ab = prep_b
    return pl.pallas_call(
        functools.partial(_s5_main_kernel, nchunk=nchunk),
        out_shape=jax.ShapeDtypeStruct((g, n, w), F32),
        grid=(g,),
        in_specs=[spec((n, w)), spec((w, w)), spec((w, w)), spec((w, p2)), spec((w, p2)),
                  spec((p2, w)), spec((p2, w)), spec((8, p2)), spec((8, p2))],
        out_specs=spec((n, w)),
        compiler_params=_params("parallel"),
        name="s5_main",
    )(u, kf, kb, bf, bb, cf, cb, af, ab)


def _s5_out_kernel(x_ref, g_ref, y_ref, d_ref, wa_ref, wb_ref, o_ref, a_ref):
    @pl.when(pl.program_id(1) == 0)
    def _():
        h = _rms(x_ref[...], g_ref[...])
        a_ref[...] = _gelu_tanh(y_ref[...] + d_ref[...] * h).astype(BF16)

    a = a_ref[...]
    ya = jnp.dot(a, wa_ref[...], preferred_element_type=F32)
    yb = jnp.dot(a, wb_ref[...], preferred_element_type=F32)
    tn = ya.shape[1]
    j = pl.program_id(1)
    res = x_ref[:, pl.ds(pl.multiple_of(j * tn, tn), tn)]
    o_ref[...] = res + ya * _sigmoid(yb)


def s5_out(x, g, y, d_skip, w_glu, tm=512, tn=512):
    m, d = x.shape
    n = w_glu.shape[1] // 2
    nb = n // tn
    return pl.pallas_call(
        _s5_out_kernel,
        out_shape=jax.ShapeDtypeStruct((m, n), F32),
        grid=(m // tm, nb),
        in_specs=[
            pl.BlockSpec((tm, d), lambda i, j: (i, 0)),
            pl.BlockSpec((1, d), lambda i, j: (0, 0)),
            pl.BlockSpec((tm, d), lambda i, j: (i, 0)),
            pl.BlockSpec((1, d), lambda i, j: (0, 0)),
            pl.BlockSpec((d, tn), lambda i, j: (0, j)),
            pl.BlockSpec((d, tn), lambda i, j: (0, j + nb)),
        ],
        out_specs=pl.BlockSpec((tm, tn), lambda i, j: (i, j)),
        scratch_shapes=[pltpu.VMEM((tm, d), BF16)],
        compiler_params=_params("parallel", "arbitrary"),
        name="s5_out",
    )(x, g.reshape(1, d), y, d_skip.reshape(1, d), w_glu, w_glu)


def s5_mixer(x, gmix, a_re, a_im, log_dt, b_re, b_im, c_re, c_im, d_skip, w_glu, bsz, seq):
    m, d = x.shape
    groups = d // S5_GROUP_DIM
    nchunk = seq // S5_L
    h = rmsnorm(x, gmix, out_dtype=BF16)
    u = h.reshape(m // S5_L, S5_L, groups, S5_GROUP_DIM).transpose(2, 0, 1, 3).reshape(groups, m // S5_L, S5_W)
    prep_f = s5_prep(a_re[0], a_im[0], log_dt[0], b_re[0], b_im[0], c_re[0], c_im[0], rev=False)
    prep_b = s5_prep(a_re[1], a_im[1], log_dt[1], b_re[1], b_im[1], c_re[1], c_im[1], rev=True)
    y = s5_main(u, prep_f, prep_b, nchunk)
    y = y.reshape(groups, m // S5_L, S5_L, S5_GROUP_DIM).transpose(1, 2, 0, 3).reshape(m, d)
    return s5_out(x, gmix, y, d_skip, w_glu.astype(BF16))


def _na_core(qkv, rpb, bsz, seq):
    d = qkv.shape[-1] // 3
    dh = d // NA_HEADS
    rows = seq // GRID_W
    win_r = min(NA_WIN_R, rows)
    q, k, v = jnp.split(qkv.astype(F32), 3, axis=-1)
    grid = lambda t: t.reshape(bsz, rows, GRID_W, NA_HEADS, dh)
    q, k, v = grid(q) * dh ** -0.5, grid(k), grid(v)
    row_start = np.clip(np.arange(rows) - win_r // 2, 0, rows - win_r)
    cols = np.arange(GRID_W)
    col_start = np.clip(cols - NA_WIN_C // 2, 0, GRID_W - NA_WIN_C)
    col_valid = (cols[None, :] >= col_start[:, None]) & (cols[None, :] < col_start[:, None] + NA_WIN_C)
    dc_idx = np.clip(cols[None, :] - cols[:, None] + NA_WIN_C - 1, 0, 2 * NA_WIN_C - 2)
    bias_c = jnp.where(col_valid, rpb[:, :, dc_idx].astype(F32), NEG_INF)
    bias_c = jnp.transpose(bias_c, (0, 2, 1, 3))
    dr_idx = row_start[:, None] + np.arange(win_r)[None, :] - np.arange(rows)[:, None] + NA_WIN_R - 1

    def one_row(args):
        q_r, r0, dri = args
        k_w = lax.dynamic_slice_in_dim(k, r0, win_r, axis=1)
        v_w = lax.dynamic_slice_in_dim(v, r0, win_r, axis=1)
        s = jnp.einsum('bqhd,bajhd->bhqaj', q_r, k_w, preferred_element_type=F32)
        s = s + jnp.take(bias_c, dri, axis=2)
        pr = jax.nn.softmax(s.reshape(bsz, NA_HEADS, GRID_W, win_r * GRID_W), axis=-1).reshape(s.shape)
        return jnp.einsum('bhqaj,bajhd->bqhd', pr, v_w)

    out = lax.map(one_row, (jnp.moveaxis(q, 1, 0), jnp.asarray(row_start, jnp.int32), jnp.asarray(dr_idx, jnp.int32)))
    return jnp.moveaxis(out, 0, 1).reshape(bsz * seq, d)


def _sg_core(uv, sg_norm, w_s, b_s, bsz, seq):
    width = uv.shape[-1] // 2
    n = seq // SG_CHUNK
    u, v = uv[:, :width], uv[:, width:]
    vf = v.astype(F32)
    vn = vf * lax.rsqrt(jnp.mean(vf * vf, axis=-1, keepdims=True) + EPS) * sg_norm
    vn = vn.reshape(bsz, n, SG_CHUNK, SG_GROUPS, width // SG_GROUPS)
    mixed = jnp.einsum('gts,bnsgc->bntgc', w_s, vn) + b_s.T[:, :, None]
    return u * mixed.reshape(bsz * seq, width)


def _l2norm(t):
    return t * lax.rsqrt(jnp.sum(t * t, axis=-1, keepdims=True) + EPS)


def _dwconv(x, w):
    width = w.shape[0]
    half = width // 2
    seq = x.shape[1]
    xp = jnp.pad(x, ((0, 0), (half, half), (0, 0)))
    return sum(xp[:, k:k + seq] * w[k] for k in range(width))


def _gdn_scan(q, k, v, g, beta):
    bsz, seq = q.shape[:2]
    n = seq // GDN_CHUNK
    rep = GDN_V_HEADS // GDN_QK_HEADS
    tri_incl = np.tril(np.ones((GDN_CHUNK, GDN_CHUNK), bool))
    tri_strict = np.tril(np.ones((GDN_CHUNK, GDN_CHUNK), bool), -1)
    eye = jnp.eye(GDN_CHUNK, dtype=F32)
    hp = lax.Precision.HIGHEST

    def chunks(t):
        return jnp.moveaxis(t.reshape(bsz, n, GDN_CHUNK, *t.shape[2:]), 1, 0)

    def tr(t):
        return jnp.swapaxes(t, -1, -2)

    def step(state, inp):
        qc, kc, vc, gc, bc = inp
        qh = jnp.swapaxes(jnp.repeat(qc, rep, axis=2), 1, 2)
        kh = jnp.swapaxes(jnp.repeat(kc, rep, axis=2), 1, 2)
        vh = jnp.swapaxes(vc, 1, 2)
        gam = jnp.cumsum(jnp.swapaxes(gc, 1, 2), axis=-1)
        bet = jnp.swapaxes(bc, 1, 2)
        decay = jnp.exp(jnp.where(tri_incl, gam[..., :, None] - gam[..., None, :], -jnp.inf))
        m = jnp.where(tri_strict, jnp.matmul(kh, tr(kh), precision=hp) * decay, 0.0) * bet[..., :, None]
        e_gam = jnp.exp(gam)
        rhs = jnp.concatenate([kh * (bet * e_gam)[..., None], vh * bet[..., None]], axis=-1)
        sol = lax.linalg.triangular_solve(eye + m, rhs, left_side=True, lower=True, unit_diagonal=True)
        w_mat, u_val = sol[..., :GDN_DK], sol[..., GDN_DK:]
        u = u_val - jnp.matmul(w_mat, state, precision=hp)
        o = jnp.matmul(qh * e_gam[..., None], state, precision=hp) + jnp.matmul(
            jnp.matmul(qh, tr(kh), precision=hp) * decay, u, precision=hp)
        k_dec = kh * jnp.exp(gam[..., -1:] - gam)[..., None]
        new_state = e_gam[..., -1][..., None, None] * state + jnp.matmul(tr(k_dec), u, precision=hp)
        return new_state, o

    state0 = jnp.zeros((bsz, GDN_V_HEADS, GDN_DK, GDN_DV), F32)
    _, o = lax.scan(step, state0, (chunks(q), chunks(k), chunks(v), chunks(g), chunks(beta)))
    return jnp.transpose(o, (1, 0, 3, 2, 4)).reshape(bsz, seq, GDN_V_HEADS, GDN_DV)


def _gdn_core(proj, conv_w, a_log, dt_bias, out_norm, bsz, seq):
    qk_w = GDN_QK_HEADS * GDN_DK
    v_w = GDN_V_HEADS * GDN_DV
    conv_width = 2 * qk_w + v_w
    proj = proj.reshape(bsz, seq, -1).astype(F32)
    qkv = jax.nn.silu(_dwconv(proj[..., :conv_width], conv_w))
    z = proj[..., conv_width:conv_width + v_w]
    ab = proj[..., conv_width + v_w:].reshape(bsz, seq, 2, 2, GDN_V_HEADS)
    q = _l2norm(qkv[..., :qk_w].reshape(bsz, seq, GDN_QK_HEADS, GDN_DK)) * GDN_DK ** -0.5
    k = _l2norm(qkv[..., qk_w:2 * qk_w].reshape(bsz, seq, GDN_QK_HEADS, GDN_DK))
    v = qkv[..., 2 * qk_w:].reshape(bsz, seq, GDN_V_HEADS, GDN_DV)
    decay_rate = jnp.exp(a_log.astype(F32))
    g = -decay_rate * jax.nn.softplus(ab[:, :, :, 0] + dt_bias.astype(F32))
    beta = jax.nn.sigmoid(ab[:, :, :, 1])
    o_fwd = _gdn_scan(q, k, v, g[:, :, 0], beta[:, :, 0])
    o_bwd = jnp.flip(_gdn_scan(jnp.flip(q, 1), jnp.flip(k, 1), jnp.flip(v, 1),
                               jnp.flip(g[:, :, 1], 1), jnp.flip(beta[:, :, 1], 1)), 1)
    zg = jax.nn.silu(z.reshape(bsz, seq, GDN_V_HEADS, GDN_DV))
    o = o_fwd + o_bwd
    o = o * lax.rsqrt(jnp.mean(o * o, axis=-1, keepdims=True) + EPS) * out_norm * zg
    return o.reshape(bsz * seq, v_w)


def _s5_direction(u, a_re, a_im, log_dt, b_re, b_im, c_re, c_im):
    bsz, seq, groups = u.shape[:3]
    n = seq // S5_CHUNK
    lam = lax.complex(a_re.astype(F32), a_im.astype(F32))
    dt = jnp.exp(log_dt.astype(F32))[:, None]
    a_bar = jnp.exp(lam * dt)
    b_bar = ((a_bar - 1.0) / lam)[..., None] * lax.complex(b_re.astype(F32), b_im.astype(F32))
    c = lax.complex(c_re.astype(F32), c_im.astype(F32))
    bu = jnp.einsum('gpc,bsgc->bsgp', b_bar, u.astype(jnp.complex64), precision=lax.Precision.HIGHEST)
    bu = jnp.moveaxis(bu.reshape(bsz, n, S5_CHUNK, groups, S5_STATE), 1, 0)
    a_elems = jnp.broadcast_to(a_bar, (bsz, S5_CHUNK, groups, S5_STATE))
    powers = jnp.exp(lam[None] * dt[None] * jnp.arange(1, S5_CHUNK + 1, dtype=F32)[:, None, None])

    def binop(e1, e2):
        return (e2[0] * e1[0], e2[0] * e1[1] + e2[1])

    def step(x_prev, bu_c):
        _, xs = lax.associative_scan(binop, (a_elems, bu_c), axis=1)
        xs = xs + powers[None] * x_prev[:, None]
        y = jnp.einsum('gcp,blgp->blgc', c, xs, precision=lax.Precision.HIGHEST).real
        return xs[:, -1], y

    x0 = jnp.zeros((bsz, groups, S5_STATE), jnp.complex64)
    _, ys = lax.scan(step, x0, bu)
    return jnp.moveaxis(ys, 0, 1).reshape(bsz, seq, groups, S5_GROUP_DIM)


def _s5_core(h, a_re, a_im, log_dt, b_re, b_im, c_re, c_im, d_skip, bsz, seq):
    d = h.shape[-1]
    groups = d // S5_GROUP_DIM
    u = h.reshape(bsz, seq, groups, S5_GROUP_DIM)
    y_f = _s5_direction(u, a_re[0], a_im[0], log_dt[0], b_re[0], b_im[0], c_re[0], c_im[0])
    y_b = jnp.flip(_s5_direction(jnp.flip(u, 1), a_re[1], a_im[1], log_dt[1], b_re[1], b_im[1], c_re[1], c_im[1]), 1)
    y = (y_f + y_b).reshape(bsz * seq, d) + d_skip * h
    return jax.nn.gelu(y)


def _trunk(x, p, w, bsz, seq):
    depth = w['norm_mix'].shape[0]
    bf = lambda t: t.astype(BF16)
    for i in range(depth):
        kind, j = i % N_MIXERS, i // N_MIXERS
        gmix = w['norm_mix'][i]
        if kind == 0:
            qkv = norm_matmul(x, gmix, bf(w['na_w_qkv'][j]), out_dtype=BF16)
            att = na_attention(qkv, w['na_rpb'][j], bsz, seq)
            x = matmul_res(att, bf(w['na_w_o'][j]), x)
        elif kind == 1:
            uv = norm_matmul(x, gmix, bf(w['sg_w_in'][j]), act="gelu", out_dtype=BF16)
            x = sg_out(uv, w['sg_norm'][j], w['sg_w_s'][j], w['sg_b_s'][j], bf(w['sg_w_o'][j]), x)
        elif kind == 2:
            x = gated_deltanet(x, gmix, w['gdn_w_in'][j], w['gdn_conv_w'][j], w['gdn_a_log'][j],
                               w['gdn_dt_bias'][j], w['gdn_out_norm'][j], w['gdn_w_o'][j], bsz, seq)
        else:
            x = s5_mixer(x, gmix, w['s5_a_re'][j], w['s5_a_im'][j], w['s5_log_dt'][j], w['s5_b_re'][j],
                         w['s5_b_im'][j], w['s5_c_re'][j], w['s5_c_im'][j], w['s5_d'][j], w['s5_w_glu'][j], bsz, seq)
        x = ffn(x, w['norm_ffn'][i], bf(w['ffn_w_gu'][i]), w['ffn_conv_w'][i], w['ffn_conv_b'][i],
                bf(w['ffn_w_down'][i]), seq)
        x = ple(x, p[i], w['norm_ple'][i], bf(w['ple_w_gate'][i]), bf(w['ple_w_proj'][i]))
    return x


def kernel(x_prompt, x_sample, p_prompt, p_sample, norm_mix, norm_ffn, norm_ple, final_norm, na_w_qkv, na_w_o, na_rpb, sg_w_in, sg_norm, sg_w_s, sg_b_s, sg_w_o, gdn_w_in, gdn_conv_w, gdn_a_log, gdn_dt_bias, gdn_out_norm, gdn_w_o, s5_a_re, s5_a_im, s5_log_dt, s5_b_re, s5_b_im, s5_c_re, s5_c_im, s5_d, s5_w_glu, ffn_w_gu, ffn_conv_w, ffn_conv_b, ffn_w_down, ple_w_proj, ple_w_gate):
    w = dict(norm_mix=norm_mix, norm_ffn=norm_ffn, norm_ple=norm_ple, final_norm=final_norm,
             na_w_qkv=na_w_qkv, na_w_o=na_w_o, na_rpb=na_rpb,
             sg_w_in=sg_w_in, sg_norm=sg_norm, sg_w_s=sg_w_s, sg_b_s=sg_b_s, sg_w_o=sg_w_o,
             gdn_w_in=gdn_w_in, gdn_conv_w=gdn_conv_w, gdn_a_log=gdn_a_log, gdn_dt_bias=gdn_dt_bias,
             gdn_out_norm=gdn_out_norm, gdn_w_o=gdn_w_o,
             s5_a_re=s5_a_re, s5_a_im=s5_a_im, s5_log_dt=s5_log_dt, s5_b_re=s5_b_re, s5_b_im=s5_b_im,
             s5_c_re=s5_c_re, s5_c_im=s5_c_im, s5_d=s5_d, s5_w_glu=s5_w_glu,
             ffn_w_gu=ffn_w_gu, ffn_conv_w=ffn_conv_w, ffn_conv_b=ffn_conv_b, ffn_w_down=ffn_w_down,
             ple_w_proj=ple_w_proj, ple_w_gate=ple_w_gate)
    b1, seq, d = x_prompt.shape
    b2 = x_sample.shape[0]
    bsz = b1 + b2
    x = jnp.concatenate([x_prompt, x_sample], axis=0).reshape(bsz * seq, d)
    p = jnp.concatenate([p_prompt, p_sample], axis=1).reshape(p_prompt.shape[0], bsz * seq, -1)
    x = _trunk(x, p, w, bsz, seq)
    y1 = rmsnorm(x, final_norm, row0=0, rows=b1 * seq).reshape(b1, seq, d)
    y2 = rmsnorm(x, final_norm, row0=b1 * seq, rows=b2 * seq).reshape(b2, seq, d)
    return (y1, y2)
```

```python
import functools
import math

import jax
import jax.numpy as jnp
import numpy as np
from jax import lax
from jax.experimental import pallas as pl
from jax.experimental.pallas import tpu as pltpu

F32 = jnp.float32
BF16 = jnp.bfloat16

EPS = 1e-6
NEG_INF = -1e30
GRID_W = 64
NA_HEADS = 16
NA_WIN_R = 8
NA_WIN_C = 16
SG_CHUNK = 128
SG_GROUPS = 16
GDN_QK_HEADS = 16
GDN_V_HEADS = 32
GDN_DK = 128
GDN_DV = 128
GDN_CHUNK = 64
S5_GROUP_DIM = 16
S5_STATE = 64
S5_CHUNK = 128
N_MIXERS = 4

VMEM_LIMIT_BYTES = 56 * 1024 * 1024
HALO = 16


def _params(*sem):
    return pltpu.CompilerParams(dimension_semantics=sem, vmem_limit_bytes=VMEM_LIMIT_BYTES)


def _rms(x, g):
    return x * lax.rsqrt(jnp.mean(x * x, axis=-1, keepdims=True) + EPS) * g


def _gelu_tanh(x):
    return 0.5 * x * (1.0 + jnp.tanh(math.sqrt(2.0 / math.pi) * (x + 0.044715 * (x * x * x))))


def _sigmoid(x):
    return 1.0 / (1.0 + jnp.exp(-x))


def _rmsnorm_kernel(x_ref, g_ref, o_ref):
    o_ref[...] = _rms(x_ref[...], g_ref[...]).astype(o_ref.dtype)


def rmsnorm(x, g, out_dtype=F32, tm=512, row0=0, rows=None):
    m, d = x.shape
    rows = m if rows is None else rows
    off = row0 // tm
    return pl.pallas_call(
        _rmsnorm_kernel,
        out_shape=jax.ShapeDtypeStruct((rows, d), out_dtype),
        grid=(rows // tm,),
        in_specs=[pl.BlockSpec((tm, d), lambda i: (i + off, 0)), pl.BlockSpec((1, d), lambda i: (0, 0))],
        out_specs=pl.BlockSpec((tm, d), lambda i: (i, 0)),
        compiler_params=_params("parallel"),
        name="rmsnorm",
    )(x, g.reshape(1, d))


def _norm_matmul_kernel(x_ref, g_ref, w_ref, o_ref, hn_ref, *, act):
    @pl.when(pl.program_id(1) == 0)
    def _():
        hn_ref[...] = _rms(x_ref[...], g_ref[...]).astype(BF16)

    y = jnp.dot(hn_ref[...], w_ref[...], preferred_element_type=F32)
    if act == "gelu":
        y = _gelu_tanh(y)
    o_ref[...] = y.astype(o_ref.dtype)


def norm_matmul(x, g, w, act=None, out_dtype=F32, tm=512, tn=1024):
    m, d = x.shape
    n = w.shape[1]
    tn = next(t for t in (tn, 512, 256, 128) if n % t == 0)
    return pl.pallas_call(
        functools.partial(_norm_matmul_kernel, act=act),
        out_shape=jax.ShapeDtypeStruct((m, n), out_dtype),
        grid=(m // tm, n // tn),
        in_specs=[
            pl.BlockSpec((tm, d), lambda i, j: (i, 0)),
            pl.BlockSpec((1, d), lambda i, j: (0, 0)),
            pl.BlockSpec((d, tn), lambda i, j: (0, j)),
        ],
        out_specs=pl.BlockSpec((tm, tn), lambda i, j: (i, j)),
        scratch_shapes=[pltpu.VMEM((tm, d), BF16)],
        compiler_params=_params("parallel", "arbitrary"),
        name="norm_matmul",
    )(x, g.reshape(1, d), w)


def _matmul_res_kernel(a_ref, w_ref, r_ref, o_ref):
    o_ref[...] = r_ref[...] + jnp.dot(a_ref[...].astype(BF16), w_ref[...], preferred_element_type=F32)


def matmul_res(a, w, res, tm=512, tn=512):
    m, k = a.shape
    n = w.shape[1]
    return pl.pallas_call(
        _matmul_res_kernel,
        out_shape=jax.ShapeDtypeStruct((m, n), F32),
        grid=(m // tm, n // tn),
        in_specs=[
            pl.BlockSpec((tm, k), lambda i, j: (i, 0)),
            pl.BlockSpec((k, tn), lambda i, j: (0, j)),
            pl.BlockSpec((tm, tn), lambda i, j: (i, j)),
        ],
        out_specs=pl.BlockSpec((tm, tn), lambda i, j: (i, j)),
        compiler_params=_params("parallel", "arbitrary"),
        name="matmul_res",
    )(a, w, res)


def _glu_res_kernel(a_ref, wa_ref, wb_ref, r_ref, o_ref):
    a = a_ref[...].astype(BF16)
    ya = jnp.dot(a, wa_ref[...], preferred_element_type=F32)
    yb = jnp.dot(a, wb_ref[...], preferred_element_type=F32)
    o_ref[...] = r_ref[...] + ya * _sigmoid(yb)


def glu_res(a, w, res, tm=512, tn=512):
    m, k = a.shape
    n = w.shape[1] // 2
    nb = n // tn
    return pl.pallas_call(
        _glu_res_kernel,
        out_shape=jax.ShapeDtypeStruct((m, n), F32),
        grid=(m // tm, nb),
        in_specs=[
            pl.BlockSpec((tm, k), lambda i, j: (i, 0)),
            pl.BlockSpec((k, tn), lambda i, j: (0, j)),
            pl.BlockSpec((k, tn), lambda i, j: (0, j + nb)),
            pl.BlockSpec((tm, tn), lambda i, j: (i, j)),
        ],
        out_specs=pl.BlockSpec((tm, tn), lambda i, j: (i, j)),
        compiler_params=_params("parallel", "arbitrary"),
        name="glu_res",
    )(a, w, w, res)


def _ffn_kernel(x_ref, xp_ref, xn_ref, g_ref, wg_ref, wu_ref, cw_ref, cb_ref, wd_ref, o_ref, hn_ref, acc_ref,
                *, tm, seq):
    i = pl.program_id(0)
    j = pl.program_id(1)

    @pl.when(j == 0)
    def _():
        g = g_ref[...]
        prev_ok = jnp.where((i * tm) % seq != 0, 1.0, 0.0)
        next_ok = jnp.where(((i + 1) * tm) % seq != 0, 1.0, 0.0)
        hn_ref[0:HALO, :] = (_rms(xp_ref[...], g) * prev_ok).astype(BF16)
        hn_ref[HALO:HALO + tm, :] = _rms(x_ref[...], g).astype(BF16)
        hn_ref[HALO + tm:, :] = (_rms(xn_ref[...], g) * next_ok).astype(BF16)
        acc_ref[...] = jnp.zeros_like(acc_ref)

    rows = tm + 2 * HALO
    gate = jnp.dot(hn_ref[...], wg_ref[...], preferred_element_type=F32)
    up = jnp.dot(hn_ref[HALO:HALO + tm, :], wu_ref[...], preferred_element_type=F32)
    cw = cw_ref[...]
    g_prev = pltpu.roll(gate, 1, 0)[HALO:HALO + tm]
    g_next = pltpu.roll(gate, rows - 1, 0)[HALO:HALO + tm]
    gc = cw[0:1] * g_prev + cw[1:2] * gate[HALO:HALO + tm] + cw[2:3] * g_next + cb_ref[...]
    act = (gc * _sigmoid(gc) * up).astype(BF16)
    acc_ref[...] += jnp.dot(act, wd_ref[...], preferred_element_type=F32)

    @pl.when(j == pl.num_programs(1) - 1)
    def _():
        o_ref[...] = x_ref[...] + acc_ref[...]


def ffn(x, g, w_gu, conv_w, conv_b, w_down, seq, tm=512, tf=512):
    m, d = x.shape
    f = w_down.shape[0]
    nf = f // tf
    hb = tm // HALO
    last = m // HALO - 1
    return pl.pallas_call(
        functools.partial(_ffn_kernel, tm=tm, seq=seq),
        out_shape=jax.ShapeDtypeStruct((m, d), F32),
        grid=(m // tm, nf),
        in_specs=[
            pl.BlockSpec((tm, d), lambda i, j: (i, 0)),
            pl.BlockSpec((HALO, d), lambda i, j: (jnp.maximum(i * hb - 1, 0), 0)),
            pl.BlockSpec((HALO, d), lambda i, j: (jnp.minimum((i + 1) * hb, last), 0)),
            pl.BlockSpec((1, d), lambda i, j: (0, 0)),
            pl.BlockSpec((d, tf), lambda i, j: (0, j)),
            pl.BlockSpec((d, tf), lambda i, j: (0, j + nf)),
            pl.BlockSpec((3, tf), lambda i, j: (0, j)),
            pl.BlockSpec((1, tf), lambda i, j: (0, j)),
            pl.BlockSpec((tf, d), lambda i, j: (j, 0)),
        ],
        out_specs=pl.BlockSpec((tm, d), lambda i, j: (i, 0)),
        scratch_shapes=[pltpu.VMEM((tm + 2 * HALO, d), BF16), pltpu.VMEM((tm, d), F32)],
        compiler_params=_params("parallel", "arbitrary"),
        name="ffn",
    )(x, x, x, g.reshape(1, d), w_gu, w_gu, conv_w, conv_b.reshape(1, f), w_down)


def _ple_kernel(x_ref, p_ref, g_ref, wg_ref, wp_ref, o_ref):
    x = x_ref[...]
    hn = _rms(x, g_ref[...]).astype(BF16)
    gate = _sigmoid(jnp.dot(hn, wg_ref[...], preferred_element_type=F32))
    proj = jnp.dot(p_ref[...].astype(BF16), wp_ref[...], preferred_element_type=F32)
    o_ref[...] = x + gate * proj


def ple(x, p, g, w_gate, w_proj, tm=512):
    m, d = x.shape
    pd = p.shape[1]
    return pl.pallas_call(
        _ple_kernel,
        out_shape=jax.ShapeDtypeStruct((m, d), F32),
        grid=(m // tm,),
        in_specs=[
            pl.BlockSpec((tm, d), lambda i: (i, 0)),
            pl.BlockSpec((tm, pd), lambda i: (i, 0)),
            pl.BlockSpec((1, d), lambda i: (0, 0)),
            pl.BlockSpec((d, d), lambda i: (0, 0)),
            pl.BlockSpec((pd, d), lambda i: (0, 0)),
        ],
        out_specs=pl.BlockSpec((tm, d), lambda i: (i, 0)),
        compiler_params=_params("parallel"),
        name="ple",
    )(x, p, g.reshape(1, d), w_gate, w_proj)


GDN_BLOCK = 256


def _conv_in_kernel(x_ref, xp_ref, xn_ref, g_ref, w_ref, cw_ref, o_ref, hn_ref, *, tm, seq, nq, nqk, scale):
    i = pl.program_id(0)
    j = pl.program_id(1)

    @pl.when(j == 0)
    def _():
        g = g_ref[...]
        prev_ok = jnp.where((i * tm) % seq != 0, 1.0, 0.0)
        next_ok = jnp.where(((i + 1) * tm) % seq != 0, 1.0, 0.0)
        hn_ref[0:HALO, :] = (_rms(xp_ref[...], g) * prev_ok).astype(BF16)
        hn_ref[HALO:HALO + tm, :] = _rms(x_ref[...], g).astype(BF16)
        hn_ref[HALO + tm:, :] = (_rms(xn_ref[...], g) * next_ok).astype(BF16)

    rows = tm + 2 * HALO
    y = jnp.dot(hn_ref[...], w_ref[...], preferred_element_type=F32)
    cw = cw_ref[...]
    y_prev = pltpu.roll(y, 1, 0)[HALO:HALO + tm]
    y_next = pltpu.roll(y, rows - 1, 0)[HALO:HALO + tm]
    c = cw[0:1] * y_prev + cw[1:2] * y[HALO:HALO + tm] + cw[2:3] * y_next
    c = c * _sigmoid(c)
    tn = c.shape[1]
    for s in range(tn // 128):
        cs = c[:, s * 128:(s + 1) * 128]
        inv = lax.rsqrt(jnp.sum(cs * cs, axis=-1, keepdims=True) + EPS)
        f = jnp.where(j < nq, inv * scale, jnp.where(j < nqk, inv, 1.0))
        o_ref[:, s * 128:(s + 1) * 128] = (cs * f).astype(o_ref.dtype)


def conv_in(x, g, w, conv_w, seq, n_q, n_qk, scale, tm=512, tn=512):
    m, d = x.shape
    n = w.shape[1]
    hb = tm // HALO
    last = m // HALO - 1
    return pl.pallas_call(
        functools.partial(_conv_in_kernel, tm=tm, seq=seq, nq=n_q // tn, nqk=n_qk // tn, scale=scale),
        out_shape=jax.ShapeDtypeStruct((m, n), BF16),
        grid=(m // tm, n // tn),
        in_specs=[
            pl.BlockSpec((tm, d), lambda i, j: (i, 0)),
            pl.BlockSpec((HALO, d), lambda i, j: (jnp.maximum(i * hb - 1, 0), 0)),
            pl.BlockSpec((HALO, d), lambda i, j: (jnp.minimum((i + 1) * hb, last), 0)),
            pl.BlockSpec((1, d), lambda i, j: (0, 0)),
            pl.BlockSpec((d, tn), lambda i, j: (0, j)),
            pl.BlockSpec((3, tn), lambda i, j: (0, j)),
        ],
        out_specs=pl.BlockSpec((tm, tn), lambda i, j: (i, j)),
        scratch_shapes=[pltpu.VMEM((tm + 2 * HALO, d), BF16)],
        compiler_params=_params("parallel", "arbitrary"),
        name="gdn_conv_in",
    )(x, x, x, g.reshape(1, d), w, conv_w)


def _gdn_gates_kernel(x_ref, g_ref, w_ref, alog_ref, bias_ref, isg_ref, o_ref):
    hn = _rms(x_ref[...], g_ref[...]).astype(BF16)
    y = jnp.dot(hn, w_ref[...], preferred_element_type=F32)
    t = y + bias_ref[...]
    softplus = jnp.maximum(t, 0.0) + jnp.log1p(jnp.exp(-jnp.abs(t)))
    isg = isg_ref[...] > 0.5
    base = jnp.where(isg, -jnp.exp(alog_ref[...]) * softplus, 0.0)
    tm, n = base.shape
    lc = GDN_CHUNK
    pos = lax.broadcasted_iota(jnp.int32, (tm, n), 0) & (lc - 1)
    pre, suf = base, base
    k = 1
    while k < lc:
        pre = pre + jnp.where(pos >= k, pltpu.roll(pre, k, 0), 0.0)
        suf = suf + jnp.where(pos + k <= lc - 1, pltpu.roll(suf, tm - k, 0), 0.0)
        k *= 2
    is_bwd = lax.broadcasted_iota(jnp.int32, (1, n), 1) >= n // 2
    o_ref[:, 0:n] = jnp.where(isg, jnp.where(is_bwd, suf, pre), _sigmoid(y))
    o_ref[:, n:2 * n] = pre + suf - base


def gdn_gates(x, g, w_ab, a_log, dt_bias, tm=512):
    m, d = x.shape
    n = w_ab.shape[1]
    nh = a_log.shape[-1]
    zeros = jnp.zeros((2, 1, nh), F32)
    arrange = lambda t: jnp.concatenate([t.reshape(2, 1, nh).astype(F32), zeros], axis=1).reshape(1, n)
    isg = jnp.concatenate([jnp.ones((2, 1, nh), F32), zeros], axis=1).reshape(1, n)
    return pl.pallas_call(
        _gdn_gates_kernel,
        out_shape=jax.ShapeDtypeStruct((m, 2 * n), F32),
        grid=(m // tm,),
        in_specs=[
            pl.BlockSpec((tm, d), lambda i: (i, 0)),
            pl.BlockSpec((1, d), lambda i: (0, 0)),
            pl.BlockSpec((d, n), lambda i: (0, 0)),
            pl.BlockSpec((1, n), lambda i: (0, 0)),
            pl.BlockSpec((1, n), lambda i: (0, 0)),
            pl.BlockSpec((1, n), lambda i: (0, 0)),
        ],
        out_specs=pl.BlockSpec((tm, 2 * n), lambda i: (i, 0)),
        compiler_params=_params("parallel"),
        name="gdn_gates",
    )(x, g.reshape(1, d), w_ab, arrange(a_log), arrange(dt_bias), isg)


def _gdn_chains(q_ref, k_ref, kt_ref, v_ref, gc_ref, gr_ref, o_ref, rev):
    c_sz, lc = GDN_BLOCK, GDN_CHUNK
    d = 1 if rev else 0
    ii = lax.broadcasted_iota(jnp.int32, (c_sz, c_sz), 0)
    jj = lax.broadcasted_iota(jnp.int32, (c_sz, c_sz), 1)
    sh = lc.bit_length() - 1
    same = (ii >> sh) == (jj >> sh)
    incl = jnp.logical_and(same, (jj >= ii) if rev else (jj <= ii))
    eye = ii == jj
    q = q_ref[...]
    k = k_ref[...]
    kf = k.astype(F32)
    qf = q.astype(F32)
    ktf = kt_ref[0].astype(F32)
    gc = gc_ref[0, 0][:, 8 * d:8 * d + 8]
    gr = gr_ref[0, 0][8 * d:8 * d + 8, :]
    gam_c = gc[:, 0:2]
    gam_r = gr[0:2]
    end_r = gr[4:6]
    gram = lax.dot_general(k, k, (((1,), (1,)), ((), ())), preferred_element_type=F32)
    qk = lax.dot_general(q, k, (((1,), (1,)), ((), ())), preferred_element_type=F32)
    chains = []
    for hs in range(2):
        gcol = gam_c[:, hs:hs + 1]
        grow = gam_r[hs:hs + 1, :]
        bcol = gc[:, 2 + hs:3 + hs]
        dec = jnp.exp(jnp.where(incl, gcol - grow, NEG_INF))
        e_g = jnp.exp(gcol)
        vh = v_ref[:, hs * GDN_DV:(hs + 1) * GDN_DV].astype(F32)
        chains.append(dict(
            n=(-(gram * jnp.where(eye, 0.0, dec)) * bcol).astype(BF16),
            x=jnp.concatenate([kf * (bcol * e_g), vh * bcol], axis=1),
            attn=(qk * dec).astype(BF16),
            qe=qf * e_g,
            kdt=ktf * jnp.exp(end_r[hs:hs + 1, :] - grow),
            end=end_r[hs:hs + 1, :],
            o_ref=o_ref, hs=hs, idx=2 * d + hs, rev=rev))
    return chains


def _gdn_scan_kernel(qf_ref, kf_ref, ktf_ref, vf_ref, gcf_ref, grf_ref,
                     qb_ref, kb_ref, ktb_ref, vb_ref, gcb_ref, grb_ref, of_ref, ob_ref, s_ref):
    @pl.when(pl.program_id(2) == 0)
    def _():
        s_ref[...] = jnp.zeros_like(s_ref)

    c_sz, lc = GDN_BLOCK, GDN_CHUNK
    nchunk = c_sz // lc
    width = GDN_DK + GDN_DV
    chains = (_gdn_chains(qf_ref, kf_ref, ktf_ref, vf_ref, gcf_ref, grf_ref, of_ref, False)
              + _gdn_chains(qb_ref, kb_ref, ktb_ref, vb_ref, gcb_ref, grb_ref, ob_ref, True))
    nlev = lc.bit_length() - 1
    for lvl in range(nlev):
        for ch in chains:
            nb = ch['n']
            xb = ch['x'].astype(BF16)
            if lvl < nlev - 1:
                r = jnp.dot(nb, jnp.concatenate([xb, nb], axis=1), preferred_element_type=F32)
                ch['x'] = ch['x'] + r[:, :width]
                ch['n'] = r[:, width:].astype(BF16)
            else:
                ch['x'] = ch['x'] + jnp.dot(nb, xb, preferred_element_type=F32)
    jrow = lax.broadcasted_iota(jnp.int32, (1, c_sz), 1) >> nlev
    for ch in chains:
        wub = ch['x'].astype(BF16)
        awu = jnp.dot(ch['attn'], wub, preferred_element_type=F32)
        ch['qeff'] = (ch['qe'] - awu[:, :GDN_DK]).astype(BF16)
        ch['o_in'] = awu[:, GDN_DK:]
        ch['kwu'] = [jnp.dot(jnp.where(jrow == c, ch['kdt'], 0.0).astype(BF16), wub, preferred_element_type=F32)
                     for c in range(nchunk)]
        ch['s'] = s_ref[ch['idx']]
    for step in range(nchunk):
        for ch in chains:
            c = nchunk - 1 - step if ch['rev'] else step
            r0 = c * lc
            s = ch['s']
            sb = s.astype(BF16)
            o_c = ch['o_in'][r0:r0 + lc] + jnp.dot(ch['qeff'][r0:r0 + lc], sb, preferred_element_type=F32)
            ch['o_ref'][r0:r0 + lc, ch['hs'] * GDN_DV:(ch['hs'] + 1) * GDN_DV] = o_c.astype(ch['o_ref'].dtype)
            kwu = ch['kwu'][c]
            e_end = jnp.exp(ch['end'][:, r0:r0 + 1])
            ch['s'] = (e_end * s - jnp.dot(kwu[:, :GDN_DK].astype(BF16), sb, preferred_element_type=F32)
                       + kwu[:, GDN_DK:])
    for ch in chains:
        s_ref[ch['idx']] = ch['s']


def gdn_scan(qkv, kt, gcol, grow, bsz, seq):
    c_sz = GDN_BLOCK
    nb = seq // c_sz
    hq = GDN_QK_HEADS
    kcol = hq
    vcol = (2 * hq * GDN_DK) // (2 * GDN_DV)
    fwd = lambda b, h, c: c
    bwd = lambda b, h, c: nb - 1 - c

    def specs(pos):
        return [
            pl.BlockSpec((c_sz, GDN_DK), lambda b, h, c: (b * nb + pos(b, h, c), h)),
            pl.BlockSpec((c_sz, GDN_DK), lambda b, h, c: (b * nb + pos(b, h, c), kcol + h)),
            pl.BlockSpec((1, GDN_DK, c_sz), lambda b, h, c: (b, h, pos(b, h, c))),
            pl.BlockSpec((c_sz, 2 * GDN_DV), lambda b, h, c: (b * nb + pos(b, h, c), vcol + h)),
            pl.BlockSpec((1, 1, c_sz, 16), lambda b, h, c: (b, h, pos(b, h, c), 0)),
            pl.BlockSpec((1, 1, 16, c_sz), lambda b, h, c: (b, h, 0, pos(b, h, c))),
        ]

    out = jax.ShapeDtypeStruct((bsz * seq, GDN_V_HEADS * GDN_DV), BF16)
    return pl.pallas_call(
        _gdn_scan_kernel,
        out_shape=(out, out),
        grid=(bsz, hq, nb),
        in_specs=specs(fwd) + specs(bwd),
        out_specs=(
            pl.BlockSpec((c_sz, 2 * GDN_DV), lambda b, h, c: (b * nb + c, h)),
            pl.BlockSpec((c_sz, 2 * GDN_DV), lambda b, h, c: (b * nb + nb - 1 - c, h)),
        ),
        scratch_shapes=[pltpu.VMEM((4, GDN_DK, GDN_DV), F32)],
        compiler_params=_params("parallel", "parallel", "arbitrary"),
        name="gdn_scan",
    )(qkv, qkv, kt, qkv, gcol, grow, qkv, qkv, kt, qkv, gcol, grow)


def _gdn_out_kernel(of_ref, ob_ref, z_ref, gn_ref, w_ref, r_ref, o_ref):
    @pl.when(pl.program_id(1) == 0)
    def _():
        o_ref[...] = r_ref[...]

    gn = gn_ref[...]
    parts = []
    for h in range(of_ref.shape[1] // GDN_DV):
        sl = slice(h * GDN_DV, (h + 1) * GDN_DV)
        o = of_ref[:, sl].astype(F32) + ob_ref[:, sl].astype(F32)
        z = z_ref[:, sl].astype(F32)
        parts.append((_rms(o, gn) * (z * _sigmoid(z))).astype(BF16))
    o_ref[...] += jnp.dot(jnp.concatenate(parts, axis=1), w_ref[...], preferred_element_type=F32)


def gdn_out(o_f, o_b, z, out_norm, w_o, res, tm=512, tk=1024):
    m, k = o_f.shape
    n = w_o.shape[1]
    return pl.pallas_call(
        _gdn_out_kernel,
        out_shape=jax.ShapeDtypeStruct((m, n), F32),
        grid=(m // tm, k // tk),
        in_specs=[
            pl.BlockSpec((tm, tk), lambda i, j: (i, j)),
            pl.BlockSpec((tm, tk), lambda i, j: (i, j)),
            pl.BlockSpec((tm, tk), lambda i, j: (i, j)),
            pl.BlockSpec((1, GDN_DV), lambda i, j: (0, 0)),
            pl.BlockSpec((tk, n), lambda i, j: (j, 0)),
            pl.BlockSpec((tm, n), lambda i, j: (i, 0)),
        ],
        out_specs=pl.BlockSpec((tm, n), lambda i, j: (i, 0)),
        compiler_params=_params("parallel", "arbitrary"),
        name="gdn_out",
    )(o_f, o_b, z, out_norm.reshape(1, GDN_DV), w_o, res)


def gated_deltanet(x, gmix, w_in, conv_w, a_log, dt_bias, out_norm, w_o, bsz, seq):
    qk_w = GDN_QK_HEADS * GDN_DK
    v_w = GDN_V_HEADS * GDN_DV
    cw = 2 * qk_w + v_w
    w_in = w_in.astype(BF16)
    qkv = conv_in(x, gmix, w_in[:, :cw], conv_w, seq, qk_w, 2 * qk_w, GDN_DK ** -0.5)
    z = norm_matmul(x, gmix, w_in[:, cw:cw + v_w], out_dtype=BF16)
    gb = gdn_gates(x, gmix, w_in[:, cw + v_w:], a_log, dt_bias)
    nab = gb.shape[1] // 2
    ab = gb[:, :nab].reshape(bsz, seq, 2, 2, GDN_QK_HEADS, 2).transpose(0, 4, 1, 2, 3, 5)
    tot = gb[:, nab:].reshape(bsz, seq, 2, 2, GDN_QK_HEADS, 2)[:, :, :, 0].transpose(0, 3, 1, 2, 4)[..., None, :]
    gcol = jnp.concatenate([ab, tot, jnp.zeros_like(tot)], axis=4).reshape(bsz, GDN_QK_HEADS, seq, 16)
    grow = jnp.swapaxes(gcol, 2, 3)
    kt = jnp.swapaxes(qkv[:, qk_w:2 * qk_w].reshape(bsz, seq, qk_w), 1, 2)
    o_f, o_b = gdn_scan(qkv, kt, gcol, grow, bsz, seq)
    return gdn_out(o_f, o_b, z, out_norm, w_o.astype(BF16), x)


def _na_bias_table(rpb, rows):
    win_r = min(NA_WIN_R, rows)
    cols = np.arange(GRID_W)
    col_start = np.clip(cols - NA_WIN_C // 2, 0, GRID_W - NA_WIN_C)
    col_valid = (cols[None, :] >= col_start[:, None]) & (cols[None, :] < col_start[:, None] + NA_WIN_C)
    dc_idx = np.clip(cols[None, :] - cols[:, None] + NA_WIN_C - 1, 0, 2 * NA_WIN_C - 2)
    bias_c = jnp.where(col_valid, rpb[:, :, dc_idx].astype(F32), NEG_INF)
    dr = np.arange(NA_WIN_R)[:, None] + np.arange(win_r)[None, :]
    tab = bias_c[:, dr]
    return jnp.transpose(tab, (0, 1, 3, 2, 4)).reshape(rpb.shape[0], NA_WIN_R, GRID_W, win_r * GRID_W)


def _na_kernel(q_ref, k_ref, v_ref, b_ref, o_ref, *, rows, win_r, scale):
    wk = win_r * GRID_W

    group = 8 if rows % 8 == 0 else 1

    def body(it, carry):
        rs = [it * group + i for i in range(group)]
        r0s = [jnp.clip(r - win_r // 2, 0, rows - win_r) for r in rs]
        scores = []
        for r, r0 in zip(rs, r0s):
            q = q_ref[pl.ds(pl.multiple_of(r * GRID_W, GRID_W), GRID_W), :]
            kw = k_ref[pl.ds(pl.multiple_of(r0 * GRID_W, GRID_W), wk), :]
            s = lax.dot_general(q, kw, (((1,), (1,)), ((), ())), preferred_element_type=F32) * scale
            scores.append(s + b_ref[0, r0 - r + NA_WIN_R - 1])
        probs, dens = [], []
        for s in scores:
            p = jnp.exp(s - jnp.max(s, axis=-1, keepdims=True))
            dens.append(jnp.sum(p, axis=-1, keepdims=True))
            probs.append(p.astype(BF16))
        for r, r0, p, den in zip(rs, r0s, probs, dens):
            vw = v_ref[pl.ds(pl.multiple_of(r0 * GRID_W, GRID_W), wk), :]
            o = jnp.dot(p, vw, preferred_element_type=F32) / den
            o_ref[pl.ds(pl.multiple_of(r * GRID_W, GRID_W), GRID_W), :] = o.astype(o_ref.dtype)
        return carry

    lax.fori_loop(0, rows // group, body, 0)


def na_attention(qkv, rpb, bsz, seq):
    d = qkv.shape[1] // 3
    dh = d // NA_HEADS
    rows = seq // GRID_W
    win_r = min(NA_WIN_R, rows)
    table = _na_bias_table(rpb, rows)
    blk = lambda off: pl.BlockSpec((seq, dh), lambda b, h: (b, off + h))
    return pl.pallas_call(
        functools.partial(_na_kernel, rows=rows, win_r=win_r, scale=dh ** -0.5),
        out_shape=jax.ShapeDtypeStruct((bsz * seq, d), BF16),
        grid=(bsz, NA_HEADS),
        in_specs=[blk(0), blk(NA_HEADS), blk(2 * NA_HEADS),
                  pl.BlockSpec((1, NA_WIN_R, GRID_W, win_r * GRID_W), lambda b, h: (h, 0, 0, 0))],
        out_specs=blk(0),
        compiler_params=_params("parallel", "parallel"),
        name="na_attention",
    )(qkv, qkv, qkv, table)


def _sg_out_kernel(u_ref, v_ref, gn_ref, ws_ref, bs_ref, w_ref, r_ref, o_ref, a_ref, *, tm):
    @pl.when(pl.program_id(1) == 0)
    def _():
        vn = _rms(v_ref[...].astype(F32), gn_ref[...]).astype(BF16)
        bs = bs_ref[...]
        gd = vn.shape[1] // SG_GROUPS
        for c in range(tm // SG_CHUNK):
            rs = slice(c * SG_CHUNK, (c + 1) * SG_CHUNK)
            for g in range(SG_GROUPS):
                cs = slice(g * gd, (g + 1) * gd)
                mixed = jnp.dot(ws_ref[g], vn[rs, cs], preferred_element_type=F32) + bs[:, g:g + 1]
                a_ref[rs, cs] = (u_ref[rs, cs].astype(F32) * mixed).astype(BF16)

    o_ref[...] = r_ref[...] + jnp.dot(a_ref[...], w_ref[...], preferred_element_type=F32)


def sg_out(uv, sg_norm, w_s, b_s, w_o, res, tm=512, tn=512):
    m = uv.shape[0]
    width = uv.shape[1] // 2
    n = w_o.shape[1]
    return pl.pallas_call(
        functools.partial(_sg_out_kernel, tm=tm),
        out_shape=jax.ShapeDtypeStruct((m, n), F32),
        grid=(m // tm, n // tn),
        in_specs=[
            pl.BlockSpec((tm, width), lambda i, j: (i, 0)),
            pl.BlockSpec((tm, width), lambda i, j: (i, 1)),
            pl.BlockSpec((1, width), lambda i, j: (0, 0)),
            pl.BlockSpec(w_s.shape, lambda i, j: (0, 0, 0)),
            pl.BlockSpec((SG_CHUNK, SG_GROUPS), lambda i, j: (0, 0)),
            pl.BlockSpec((width, tn), lambda i, j: (0, j)),
            pl.BlockSpec((tm, tn), lambda i, j: (i, j)),
        ],
        out_specs=pl.BlockSpec((tm, tn), lambda i, j: (i, j)),
        scratch_shapes=[pltpu.VMEM((tm, width), BF16)],
        compiler_params=_params("parallel", "arbitrary"),
        name="sg_out",
    )(uv, uv, sg_norm.reshape(1, width), w_s.astype(BF16), b_s.T.astype(F32), w_o, res)


S5_L = 32
S5_W = S5_L * S5_GROUP_DIM


def _cexp(are, aim, dt, e):
    mag = jnp.exp(are * dt * e)
    ang = aim * dt * e
    return mag * jnp.cos(ang), mag * jnp.sin(ang)


def _s5_prep_kernel(arc_ref, aic_ref, arr_ref, air_ref, ldt_ref, btr_ref, bti_ref, ctr_ref, cti_ref,
                    k_ref, bm_ref, cm_ref, ap_ref, *, rev):
    hp = lax.Precision.HIGHEST
    ll, cg, w, p = S5_L, S5_GROUP_DIM, S5_W, S5_STATE
    sh = cg.bit_length() - 1
    dt = jnp.exp(ldt_ref[0])
    arc, aic = arc_ref[0], aic_ref[0]
    arr, air = arr_ref[0], air_ref[0]
    abr, abi = _cexp(arr, air, dt, 1.0)
    nr, ni = abr - 1.0, abi
    den = arr * arr + air * air
    cr, ci = (nr * arr + ni * air) / den, (ni * arr - nr * air) / den
    btr, bti = btr_ref[0], bti_ref[0]
    bbr, bbi = cr * btr - ci * bti, cr * bti + ci * btr
    lane = lax.broadcasted_iota(jnp.int32, (cg, w), 1)
    sel = jnp.where((lane & (cg - 1)) == lax.broadcasted_iota(jnp.int32, (cg, w), 0), 1.0, 0.0)
    cer = jnp.dot(ctr_ref[0], sel, precision=hp, preferred_element_type=F32)
    cei = jnp.dot(cti_ref[0], sel, precision=hp, preferred_element_type=F32)
    tl = (lax.broadcasted_iota(jnp.int32, (p, w), 1) >> sh).astype(F32)

    def cz(e):
        zr, zi = _cexp(arc, aic, dt, e)
        return cer * zr - cei * zi, cer * zi + cei * zr

    czr, czi = cz((ll - 1.0 - tl) if rev else tl)
    r = jnp.dot(bbr, czr, precision=hp, preferred_element_type=F32) - jnp.dot(bbi, czi, precision=hp,
                                                                                preferred_element_type=F32)
    lane_r = lax.broadcasted_iota(jnp.int32, (cg, w), 1)
    for s in range(ll):
        if rev:
            blk = jnp.where(lane_r < cg * (s + 1), pltpu.roll(r, (w - cg * (ll - 1 - s)) % w, 1), 0.0)
        else:
            blk = jnp.where(lane_r >= cg * s, pltpu.roll(r, cg * s, 1), 0.0)
        k_ref[0, s * cg:(s + 1) * cg, :] = blk.astype(k_ref.dtype)
    c1r, c1i = cz((ll - tl) if rev else (tl + 1.0))
    cm_ref[0, 0:p, :] = c1r.astype(cm_ref.dtype)
    cm_ref[0, p:2 * p, :] = (-c1i).astype(cm_ref.dtype)
    srow = (lax.broadcasted_iota(jnp.int32, (w, p), 0) >> sh).astype(F32)
    zr, zi = _cexp(arr, air, dt, srow if rev else (ll - 1.0 - srow))
    tbr, tbi = jnp.tile(bbr, (ll, 1)), jnp.tile(bbi, (ll, 1))
    bm_ref[0, :, 0:p] = (zr * tbr - zi * tbi).astype(bm_ref.dtype)
    bm_ref[0, :, p:2 * p] = (zr * tbi + zi * tbr).astype(bm_ref.dtype)
    ek = (ll << lax.broadcasted_iota(jnp.int32, (8, p), 0)).astype(F32)
    pr, pi = _cexp(arr, air, dt, ek)
    ap_ref[0, :, 0:p] = pr
    ap_ref[0, :, p:2 * p] = pi


def s5_prep(a_re, a_im, log_dt, b_re, b_im, c_re, c_im, rev):
    g, p = a_re.shape
    cg, w = S5_GROUP_DIM, S5_W
    col = lambda t: t.reshape(g, p, 1).astype(F32)
    row = lambda t: t.reshape(g, 1, p).astype(F32)
    tr = lambda t: jnp.swapaxes(t, 1, 2).astype(F32)
    spec = lambda s: pl.BlockSpec((1,) + s, lambda i: (i, 0, 0))
    return pl.pallas_call(
        functools.partial(_s5_prep_kernel, rev=rev),
        out_shape=(jax.ShapeDtypeStruct((g, w, w), BF16), jax.ShapeDtypeStruct((g, w, 2 * p), BF16),
                   jax.ShapeDtypeStruct((g, 2 * p, w), BF16), jax.ShapeDtypeStruct((g, 8, 2 * p), F32)),
        grid=(g,),
        in_specs=[spec((p, 1)), spec((p, 1)), spec((1, p)), spec((1, p)), spec((1, 1)),
                  spec((cg, p)), spec((cg, p)), spec((p, cg)), spec((p, cg))],
        out_specs=(spec((w, w)), spec((w, 2 * p)), spec((2 * p, w)), spec((8, 2 * p))),
        compiler_params=_params("parallel"),
        name="s5_prep",
    )(col(a_re), col(a_im), row(a_re), row(a_im), log_dt.reshape(g, 1, 1).astype(F32),
      tr(b_re), tr(b_im), tr(c_re), tr(c_im))


def _s5_chunk_scan(s, ap, nchunk, rev):
    n, w2 = s.shape
    p = w2 // 2
    m = lax.broadcasted_iota(jnp.int32, (n, w2), 0) & (nchunk - 1)
    lane = lax.broadcasted_iota(jnp.int32, (1, w2), 1)

    def shift(x, k):
        if rev:
            return jnp.where(m + k <= nchunk - 1, pltpu.roll(x, n - k, 0), 0.0)
        return jnp.where(m >= k, pltpu.roll(x, k, 0), 0.0)

    x = s
    k, lvl = 1, 0
    while k < nchunk:
        a = ap[lvl:lvl + 1, :]
        a1 = jnp.where(lane < p, a, pltpu.roll(a, p, 1))
        a2 = jnp.where(lane < p, -pltpu.roll(a, p, 1), a)
        xs = shift(x, k)
        x = x + a1 * xs + a2 * pltpu.roll(xs, p, 1)
        k, lvl = 2 * k, lvl + 1
    return shift(x, 1)


def _s5_main_kernel(u_ref, kf_ref, kb_ref, bf_ref, bb_ref, cf_ref, cb_ref, af_ref, ab_ref, y_ref, *, nchunk):
    u = u_ref[0]
    y = jnp.dot(u, kf_ref[0], preferred_element_type=F32) + jnp.dot(u, kb_ref[0], preferred_element_type=F32)
    for bm_ref, cm_ref, ap_ref, rev in ((bf_ref, cf_ref, af_ref, False), (bb_ref, cb_ref, ab_ref, True)):
        s = jnp.dot(u, bm_ref[0], preferred_element_type=F32)
        xin = _s5_chunk_scan(s, ap_ref[0], nchunk, rev)
        y = y + jnp.dot(xin.astype(BF16), cm_ref[0], preferred_element_type=F32)
    y_ref[0] = y


def s5_main(u, prep_f, prep_b, nchunk):
    g, n, w = u.shape
    p2 = 2 * S5_STATE
    spec = lambda s: pl.BlockSpec((1,) + s, lambda i: (i, 0, 0))
    kf, bf, cf, af = prep_f
    kb, bb, cb, ab = prep_b
    return pl.pallas_call(
        functools.partial(_s5_main_kernel, nchunk=nchunk),
        out_shape=jax.ShapeDtypeStruct((g, n, w), F32),
        grid=(g,),
        in_specs=[spec((n, w)), spec((w, w)), spec((w, w)), spec((w, p2)), spec((w, p2)),
                  spec((p2, w)), spec((p2, w)), spec((8, p2)), spec((8, p2))],
        out_specs=spec((n, w)),
        compiler_params=_params("parallel"),
        name="s5_main",
    )(u, kf, kb, bf, bb, cf, cb, af, ab)


def _s5_out_kernel(x_ref, g_ref, y_ref, d_ref, wa_ref, wb_ref, o_ref, inv_ref, acca_ref, accb_ref):
    k = pl.program_id(1)
    tk = y_ref.shape[1]

    @pl.when(k == 0)
    def _():
        x = x_ref[...]
        inv_ref[...] = jnp.broadcast_to(lax.rsqrt(jnp.mean(x * x, axis=-1, keepdims=True) + EPS), inv_ref.shape)
        acca_ref[...] = jnp.zeros_like(acca_ref)
        accb_ref[...] = jnp.zeros_like(accb_ref)

    xk = x_ref[:, pl.ds(pl.multiple_of(k * tk, tk), tk)]
    h = xk * inv_ref[:, 0:1] * g_ref[...]
    a = _gelu_tanh(y_ref[...] + d_ref[...] * h).astype(BF16)
    acca_ref[...] += jnp.dot(a, wa_ref[...], preferred_element_type=F32)
    accb_ref[...] += jnp.dot(a, wb_ref[...], preferred_element_type=F32)

    @pl.when(k == pl.num_programs(1) - 1)
    def _():
        o_ref[...] = x_ref[...] + acca_ref[...] * _sigmoid(accb_ref[...])


def s5_out(x, g, y, d_skip, w_glu, tm=512, tk=512):
    m, d = x.shape
    n = w_glu.shape[1] // 2
    return pl.pallas_call(
        _s5_out_kernel,
        out_shape=jax.ShapeDtypeStruct((m, n), F32),
        grid=(m // tm, d // tk),
        in_specs=[
            pl.BlockSpec((tm, d), lambda i, k: (i, 0)),
            pl.BlockSpec((1, tk), lambda i, k: (0, k)),
            pl.BlockSpec((tm, tk), lambda i, k: (i, k)),
            pl.BlockSpec((1, tk), lambda i, k: (0, k)),
            pl.BlockSpec((tk, n), lambda i, k: (k, 0)),
            pl.BlockSpec((tk, n), lambda i, k: (k, 1)),
        ],
        out_specs=pl.BlockSpec((tm, n), lambda i, k: (i, 0)),
        scratch_shapes=[pltpu.VMEM((tm, 128), F32), pltpu.VMEM((tm, n), F32), pltpu.VMEM((tm, n), F32)],
        compiler_params=_params("parallel", "arbitrary"),
        name="s5_out",
    )(x, g.reshape(1, d), y, d_skip.reshape(1, d), w_glu, w_glu)


def s5_mixer(x, gmix, a_re, a_im, log_dt, b_re, b_im, c_re, c_im, d_skip, w_glu, bsz, seq):
    m, d = x.shape
    groups = d // S5_GROUP_DIM
    nchunk = seq // S5_L
    h = rmsnorm(x, gmix, out_dtype=BF16)
    u = h.reshape(m // S5_L, S5_L, groups, S5_GROUP_DIM).transpose(2, 0, 1, 3).reshape(groups, m // S5_L, S5_W)
    prep_f = s5_prep(a_re[0], a_im[0], log_dt[0], b_re[0], b_im[0], c_re[0], c_im[0], rev=False)
    prep_b = s5_prep(a_re[1], a_im[1], log_dt[1], b_re[1], b_im[1], c_re[1], c_im[1], rev=True)
    y = s5_main(u, prep_f, prep_b, nchunk)
    y = y.reshape(groups, m // S5_L, S5_L, S5_GROUP_DIM).transpose(1, 2, 0, 3).reshape(m, d)
    return s5_out(x, gmix, y, d_skip, w_glu.astype(BF16))


def _na_core(qkv, rpb, bsz, seq):
    d = qkv.shape[-1] // 3
    dh = d // NA_HEADS
    rows = seq // GRID_W
    win_r = min(NA_WIN_R, rows)
    q, k, v = jnp.split(qkv.astype(F32), 3, axis=-1)
    grid = lambda t: t.reshape(bsz, rows, GRID_W, NA_HEADS, dh)
    q, k, v = grid(q) * dh ** -0.5, grid(k), grid(v)
    row_start = np.clip(np.arange(rows) - win_r // 2, 0, rows - win_r)
    cols = np.arange(GRID_W)
    col_start = np.clip(cols - NA_WIN_C // 2, 0, GRID_W - NA_WIN_C)
    col_valid = (cols[None, :] >= col_start[:, None]) & (cols[None, :] < col_start[:, None] + NA_WIN_C)
    dc_idx = np.clip(cols[None, :] - cols[:, None] + NA_WIN_C - 1, 0, 2 * NA_WIN_C - 2)
    bias_c = jnp.where(col_valid, rpb[:, :, dc_idx].astype(F32), NEG_INF)
    bias_c = jnp.transpose(bias_c, (0, 2, 1, 3))
    dr_idx = row_start[:, None] + np.arange(win_r)[None, :] - np.arange(rows)[:, None] + NA_WIN_R - 1

    def one_row(args):
        q_r, r0, dri = args
        k_w = lax.dynamic_slice_in_dim(k, r0, win_r, axis=1)
        v_w = lax.dynamic_slice_in_dim(v, r0, win_r, axis=1)
        s = jnp.einsum('bqhd,bajhd->bhqaj', q_r, k_w, preferred_element_type=F32)
        s = s + jnp.take(bias_c, dri, axis=2)
        pr = jax.nn.softmax(s.reshape(bsz, NA_HEADS, GRID_W, win_r * GRID_W), axis=-1).reshape(s.shape)
        return jnp.einsum('bhqaj,bajhd->bqhd', pr, v_w)

    out = lax.map(one_row, (jnp.moveaxis(q, 1, 0), jnp.asarray(row_start, jnp.int32), jnp.asarray(dr_idx, jnp.int32)))
    return jnp.moveaxis(out, 0, 1).reshape(bsz * seq, d)


def _sg_core(uv, sg_norm, w_s, b_s, bsz, seq):
    width = uv.shape[-1] // 2
    n = seq // SG_CHUNK
    u, v = uv[:, :width], uv[:, width:]
    vf = v.astype(F32)
    vn = vf * lax.rsqrt(jnp.mean(vf * vf, axis=-1, keepdims=True) + EPS) * sg_norm
    vn = vn.reshape(bsz, n, SG_CHUNK, SG_GROUPS, width // SG_GROUPS)
    mixed = jnp.einsum('gts,bnsgc->bntgc', w_s, vn) + b_s.T[:, :, None]
    return u * mixed.reshape(bsz * seq, width)


def _l2norm(t):
    return t * lax.rsqrt(jnp.sum(t * t, axis=-1, keepdims=True) + EPS)


def _dwconv(x, w):
    width = w.shape[0]
    half = width // 2
    seq = x.shape[1]
    xp = jnp.pad(x, ((0, 0), (half, half), (0, 0)))
    return sum(xp[:, k:k + seq] * w[k] for k in range(width))


def _gdn_scan(q, k, v, g, beta):
    bsz, seq = q.shape[:2]
    n = seq // GDN_CHUNK
    rep = GDN_V_HEADS // GDN_QK_HEADS
    tri_incl = np.tril(np.ones((GDN_CHUNK, GDN_CHUNK), bool))
    tri_strict = np.tril(np.ones((GDN_CHUNK, GDN_CHUNK), bool), -1)
    eye = jnp.eye(GDN_CHUNK, dtype=F32)
    hp = lax.Precision.HIGHEST

    def chunks(t):
        return jnp.moveaxis(t.reshape(bsz, n, GDN_CHUNK, *t.shape[2:]), 1, 0)

    def tr(t):
        return jnp.swapaxes(t, -1, -2)

    def step(state, inp):
        qc, kc, vc, gc, bc = inp
        qh = jnp.swapaxes(jnp.repeat(qc, rep, axis=2), 1, 2)
        kh = jnp.swapaxes(jnp.repeat(kc, rep, axis=2), 1, 2)
        vh = jnp.swapaxes(vc, 1, 2)
        gam = jnp.cumsum(jnp.swapaxes(gc, 1, 2), axis=-1)
        bet = jnp.swapaxes(bc, 1, 2)
        decay = jnp.exp(jnp.where(tri_incl, gam[..., :, None] - gam[..., None, :], -jnp.inf))
        m = jnp.where(tri_strict, jnp.matmul(kh, tr(kh), precision=hp) * decay, 0.0) * bet[..., :, None]
        e_gam = jnp.exp(gam)
        rhs = jnp.concatenate([kh * (bet * e_gam)[..., None], vh * bet[..., None]], axis=-1)
        sol = lax.linalg.triangular_solve(eye + m, rhs, left_side=True, lower=True, unit_diagonal=True)
        w_mat, u_val = sol[..., :GDN_DK], sol[..., GDN_DK:]
        u = u_val - jnp.matmul(w_mat, state, precision=hp)
        o = jnp.matmul(qh * e_gam[..., None], state, precision=hp) + jnp.matmul(
            jnp.matmul(qh, tr(kh), precision=hp) * decay, u, precision=hp)
        k_dec = kh * jnp.exp(gam[..., -1:] - gam)[..., None]
        new_state = e_gam[..., -1][..., None, None] * state + jnp.matmul(tr(k_dec), u, precision=hp)
        return new_state, o

    state0 = jnp.zeros((bsz, GDN_V_HEADS, GDN_DK, GDN_DV), F32)
    _, o = lax.scan(step, state0, (chunks(q), chunks(k), chunks(v), chunks(g), chunks(beta)))
    return jnp.transpose(o, (1, 0, 3, 2, 4)).reshape(bsz, seq, GDN_V_HEADS, GDN_DV)


def _gdn_core(proj, conv_w, a_log, dt_bias, out_norm, bsz, seq):
    qk_w = GDN_QK_HEADS * GDN_DK
    v_w = GDN_V_HEADS * GDN_DV
    conv_width = 2 * qk_w + v_w
    proj = proj.reshape(bsz, seq, -1).astype(F32)
    qkv = jax.nn.silu(_dwconv(proj[..., :conv_width], conv_w))
    z = proj[..., conv_width:conv_width + v_w]
    ab = proj[..., conv_width + v_w:].reshape(bsz, seq, 2, 2, GDN_V_HEADS)
    q = _l2norm(qkv[..., :qk_w].reshape(bsz, seq, GDN_QK_HEADS, GDN_DK)) * GDN_DK ** -0.5
    k = _l2norm(qkv[..., qk_w:2 * qk_w].reshape(bsz, seq, GDN_QK_HEADS, GDN_DK))
    v = qkv[..., 2 * qk_w:].reshape(bsz, seq, GDN_V_HEADS, GDN_DV)
    decay_rate = jnp.exp(a_log.astype(F32))
    g = -decay_rate * jax.nn.softplus(ab[:, :, :, 0] + dt_bias.astype(F32))
    beta = jax.nn.sigmoid(ab[:, :, :, 1])
    o_fwd = _gdn_scan(q, k, v, g[:, :, 0], beta[:, :, 0])
    o_bwd = jnp.flip(_gdn_scan(jnp.flip(q, 1), jnp.flip(k, 1), jnp.flip(v, 1),
                               jnp.flip(g[:, :, 1], 1), jnp.flip(beta[:, :, 1], 1)), 1)
    zg = jax.nn.silu(z.reshape(bsz, seq, GDN_V_HEADS, GDN_DV))
    o = o_fwd + o_bwd
    o = o * lax.rsqrt(jnp.mean(o * o, axis=-1, keepdims=True) + EPS) * out_norm * zg
    return o.reshape(bsz * seq, v_w)


def _s5_direction(u, a_re, a_im, log_dt, b_re, b_im, c_re, c_im):
    bsz, seq, groups = u.shape[:3]
    n = seq // S5_CHUNK
    lam = lax.complex(a_re.astype(F32), a_im.astype(F32))
    dt = jnp.exp(log_dt.astype(F32))[:, None]
    a_bar = jnp.exp(lam * dt)
    b_bar = ((a_bar - 1.0) / lam)[..., None] * lax.complex(b_re.astype(F32), b_im.astype(F32))
    c = lax.complex(c_re.astype(F32), c_im.astype(F32))
    bu = jnp.einsum('gpc,bsgc->bsgp', b_bar, u.astype(jnp.complex64), precision=lax.Precision.HIGHEST)
    bu = jnp.moveaxis(bu.reshape(bsz, n, S5_CHUNK, groups, S5_STATE), 1, 0)
    a_elems = jnp.broadcast_to(a_bar, (bsz, S5_CHUNK, groups, S5_STATE))
    powers = jnp.exp(lam[None] * dt[None] * jnp.arange(1, S5_CHUNK + 1, dtype=F32)[:, None, None])

    def binop(e1, e2):
        return (e2[0] * e1[0], e2[0] * e1[1] + e2[1])

    def step(x_prev, bu_c):
        _, xs = lax.associative_scan(binop, (a_elems, bu_c), axis=1)
        xs = xs + powers[None] * x_prev[:, None]
        y = jnp.einsum('gcp,blgp->blgc', c, xs, precision=lax.Precision.HIGHEST).real
        return xs[:, -1], y

    x0 = jnp.zeros((bsz, groups, S5_STATE), jnp.complex64)
    _, ys = lax.scan(step, x0, bu)
    return jnp.moveaxis(ys, 0, 1).reshape(bsz, seq, groups, S5_GROUP_DIM)


def _s5_core(h, a_re, a_im, log_dt, b_re, b_im, c_re, c_im, d_skip, bsz, seq):
    d = h.shape[-1]
    groups = d // S5_GROUP_DIM
    u = h.reshape(bsz, seq, groups, S5_GROUP_DIM)
    y_f = _s5_direction(u, a_re[0], a_im[0], log_dt[0], b_re[0], b_im[0], c_re[0], c_im[0])
    y_b = jnp.flip(_s5_direction(jnp.flip(u, 1), a_re[1], a_im[1], log_dt[1], b_re[1], b_im[1], c_re[1], c_im[1]), 1)
    y = (y_f + y_b).reshape(bsz * seq, d) + d_skip * h
    return jax.nn.gelu(y)


def _trunk(x, p, w, bsz, seq):
    depth = w['norm_mix'].shape[0]
    bf = lambda t: t.astype(BF16)
    for i in range(depth):
        kind, j = i % N_MIXERS, i // N_MIXERS
        gmix = w['norm_mix'][i]
        if kind == 0:
            qkv = norm_matmul(x, gmix, bf(w['na_w_qkv'][j]), out_dtype=BF16)
            att = na_attention(qkv, w['na_rpb'][j], bsz, seq)
            x = matmul_res(att, bf(w['na_w_o'][j]), x)
        elif kind == 1:
            uv = norm_matmul(x, gmix, bf(w['sg_w_in'][j]), act="gelu", out_dtype=BF16)
            x = sg_out(uv, w['sg_norm'][j], w['sg_w_s'][j], w['sg_b_s'][j], bf(w['sg_w_o'][j]), x)
        elif kind == 2:
            x = gated_deltanet(x, gmix, w['gdn_w_in'][j], w['gdn_conv_w'][j], w['gdn_a_log'][j],
                               w['gdn_dt_bias'][j], w['gdn_out_norm'][j], w['gdn_w_o'][j], bsz, seq)
        else:
            x = s5_mixer(x, gmix, w['s5_a_re'][j], w['s5_a_im'][j], w['s5_log_dt'][j], w['s5_b_re'][j],
                         w['s5_b_im'][j], w['s5_c_re'][j], w['s5_c_im'][j], w['s5_d'][j], w['s5_w_glu'][j], bsz, seq)
        x = ffn(x, w['norm_ffn'][i], bf(w['ffn_w_gu'][i]), w['ffn_conv_w'][i], w['ffn_conv_b'][i],
                bf(w['ffn_w_down'][i]), seq)
        x = ple(x, p[i], w['norm_ple'][i], bf(w['ple_w_gate'][i]), bf(w['ple_w_proj'][i]))
    return x


def kernel(x_prompt, x_sample, p_prompt, p_sample, norm_mix, norm_ffn, norm_ple, final_norm, na_w_qkv, na_w_o, na_rpb, sg_w_in, sg_norm, sg_w_s, sg_b_s, sg_w_o, gdn_w_in, gdn_conv_w, gdn_a_log, gdn_dt_bias, gdn_out_norm, gdn_w_o, s5_a_re, s5_a_im, s5_log_dt, s5_b_re, s5_b_im, s5_c_re, s5_c_im, s5_d, s5_w_glu, ffn_w_gu, ffn_conv_w, ffn_conv_b, ffn_w_down, ple_w_proj, ple_w_gate):
    w = dict(norm_mix=norm_mix, norm_ffn=norm_ffn, norm_ple=norm_ple, final_norm=final_norm,
             na_w_qkv=na_w_qkv, na_w_o=na_w_o, na_rpb=na_rpb,
             sg_w_in=sg_w_in, sg_norm=sg_norm, sg_w_s=sg_w_s, sg_b_s=sg_b_s, sg_w_o=sg_w_o,
             gdn_w_in=gdn_w_in, gdn_conv_w=gdn_conv_w, gdn_a_log=gdn_a_log, gdn_dt_bias=gdn_dt_bias,
             gdn_out_norm=gdn_out_norm, gdn_w_o=gdn_w_o,
             s5_a_re=s5_a_re, s5_a_im=s5_a_im, s5_log_dt=s5_log_dt, s5_b_re=s5_b_re, s5_b_im=s5_b_im,
             s5_c_re=s5_c_re, s5_c_im=s5_c_im, s5_d=s5_d, s5_w_glu=s5_w_glu,
             ffn_w_gu=ffn_w_gu, ffn_conv_w=ffn_conv_w, ffn_conv_b=ffn_conv_b, ffn_w_down=ffn_w_down,
             ple_w_proj=ple_w_proj, ple_w_gate=ple_w_gate)
    b1, seq, d = x_prompt.shape
    b2 = x_sample.shape[0]
    bsz = b1 + b2
    x = jnp.concatenate([x_prompt, x_sample], axis=0).reshape(bsz * seq, d)
    p = jnp.concatenate([p_prompt, p_sample], axis=1).reshape(p_prompt.shape[0], bsz * seq, -1)
    x = _trunk(x, p, w, bsz, seq)
    y1 = rmsnorm(x, final_norm, row0=0, rows=b1 * seq).reshape(b1, seq, d)
    y2 = rmsnorm(x, final_norm, row0=b1 * seq, rows=b2 * seq).reshape(b2, seq, d)
    return (y1, y2)
```

```python
import functools
import math

import jax
import jax.numpy as jnp
import numpy as np
from jax import lax
from jax.experimental import pallas as pl
from jax.experimental.pallas import tpu as pltpu

F32 = jnp.float32
BF16 = jnp.bfloat16

EPS = 1e-6
NEG_INF = -1e30
GRID_W = 64
NA_HEADS = 16
NA_WIN_R = 8
NA_WIN_C = 16
SG_CHUNK = 128
SG_GROUPS = 16
GDN_QK_HEADS = 16
GDN_V_HEADS = 32
GDN_DK = 128
GDN_DV = 128
GDN_CHUNK = 64
S5_GROUP_DIM = 16
S5_STATE = 64
S5_CHUNK = 128
N_MIXERS = 4

VMEM_LIMIT_BYTES = 56 * 1024 * 1024
HALO = 16


def _params(*sem):
    return pltpu.CompilerParams(dimension_semantics=sem, vmem_limit_bytes=VMEM_LIMIT_BYTES)


def _rms(x, g):
    return x * lax.rsqrt(jnp.mean(x * x, axis=-1, keepdims=True) + EPS) * g


def _gelu_tanh(x):
    return 0.5 * x * (1.0 + jnp.tanh(math.sqrt(2.0 / math.pi) * (x + 0.044715 * (x * x * x))))


def _sigmoid(x):
    return 1.0 / (1.0 + jnp.exp(-x))


def _rmsnorm_kernel(x_ref, g_ref, o_ref):
    o_ref[...] = _rms(x_ref[...], g_ref[...]).astype(o_ref.dtype)


def rmsnorm(x, g, out_dtype=F32, tm=512, row0=0, rows=None):
    m, d = x.shape
    rows = m if rows is None else rows
    off = row0 // tm
    return pl.pallas_call(
        _rmsnorm_kernel,
        out_shape=jax.ShapeDtypeStruct((rows, d), out_dtype),
        grid=(rows // tm,),
        in_specs=[pl.BlockSpec((tm, d), lambda i: (i + off, 0)), pl.BlockSpec((1, d), lambda i: (0, 0))],
        out_specs=pl.BlockSpec((tm, d), lambda i: (i, 0)),
        compiler_params=_params("parallel"),
        name="rmsnorm",
    )(x, g.reshape(1, d))


def _norm_matmul_kernel(x_ref, g_ref, w_ref, o_ref, hn_ref, *, act):
    @pl.when(pl.program_id(1) == 0)
    def _():
        hn_ref[...] = _rms(x_ref[...], g_ref[...]).astype(BF16)

    y = jnp.dot(hn_ref[...], w_ref[...], preferred_element_type=F32)
    if act == "gelu":
        y = _gelu_tanh(y)
    o_ref[...] = y.astype(o_ref.dtype)


def norm_matmul(x, g, w, act=None, out_dtype=F32, tm=512, tn=1024):
    m, d = x.shape
    n = w.shape[1]
    tn = next(t for t in (tn, 512, 256, 128) if n % t == 0)
    return pl.pallas_call(
        functools.partial(_norm_matmul_kernel, act=act),
        out_shape=jax.ShapeDtypeStruct((m, n), out_dtype),
        grid=(m // tm, n // tn),
        in_specs=[
            pl.BlockSpec((tm, d), lambda i, j: (i, 0)),
            pl.BlockSpec((1, d), lambda i, j: (0, 0)),
            pl.BlockSpec((d, tn), lambda i, j: (0, j)),
        ],
        out_specs=pl.BlockSpec((tm, tn), lambda i, j: (i, j)),
        scratch_shapes=[pltpu.VMEM((tm, d), BF16)],
        compiler_params=_params("parallel", "arbitrary"),
        name="norm_matmul",
    )(x, g.reshape(1, d), w)


def _matmul_res_kernel(a_ref, w_ref, r_ref, o_ref):
    o_ref[...] = r_ref[...] + jnp.dot(a_ref[...].astype(BF16), w_ref[...], preferred_element_type=F32)


def matmul_res(a, w, res, tm=512, tn=1024):
    m, k = a.shape
    n = w.shape[1]
    return pl.pallas_call(
        _matmul_res_kernel,
        out_shape=jax.ShapeDtypeStruct((m, n), F32),
        grid=(m // tm, n // tn),
        in_specs=[
            pl.BlockSpec((tm, k), lambda i, j: (i, 0)),
            pl.BlockSpec((k, tn), lambda i, j: (0, j)),
            pl.BlockSpec((tm, tn), lambda i, j: (i, j)),
        ],
        out_specs=pl.BlockSpec((tm, tn), lambda i, j: (i, j)),
        compiler_params=_params("parallel", "arbitrary"),
        name="matmul_res",
    )(a, w, res)


def _glu_res_kernel(a_ref, wa_ref, wb_ref, r_ref, o_ref):
    a = a_ref[...].astype(BF16)
    ya = jnp.dot(a, wa_ref[...], preferred_element_type=F32)
    yb = jnp.dot(a, wb_ref[...], preferred_element_type=F32)
    o_ref[...] = r_ref[...] + ya * _sigmoid(yb)


def glu_res(a, w, res, tm=512, tn=512):
    m, k = a.shape
    n = w.shape[1] // 2
    nb = n // tn
    return pl.pallas_call(
        _glu_res_kernel,
        out_shape=jax.ShapeDtypeStruct((m, n), F32),
        grid=(m // tm, nb),
        in_specs=[
            pl.BlockSpec((tm, k), lambda i, j: (i, 0)),
            pl.BlockSpec((k, tn), lambda i, j: (0, j)),
            pl.BlockSpec((k, tn), lambda i, j: (0, j + nb)),
            pl.BlockSpec((tm, tn), lambda i, j: (i, j)),
        ],
        out_specs=pl.BlockSpec((tm, tn), lambda i, j: (i, j)),
        compiler_params=_params("parallel", "arbitrary"),
        name="glu_res",
    )(a, w, w, res)


def _ffn_kernel(x_ref, xp_ref, xn_ref, g_ref, wg_ref, wu_ref, cw_ref, cb_ref, wd_ref, o_ref, hn_ref, acc_ref,
                *, tm, seq):
    i = pl.program_id(0)
    j = pl.program_id(1)

    @pl.when(j == 0)
    def _():
        g = g_ref[...]
        prev_ok = jnp.where((i * tm) % seq != 0, 1.0, 0.0)
        next_ok = jnp.where(((i + 1) * tm) % seq != 0, 1.0, 0.0)
        hn_ref[0:HALO, :] = (_rms(xp_ref[...], g) * prev_ok).astype(BF16)
        hn_ref[HALO:HALO + tm, :] = _rms(x_ref[...], g).astype(BF16)
        hn_ref[HALO + tm:, :] = (_rms(xn_ref[...], g) * next_ok).astype(BF16)
        acc_ref[...] = jnp.zeros_like(acc_ref)

    rows = tm + 2 * HALO
    gate = jnp.dot(hn_ref[...], wg_ref[...], preferred_element_type=F32)
    up = jnp.dot(hn_ref[HALO:HALO + tm, :], wu_ref[...], preferred_element_type=F32)
    cw = cw_ref[...]
    g_prev = pltpu.roll(gate, 1, 0)[HALO:HALO + tm]
    g_next = pltpu.roll(gate, rows - 1, 0)[HALO:HALO + tm]
    gc = cw[0:1] * g_prev + cw[1:2] * gate[HALO:HALO + tm] + cw[2:3] * g_next + cb_ref[...]
    act = (gc * _sigmoid(gc) * up).astype(BF16)
    acc_ref[...] += jnp.dot(act, wd_ref[...], preferred_element_type=F32)

    @pl.when(j == pl.num_programs(1) - 1)
    def _():
        o_ref[...] = x_ref[...] + acc_ref[...]


def ffn(x, g, w_gu, conv_w, conv_b, w_down, seq, tm=512, tf=512):
    m, d = x.shape
    f = w_down.shape[0]
    nf = f // tf
    hb = tm // HALO
    last = m // HALO - 1
    return pl.pallas_call(
        functools.partial(_ffn_kernel, tm=tm, seq=seq),
        out_shape=jax.ShapeDtypeStruct((m, d), F32),
        grid=(m // tm, nf),
        in_specs=[
            pl.BlockSpec((tm, d), lambda i, j: (i, 0)),
            pl.BlockSpec((HALO, d), lambda i, j: (jnp.maximum(i * hb - 1, 0), 0)),
            pl.BlockSpec((HALO, d), lambda i, j: (jnp.minimum((i + 1) * hb, last), 0)),
            pl.BlockSpec((1, d), lambda i, j: (0, 0)),
            pl.BlockSpec((d, tf), lambda i, j: (0, j)),
            pl.BlockSpec((d, tf), lambda i, j: (0, j + nf)),
            pl.BlockSpec((3, tf), lambda i, j: (0, j)),
            pl.BlockSpec((1, tf), lambda i, j: (0, j)),
            pl.BlockSpec((tf, d), lambda i, j: (j, 0)),
        ],
        out_specs=pl.BlockSpec((tm, d), lambda i, j: (i, 0)),
        scratch_shapes=[pltpu.VMEM((tm + 2 * HALO, d), BF16), pltpu.VMEM((tm, d), F32)],
        compiler_params=_params("parallel", "arbitrary"),
        name="ffn",
    )(x, x, x, g.reshape(1, d), w_gu, w_gu, conv_w, conv_b.reshape(1, f), w_down)


def _ple_kernel(x_ref, p_ref, g_ref, wg_ref, wp_ref, o_ref):
    x = x_ref[...]
    hn = _rms(x, g_ref[...]).astype(BF16)
    gate = _sigmoid(jnp.dot(hn, wg_ref[...], preferred_element_type=F32))
    proj = jnp.dot(p_ref[...].astype(BF16), wp_ref[...], preferred_element_type=F32)
    o_ref[...] = x + gate * proj


def ple(x, p, g, w_gate, w_proj, tm=512):
    m, d = x.shape
    pd = p.shape[1]
    return pl.pallas_call(
        _ple_kernel,
        out_shape=jax.ShapeDtypeStruct((m, d), F32),
        grid=(m // tm,),
        in_specs=[
            pl.BlockSpec((tm, d), lambda i: (i, 0)),
            pl.BlockSpec((tm, pd), lambda i: (i, 0)),
            pl.BlockSpec((1, d), lambda i: (0, 0)),
            pl.BlockSpec((d, d), lambda i: (0, 0)),
            pl.BlockSpec((pd, d), lambda i: (0, 0)),
        ],
        out_specs=pl.BlockSpec((tm, d), lambda i: (i, 0)),
        compiler_params=_params("parallel"),
        name="ple",
    )(x, p, g.reshape(1, d), w_gate, w_proj)


GDN_BLOCK = 256


def _conv_in_kernel(x_ref, xp_ref, xn_ref, g_ref, w_ref, cw_ref, o_ref, hn_ref, *, tm, seq, nq, nqk, scale):
    i = pl.program_id(0)
    j = pl.program_id(1)

    @pl.when(j == 0)
    def _():
        g = g_ref[...]
        prev_ok = jnp.where((i * tm) % seq != 0, 1.0, 0.0)
        next_ok = jnp.where(((i + 1) * tm) % seq != 0, 1.0, 0.0)
        hn_ref[0:HALO, :] = (_rms(xp_ref[...], g) * prev_ok).astype(BF16)
        hn_ref[HALO:HALO + tm, :] = _rms(x_ref[...], g).astype(BF16)
        hn_ref[HALO + tm:, :] = (_rms(xn_ref[...], g) * next_ok).astype(BF16)

    cw = cw_ref[...]
    w = w_ref[...]
    tn = w.shape[1]
    half = tm // 2
    rows = half + 2 * HALO
    for lo in (0, half):
        y = jnp.dot(hn_ref[lo:lo + rows, :], w, preferred_element_type=F32)
        y_prev = pltpu.roll(y, 1, 0)[HALO:HALO + half]
        y_next = pltpu.roll(y, rows - 1, 0)[HALO:HALO + half]
        c = cw[0:1] * y_prev + cw[1:2] * y[HALO:HALO + half] + cw[2:3] * y_next
        c = c * _sigmoid(c)
        for s in range(tn // 128):
            cs = c[:, s * 128:(s + 1) * 128]
            inv = lax.rsqrt(jnp.sum(cs * cs, axis=-1, keepdims=True) + EPS)
            f = jnp.where(j < nq, inv * scale, jnp.where(j < nqk, inv, 1.0))
            o_ref[lo:lo + half, s * 128:(s + 1) * 128] = (cs * f).astype(o_ref.dtype)


def conv_in(x, g, w, conv_w, seq, n_q, n_qk, scale, tm=512, tn=512):
    m, d = x.shape
    n = w.shape[1]
    hb = tm // HALO
    last = m // HALO - 1
    return pl.pallas_call(
        functools.partial(_conv_in_kernel, tm=tm, seq=seq, nq=n_q // tn, nqk=n_qk // tn, scale=scale),
        out_shape=jax.ShapeDtypeStruct((m, n), BF16),
        grid=(m // tm, n // tn),
        in_specs=[
            pl.BlockSpec((tm, d), lambda i, j: (i, 0)),
            pl.BlockSpec((HALO, d), lambda i, j: (jnp.maximum(i * hb - 1, 0), 0)),
            pl.BlockSpec((HALO, d), lambda i, j: (jnp.minimum((i + 1) * hb, last), 0)),
            pl.BlockSpec((1, d), lambda i, j: (0, 0)),
            pl.BlockSpec((d, tn), lambda i, j: (0, j)),
            pl.BlockSpec((3, tn), lambda i, j: (0, j)),
        ],
        out_specs=pl.BlockSpec((tm, tn), lambda i, j: (i, j)),
        scratch_shapes=[pltpu.VMEM((tm + 2 * HALO, d), BF16)],
        compiler_params=_params("parallel", "arbitrary"),
        name="gdn_conv_in",
    )(x, x, x, g.reshape(1, d), w, conv_w)


def _gdn_gates_kernel(x_ref, g_ref, w_ref, alog_ref, bias_ref, isg_ref, o_ref):
    hn = _rms(x_ref[...], g_ref[...]).astype(BF16)
    y = jnp.dot(hn, w_ref[...], preferred_element_type=F32)
    t = y + bias_ref[...]
    softplus = jnp.maximum(t, 0.0) + jnp.log1p(jnp.exp(-jnp.abs(t)))
    isg = isg_ref[...] > 0.5
    base = jnp.where(isg, -jnp.exp(alog_ref[...]) * softplus, 0.0)
    tm, n = base.shape
    lc = GDN_CHUNK
    pos = lax.broadcasted_iota(jnp.int32, (tm, n), 0) & (lc - 1)
    pre, suf = base, base
    k = 1
    while k < lc:
        pre = pre + jnp.where(pos >= k, pltpu.roll(pre, k, 0), 0.0)
        suf = suf + jnp.where(pos + k <= lc - 1, pltpu.roll(suf, tm - k, 0), 0.0)
        k *= 2
    is_bwd = lax.broadcasted_iota(jnp.int32, (1, n), 1) >= n // 2
    o_ref[:, 0:n] = jnp.where(isg, jnp.where(is_bwd, suf, pre), _sigmoid(y))
    o_ref[:, n:2 * n] = pre + suf - base


def gdn_gates(x, g, w_ab, a_log, dt_bias, tm=512):
    m, d = x.shape
    n = w_ab.shape[1]
    nh = a_log.shape[-1]
    zeros = jnp.zeros((2, 1, nh), F32)
    arrange = lambda t: jnp.concatenate([t.reshape(2, 1, nh).astype(F32), zeros], axis=1).reshape(1, n)
    isg = jnp.concatenate([jnp.ones((2, 1, nh), F32), zeros], axis=1).reshape(1, n)
    return pl.pallas_call(
        _gdn_gates_kernel,
        out_shape=jax.ShapeDtypeStruct((m, 2 * n), F32),
        grid=(m // tm,),
        in_specs=[
            pl.BlockSpec((tm, d), lambda i: (i, 0)),
            pl.BlockSpec((1, d), lambda i: (0, 0)),
            pl.BlockSpec((d, n), lambda i: (0, 0)),
            pl.BlockSpec((1, n), lambda i: (0, 0)),
            pl.BlockSpec((1, n), lambda i: (0, 0)),
            pl.BlockSpec((1, n), lambda i: (0, 0)),
        ],
        out_specs=pl.BlockSpec((tm, 2 * n), lambda i: (i, 0)),
        compiler_params=_params("parallel"),
        name="gdn_gates",
    )(x, g.reshape(1, d), w_ab, arrange(a_log), arrange(dt_bias), isg)


def _gdn_chains(q_ref, k_ref, v_ref, gc_ref, gr_ref, o_ref, rev):
    c_sz, lc = GDN_BLOCK, GDN_CHUNK
    d = 1 if rev else 0
    ii = lax.broadcasted_iota(jnp.int32, (c_sz, c_sz), 0)
    jj = lax.broadcasted_iota(jnp.int32, (c_sz, c_sz), 1)
    sh = lc.bit_length() - 1
    same = (ii >> sh) == (jj >> sh)
    incl = jnp.logical_and(same, (jj >= ii) if rev else (jj <= ii))
    eye = ii == jj
    q = q_ref[...]
    k = k_ref[...]
    kf = k.astype(F32)
    qf = q.astype(F32)
    gc = gc_ref[0, 0][:, 8 * d:8 * d + 8]
    gr = gr_ref[0, 0][8 * d:8 * d + 8, :]
    gam_c = gc[:, 0:2]
    gam_r = gr[0:2]
    end_r = gr[4:6]
    gram = lax.dot_general(k, k, (((1,), (1,)), ((), ())), preferred_element_type=F32)
    qk = lax.dot_general(q, k, (((1,), (1,)), ((), ())), preferred_element_type=F32)
    chains = []
    for hs in range(2):
        gcol = gam_c[:, hs:hs + 1]
        grow = gam_r[hs:hs + 1, :]
        bcol = gc[:, 2 + hs:3 + hs]
        dec = jnp.exp(jnp.where(incl, gcol - grow, NEG_INF))
        e_g = jnp.exp(gcol)
        vh = v_ref[:, hs * GDN_DV:(hs + 1) * GDN_DV].astype(F32)
        chains.append(dict(
            n=(-(gram * jnp.where(eye, 0.0, dec)) * bcol).astype(BF16),
            x=jnp.concatenate([kf * (bcol * e_g), vh * bcol], axis=1),
            attn=(qk * dec).astype(BF16),
            qe=qf * e_g,
            kd=kf * jnp.exp(gc[:, 4 + hs:5 + hs] - gcol),
            end=end_r[hs:hs + 1, :],
            o_ref=o_ref, hs=hs, idx=2 * d + hs, rev=rev))
    return chains


def _gdn_scan_kernel(qf_ref, kf_ref, vf_ref, gcf_ref, grf_ref,
                     qb_ref, kb_ref, vb_ref, gcb_ref, grb_ref, of_ref, ob_ref, s_ref):
    @pl.when(pl.program_id(2) == 0)
    def _():
        s_ref[...] = jnp.zeros_like(s_ref)

    c_sz, lc = GDN_BLOCK, GDN_CHUNK
    nchunk = c_sz // lc
    width = GDN_DK + GDN_DV
    chains = (_gdn_chains(qf_ref, kf_ref, vf_ref, gcf_ref, grf_ref, of_ref, False)
              + _gdn_chains(qb_ref, kb_ref, vb_ref, gcb_ref, grb_ref, ob_ref, True))
    nlev = lc.bit_length() - 1
    for lvl in range(nlev):
        for ch in chains:
            nb = ch['n']
            xb = ch['x'].astype(BF16)
            if lvl < nlev - 1:
                r = jnp.dot(nb, jnp.concatenate([xb, nb], axis=1), preferred_element_type=F32)
                ch['x'] = ch['x'] + r[:, :width]
                ch['n'] = r[:, width:].astype(BF16)
            else:
                ch['x'] = ch['x'] + jnp.dot(nb, xb, preferred_element_type=F32)
    irow = lax.broadcasted_iota(jnp.int32, (c_sz, 1), 0) >> nlev
    for ch in chains:
        wub = ch['x'].astype(BF16)
        awu = jnp.dot(ch['attn'], wub, preferred_element_type=F32)
        ch['qeff'] = (ch['qe'] - awu[:, :GDN_DK]).astype(BF16)
        ch['o_in'] = awu[:, GDN_DK:]
        ch['kwu'] = [lax.dot_general(jnp.where(irow == c, ch['kd'], 0.0).astype(BF16), wub,
                                     (((0,), (0,)), ((), ())), preferred_element_type=F32)
                     for c in range(nchunk)]
        ch['s'] = s_ref[ch['idx']]
    for step in range(nchunk):
        for ch in chains:
            c = nchunk - 1 - step if ch['rev'] else step
            r0 = c * lc
            s = ch['s']
            sb = s.astype(BF16)
            o_c = ch['o_in'][r0:r0 + lc] + jnp.dot(ch['qeff'][r0:r0 + lc], sb, preferred_element_type=F32)
            ch['o_ref'][r0:r0 + lc, ch['hs'] * GDN_DV:(ch['hs'] + 1) * GDN_DV] = o_c.astype(ch['o_ref'].dtype)
            kwu = ch['kwu'][c]
            e_end = jnp.exp(ch['end'][:, r0:r0 + 1])
            ch['s'] = (e_end * s - jnp.dot(kwu[:, :GDN_DK].astype(BF16), sb, preferred_element_type=F32)
                       + kwu[:, GDN_DK:])
    for ch in chains:
        s_ref[ch['idx']] = ch['s']


def gdn_scan(qkv, gcol, grow, bsz, seq):
    c_sz = GDN_BLOCK
    nb = seq // c_sz
    hq = GDN_QK_HEADS
    kcol = hq
    vcol = (2 * hq * GDN_DK) // (2 * GDN_DV)
    fwd = lambda b, h, c: c
    bwd = lambda b, h, c: nb - 1 - c

    def specs(pos):
        return [
            pl.BlockSpec((c_sz, GDN_DK), lambda b, h, c: (b * nb + pos(b, h, c), h)),
            pl.BlockSpec((c_sz, GDN_DK), lambda b, h, c: (b * nb + pos(b, h, c), kcol + h)),
            pl.BlockSpec((c_sz, 2 * GDN_DV), lambda b, h, c: (b * nb + pos(b, h, c), vcol + h)),
            pl.BlockSpec((1, 1, c_sz, 16), lambda b, h, c: (b, h, pos(b, h, c), 0)),
            pl.BlockSpec((1, 1, 16, c_sz), lambda b, h, c: (b, h, 0, pos(b, h, c))),
        ]

    out = jax.ShapeDtypeStruct((bsz * seq, GDN_V_HEADS * GDN_DV), BF16)
    return pl.pallas_call(
        _gdn_scan_kernel,
        out_shape=(out, out),
        grid=(bsz, hq, nb),
        in_specs=specs(fwd) + specs(bwd),
        out_specs=(
            pl.BlockSpec((c_sz, 2 * GDN_DV), lambda b, h, c: (b * nb + c, h)),
            pl.BlockSpec((c_sz, 2 * GDN_DV), lambda b, h, c: (b * nb + nb - 1 - c, h)),
        ),
        scratch_shapes=[pltpu.VMEM((4, GDN_DK, GDN_DV), F32)],
        compiler_params=_params("parallel", "parallel", "arbitrary"),
        name="gdn_scan",
    )(qkv, qkv, qkv, gcol, grow, qkv, qkv, qkv, gcol, grow)


def _gdn_out_kernel(of_ref, ob_ref, z_ref, gn_ref, w_ref, r_ref, o_ref):
    @pl.when(pl.program_id(1) == 0)
    def _():
        o_ref[...] = r_ref[...]

    gn = gn_ref[...]
    parts = []
    for h in range(of_ref.shape[1] // GDN_DV):
        sl = slice(h * GDN_DV, (h + 1) * GDN_DV)
        o = of_ref[:, sl].astype(F32) + ob_ref[:, sl].astype(F32)
        z = z_ref[:, sl].astype(F32)
        parts.append((_rms(o, gn) * (z * _sigmoid(z))).astype(BF16))
    o_ref[...] += jnp.dot(jnp.concatenate(parts, axis=1), w_ref[...], preferred_element_type=F32)


def gdn_out(o_f, o_b, z, out_norm, w_o, res, tm=512, tk=1024):
    m, k = o_f.shape
    n = w_o.shape[1]
    return pl.pallas_call(
        _gdn_out_kernel,
        out_shape=jax.ShapeDtypeStruct((m, n), F32),
        grid=(m // tm, k // tk),
        in_specs=[
            pl.BlockSpec((tm, tk), lambda i, j: (i, j)),
            pl.BlockSpec((tm, tk), lambda i, j: (i, j)),
            pl.BlockSpec((tm, tk), lambda i, j: (i, j)),
            pl.BlockSpec((1, GDN_DV), lambda i, j: (0, 0)),
            pl.BlockSpec((tk, n), lambda i, j: (j, 0)),
            pl.BlockSpec((tm, n), lambda i, j: (i, 0)),
        ],
        out_specs=pl.BlockSpec((tm, n), lambda i, j: (i, 0)),
        compiler_params=_params("parallel", "arbitrary"),
        name="gdn_out",
    )(o_f, o_b, z, out_norm.reshape(1, GDN_DV), w_o, res)


def gated_deltanet(x, gmix, w_in, conv_w, a_log, dt_bias, out_norm, w_o, bsz, seq):
    qk_w = GDN_QK_HEADS * GDN_DK
    v_w = GDN_V_HEADS * GDN_DV
    cw = 2 * qk_w + v_w
    w_in = w_in.astype(BF16)
    qkv = conv_in(x, gmix, w_in[:, :cw], conv_w, seq, qk_w, 2 * qk_w, GDN_DK ** -0.5)
    z = norm_matmul(x, gmix, w_in[:, cw:cw + v_w], out_dtype=BF16)
    gb = gdn_gates(x, gmix, w_in[:, cw + v_w:], a_log, dt_bias)
    nab = gb.shape[1] // 2
    ab = gb[:, :nab].reshape(bsz, seq, 2, 2, GDN_QK_HEADS, 2).transpose(0, 4, 1, 2, 3, 5)
    tot = gb[:, nab:].reshape(bsz, seq, 2, 2, GDN_QK_HEADS, 2)[:, :, :, 0].transpose(0, 3, 1, 2, 4)[..., None, :]
    gcol = jnp.concatenate([ab, tot, jnp.zeros_like(tot)], axis=4).reshape(bsz, GDN_QK_HEADS, seq, 16)
    grow = jnp.swapaxes(gcol, 2, 3)
    o_f, o_b = gdn_scan(qkv, gcol, grow, bsz, seq)
    return gdn_out(o_f, o_b, z, out_norm, w_o.astype(BF16), x)


def _na_bias_table(rpb, rows):
    win_r = min(NA_WIN_R, rows)
    cols = np.arange(GRID_W)
    col_start = np.clip(cols - NA_WIN_C // 2, 0, GRID_W - NA_WIN_C)
    col_valid = (cols[None, :] >= col_start[:, None]) & (cols[None, :] < col_start[:, None] + NA_WIN_C)
    dc_idx = np.clip(cols[None, :] - cols[:, None] + NA_WIN_C - 1, 0, 2 * NA_WIN_C - 2)
    bias_c = jnp.where(col_valid, rpb[:, :, dc_idx].astype(F32), NEG_INF)
    dr = np.arange(NA_WIN_R)[:, None] + np.arange(win_r)[None, :]
    tab = bias_c[:, dr]
    return jnp.transpose(tab, (0, 1, 3, 2, 4)).reshape(rpb.shape[0], NA_WIN_R, GRID_W, win_r * GRID_W)


def _na_kernel(q_ref, k_ref, v_ref, b_ref, o_ref, *, rows, win_r, scale):
    wk = win_r * GRID_W

    group = 8 if rows % 8 == 0 else 1

    def body(it, carry):
        rs = [it * group + i for i in range(group)]
        r0s = [jnp.clip(r - win_r // 2, 0, rows - win_r) for r in rs]
        scores = []
        for r, r0 in zip(rs, r0s):
            q = q_ref[pl.ds(pl.multiple_of(r * GRID_W, GRID_W), GRID_W), :]
            kw = k_ref[pl.ds(pl.multiple_of(r0 * GRID_W, GRID_W), wk), :]
            s = lax.dot_general(q, kw, (((1,), (1,)), ((), ())), preferred_element_type=F32) * scale
            scores.append(s + b_ref[0, r0 - r + NA_WIN_R - 1])
        probs, dens = [], []
        for s in scores:
            p = jnp.exp(s - jnp.max(s, axis=-1, keepdims=True))
            dens.append(jnp.sum(p, axis=-1, keepdims=True))
            probs.append(p.astype(BF16))
        for r, r0, p, den in zip(rs, r0s, probs, dens):
            vw = v_ref[pl.ds(pl.multiple_of(r0 * GRID_W, GRID_W), wk), :]
            o = jnp.dot(p, vw, preferred_element_type=F32) / den
            o_ref[pl.ds(pl.multiple_of(r * GRID_W, GRID_W), GRID_W), :] = o.astype(o_ref.dtype)
        return carry

    lax.fori_loop(0, rows // group, body, 0)


def na_attention(qkv, rpb, bsz, seq):
    d = qkv.shape[1] // 3
    dh = d // NA_HEADS
    rows = seq // GRID_W
    win_r = min(NA_WIN_R, rows)
    table = _na_bias_table(rpb, rows)
    blk = lambda off: pl.BlockSpec((seq, dh), lambda b, h: (b, off + h))
    return pl.pallas_call(
        functools.partial(_na_kernel, rows=rows, win_r=win_r, scale=dh ** -0.5),
        out_shape=jax.ShapeDtypeStruct((bsz * seq, d), BF16),
        grid=(bsz, NA_HEADS),
        in_specs=[blk(0), blk(NA_HEADS), blk(2 * NA_HEADS),
                  pl.BlockSpec((1, NA_WIN_R, GRID_W, win_r * GRID_W), lambda b, h: (h, 0, 0, 0))],
        out_specs=blk(0),
        compiler_params=_params("parallel", "parallel"),
        name="na_attention",
    )(qkv, qkv, qkv, table)


def _sg_out_kernel(u_ref, v_ref, gn_ref, ws_ref, bs_ref, w_ref, r_ref, o_ref, a_ref, *, tm):
    @pl.when(pl.program_id(1) == 0)
    def _():
        vn = _rms(v_ref[...].astype(F32), gn_ref[...]).astype(BF16)
        bs = bs_ref[...]
        gd = vn.shape[1] // SG_GROUPS
        for c in range(tm // SG_CHUNK):
            rs = slice(c * SG_CHUNK, (c + 1) * SG_CHUNK)
            for g in range(SG_GROUPS):
                cs = slice(g * gd, (g + 1) * gd)
                mixed = jnp.dot(ws_ref[g], vn[rs, cs], preferred_element_type=F32) + bs[:, g:g + 1]
                a_ref[rs, cs] = (u_ref[rs, cs].astype(F32) * mixed).astype(BF16)

    o_ref[...] = r_ref[...] + jnp.dot(a_ref[...], w_ref[...], preferred_element_type=F32)


def sg_out(uv, sg_norm, w_s, b_s, w_o, res, tm=512, tn=1024):
    m = uv.shape[0]
    width = uv.shape[1] // 2
    n = w_o.shape[1]
    return pl.pallas_call(
        functools.partial(_sg_out_kernel, tm=tm),
        out_shape=jax.ShapeDtypeStruct((m, n), F32),
        grid=(m // tm, n // tn),
        in_specs=[
            pl.BlockSpec((tm, width), lambda i, j: (i, 0)),
            pl.BlockSpec((tm, width), lambda i, j: (i, 1)),
            pl.BlockSpec((1, width), lambda i, j: (0, 0)),
            pl.BlockSpec(w_s.shape, lambda i, j: (0, 0, 0)),
            pl.BlockSpec((SG_CHUNK, SG_GROUPS), lambda i, j: (0, 0)),
            pl.BlockSpec((width, tn), lambda i, j: (0, j)),
            pl.BlockSpec((tm, tn), lambda i, j: (i, j)),
        ],
        out_specs=pl.BlockSpec((tm, tn), lambda i, j: (i, j)),
        scratch_shapes=[pltpu.VMEM((tm, width), BF16)],
        compiler_params=_params("parallel", "arbitrary"),
        name="sg_out",
    )(uv, uv, sg_norm.reshape(1, width), w_s.astype(BF16), b_s.T.astype(F32), w_o, res)


S5_L = 32
S5_W = S5_L * S5_GROUP_DIM


def _cexp(are, aim, dt, e):
    mag = jnp.exp(are * dt * e)
    ang = aim * dt * e
    return mag * jnp.cos(ang), mag * jnp.sin(ang)


def _s5_prep_kernel(arc_ref, aic_ref, arr_ref, air_ref, ldt_ref, btr_ref, bti_ref, ctr_ref, cti_ref,
                    k_ref, bm_ref, cm_ref, ap_ref, *, rev):
    hp = lax.Precision.HIGHEST
    ll, cg, w, p = S5_L, S5_GROUP_DIM, S5_W, S5_STATE
    sh = cg.bit_length() - 1
    dt = jnp.exp(ldt_ref[0])
    arc, aic = arc_ref[0], aic_ref[0]
    arr, air = arr_ref[0], air_ref[0]
    abr, abi = _cexp(arr, air, dt, 1.0)
    nr, ni = abr - 1.0, abi
    den = arr * arr + air * air
    cr, ci = (nr * arr + ni * air) / den, (ni * arr - nr * air) / den
    btr, bti = btr_ref[0], bti_ref[0]
    bbr, bbi = cr * btr - ci * bti, cr * bti + ci * btr
    lane = lax.broadcasted_iota(jnp.int32, (cg, w), 1)
    sel = jnp.where((lane & (cg - 1)) == lax.broadcasted_iota(jnp.int32, (cg, w), 0), 1.0, 0.0)
    cer = jnp.dot(ctr_ref[0], sel, precision=hp, preferred_element_type=F32)
    cei = jnp.dot(cti_ref[0], sel, precision=hp, preferred_element_type=F32)
    tl = (lax.broadcasted_iota(jnp.int32, (p, w), 1) >> sh).astype(F32)

    def cz(e):
        zr, zi = _cexp(arc, aic, dt, e)
        return cer * zr - cei * zi, cer * zi + cei * zr

    czr, czi = cz((ll - 1.0 - tl) if rev else tl)
    r = jnp.dot(bbr, czr, precision=hp, preferred_element_type=F32) - jnp.dot(bbi, czi, precision=hp,
                                                                                preferred_element_type=F32)
    lane_r = lax.broadcasted_iota(jnp.int32, (cg, w), 1)
    for s in range(ll):
        if rev:
            blk = jnp.where(lane_r < cg * (s + 1), pltpu.roll(r, (w - cg * (ll - 1 - s)) % w, 1), 0.0)
        else:
            blk = jnp.where(lane_r >= cg * s, pltpu.roll(r, cg * s, 1), 0.0)
        k_ref[0, s * cg:(s + 1) * cg, :] = blk.astype(k_ref.dtype)
    c1r, c1i = cz((ll - tl) if rev else (tl + 1.0))
    cm_ref[0, 0:p, :] = c1r.astype(cm_ref.dtype)
    cm_ref[0, p:2 * p, :] = (-c1i).astype(cm_ref.dtype)
    srow = (lax.broadcasted_iota(jnp.int32, (w, p), 0) >> sh).astype(F32)
    zr, zi = _cexp(arr, air, dt, srow if rev else (ll - 1.0 - srow))
    tbr, tbi = jnp.tile(bbr, (ll, 1)), jnp.tile(bbi, (ll, 1))
    bm_ref[0, :, 0:p] = (zr * tbr - zi * tbi).astype(bm_ref.dtype)
    bm_ref[0, :, p:2 * p] = (zr * tbi + zi * tbr).astype(bm_ref.dtype)
    ek = (ll << lax.broadcasted_iota(jnp.int32, (8, p), 0)).astype(F32)
    pr, pi = _cexp(arr, air, dt, ek)
    ap_ref[0, :, 0:p] = pr
    ap_ref[0, :, p:2 * p] = pi


def s5_prep(a_re, a_im, log_dt, b_re, b_im, c_re, c_im, rev):
    g, p = a_re.shape
    cg, w = S5_GROUP_DIM, S5_W
    col = lambda t: t.reshape(g, p, 1).astype(F32)
    row = lambda t: t.reshape(g, 1, p).astype(F32)
    tr = lambda t: jnp.swapaxes(t, 1, 2).astype(F32)
    spec = lambda s: pl.BlockSpec((1,) + s, lambda i: (i, 0, 0))
    return pl.pallas_call(
        functools.partial(_s5_prep_kernel, rev=rev),
        out_shape=(jax.ShapeDtypeStruct((g, w, w), BF16), jax.ShapeDtypeStruct((g, w, 2 * p), BF16),
                   jax.ShapeDtypeStruct((g, 2 * p, w), BF16), jax.ShapeDtypeStruct((g, 8, 2 * p), F32)),
        grid=(g,),
        in_specs=[spec((p, 1)), spec((p, 1)), spec((1, p)), spec((1, p)), spec((1, 1)),
                  spec((cg, p)), spec((cg, p)), spec((p, cg)), spec((p, cg))],
        out_specs=(spec((w, w)), spec((w, 2 * p)), spec((2 * p, w)), spec((8, 2 * p))),
        compiler_params=_params("parallel"),
        name="s5_prep",
    )(col(a_re), col(a_im), row(a_re), row(a_im), log_dt.reshape(g, 1, 1).astype(F32),
      tr(b_re), tr(b_im), tr(c_re), tr(c_im))


def _s5_chunk_scan(s, ap, nchunk, rev):
    n, w2 = s.shape
    p = w2 // 2
    m = lax.broadcasted_iota(jnp.int32, (n, w2), 0) & (nchunk - 1)
    lane = lax.broadcasted_iota(jnp.int32, (1, w2), 1)

    def shift(x, k):
        if rev:
            return jnp.where(m + k <= nchunk - 1, pltpu.roll(x, n - k, 0), 0.0)
        return jnp.where(m >= k, pltpu.roll(x, k, 0), 0.0)

    x = s
    k, lvl = 1, 0
    while k < nchunk:
        a = ap[lvl:lvl + 1, :]
        a1 = jnp.where(lane < p, a, pltpu.roll(a, p, 1))
        a2 = jnp.where(lane < p, -pltpu.roll(a, p, 1), a)
        xs = shift(x, k)
        x = x + a1 * xs + a2 * pltpu.roll(xs, p, 1)
        k, lvl = 2 * k, lvl + 1
    return shift(x, 1)


def _s5_main_kernel(u_ref, kf_ref, kb_ref, bf_ref, bb_ref, cf_ref, cb_ref, af_ref, ab_ref, y_ref, *, nchunk):
    u = u_ref[0]
    y = jnp.dot(u, kf_ref[0], preferred_element_type=F32) + jnp.dot(u, kb_ref[0], preferred_element_type=F32)
    for bm_ref, cm_ref, ap_ref, rev in ((bf_ref, cf_ref, af_ref, False), (bb_ref, cb_ref, ab_ref, True)):
        s = jnp.dot(u, bm_ref[0], preferred_element_type=F32)
        xin = _s5_chunk_scan(s, ap_ref[0], nchunk, rev)
        y = y + jnp.dot(xin.astype(BF16), cm_ref[0], preferred_element_type=F32)
    y_ref[0] = y.astype(y_ref.dtype)


def s5_main(u, prep_f, prep_b, nchunk):
    g, n, w = u.shape
    p2 = 2 * S5_STATE
    spec = lambda s: pl.BlockSpec((1,) + s, lambda i: (i, 0, 0))
    kf, bf, cf, af = prep_f
    kb, bb, cb, ab = prep_b
    return pl.pallas_call(
        functools.partial(_s5_main_kernel, nchunk=nchunk),
        out_shape=jax.ShapeDtypeStruct((g, n, w), BF16),
        grid=(g,),
        in_specs=[spec((n, w)), spec((w, w)), spec((w, w)), spec((w, p2)), spec((w, p2)),
                  spec((p2, w)), spec((p2, w)), spec((8, p2)), spec((8, p2))],
        out_specs=spec((n, w)),
        compiler_params=_params("parallel"),
        name="s5_main",
    )(u, kf, kb, bf, bb, cf, cb, af, ab)


def _s5_out_kernel(x_ref, g_ref, y_ref, d_ref, wa_ref, wb_ref, o_ref, inv_ref, acca_ref, accb_ref):
    k = pl.program_id(1)
    tk = y_ref.shape[1]

    @pl.when(k == 0)
    def _():
        x = x_ref[...]
        inv_ref[...] = jnp.broadcast_to(lax.rsqrt(jnp.mean(x * x, axis=-1, keepdims=True) + EPS), inv_ref.shape)
        acca_ref[...] = jnp.zeros_like(acca_ref)
        accb_ref[...] = jnp.zeros_like(accb_ref)

    xk = x_ref[:, pl.ds(pl.multiple_of(k * tk, tk), tk)]
    h = xk * inv_ref[:, 0:1] * g_ref[...]
    a = _gelu_tanh(y_ref[...].astype(F32) + d_ref[...] * h).astype(BF16)
    acca_ref[...] += jnp.dot(a, wa_ref[...], preferred_element_type=F32)
    accb_ref[...] += jnp.dot(a, wb_ref[...], preferred_element_type=F32)

    @pl.when(k == pl.num_programs(1) - 1)
    def _():
        o_ref[...] = x_ref[...] + acca_ref[...] * _sigmoid(accb_ref[...])


def s5_out(x, g, y, d_skip, w_glu, tm=512, tk=512):
    m, d = x.shape
    n = w_glu.shape[1] // 2
    return pl.pallas_call(
        _s5_out_kernel,
        out_shape=jax.ShapeDtypeStruct((m, n), F32),
        grid=(m // tm, d // tk),
        in_specs=[
            pl.BlockSpec((tm, d), lambda i, k: (i, 0)),
            pl.BlockSpec((1, tk), lambda i, k: (0, k)),
            pl.BlockSpec((tm, tk), lambda i, k: (i, k)),
            pl.BlockSpec((1, tk), lambda i, k: (0, k)),
            pl.BlockSpec((tk, n), lambda i, k: (k, 0)),
            pl.BlockSpec((tk, n), lambda i, k: (k, 1)),
        ],
        out_specs=pl.BlockSpec((tm, n), lambda i, k: (i, 0)),
        scratch_shapes=[pltpu.VMEM((tm, 128), F32), pltpu.VMEM((tm, n), F32), pltpu.VMEM((tm, n), F32)],
        compiler_params=_params("parallel", "arbitrary"),
        name="s5_out",
    )(x, g.reshape(1, d), y, d_skip.reshape(1, d), w_glu, w_glu)


def s5_mixer(x, gmix, a_re, a_im, log_dt, b_re, b_im, c_re, c_im, d_skip, w_glu, bsz, seq):
    m, d = x.shape
    groups = d // S5_GROUP_DIM
    nchunk = seq // S5_L
    h = rmsnorm(x, gmix, out_dtype=BF16)
    u = h.reshape(m // S5_L, S5_L, groups, S5_GROUP_DIM).transpose(2, 0, 1, 3).reshape(groups, m // S5_L, S5_W)
    prep_f = s5_prep(a_re[0], a_im[0], log_dt[0], b_re[0], b_im[0], c_re[0], c_im[0], rev=False)
    prep_b = s5_prep(a_re[1], a_im[1], log_dt[1], b_re[1], b_im[1], c_re[1], c_im[1], rev=True)
    y = s5_main(u, prep_f, prep_b, nchunk)
    y = y.reshape(groups, m // S5_L, S5_L, S5_GROUP_DIM).transpose(1, 2, 0, 3).reshape(m, d)
    return s5_out(x, gmix, y, d_skip, w_glu.astype(BF16))


def _na_core(qkv, rpb, bsz, seq):
    d = qkv.shape[-1] // 3
    dh = d // NA_HEADS
    rows = seq // GRID_W
    win_r = min(NA_WIN_R, rows)
    q, k, v = jnp.split(qkv.astype(F32), 3, axis=-1)
    grid = lambda t: t.reshape(bsz, rows, GRID_W, NA_HEADS, dh)
    q, k, v = grid(q) * dh ** -0.5, grid(k), grid(v)
    row_start = np.clip(np.arange(rows) - win_r // 2, 0, rows - win_r)
    cols = np.arange(GRID_W)
    col_start = np.clip(cols - NA_WIN_C // 2, 0, GRID_W - NA_WIN_C)
    col_valid = (cols[None, :] >= col_start[:, None]) & (cols[None, :] < col_start[:, None] + NA_WIN_C)
    dc_idx = np.clip(cols[None, :] - cols[:, None] + NA_WIN_C - 1, 0, 2 * NA_WIN_C - 2)
    bias_c = jnp.where(col_valid, rpb[:, :, dc_idx].astype(F32), NEG_INF)
    bias_c = jnp.transpose(bias_c, (0, 2, 1, 3))
    dr_idx = row_start[:, None] + np.arange(win_r)[None, :] - np.arange(rows)[:, None] + NA_WIN_R - 1

    def one_row(args):
        q_r, r0, dri = args
        k_w = lax.dynamic_slice_in_dim(k, r0, win_r, axis=1)
        v_w = lax.dynamic_slice_in_dim(v, r0, win_r, axis=1)
        s = jnp.einsum('bqhd,bajhd->bhqaj', q_r, k_w, preferred_element_type=F32)
        s = s + jnp.take(bias_c, dri, axis=2)
        pr = jax.nn.softmax(s.reshape(bsz, NA_HEADS, GRID_W, win_r * GRID_W), axis=-1).reshape(s.shape)
        return jnp.einsum('bhqaj,bajhd->bqhd', pr, v_w)

    out = lax.map(one_row, (jnp.moveaxis(q, 1, 0), jnp.asarray(row_start, jnp.int32), jnp.asarray(dr_idx, jnp.int32)))
    return jnp.moveaxis(out, 0, 1).reshape(bsz * seq, d)


def _sg_core(uv, sg_norm, w_s, b_s, bsz, seq):
    width = uv.shape[-1] // 2
    n = seq // SG_CHUNK
    u, v = uv[:, :width], uv[:, width:]
    vf = v.astype(F32)
    vn = vf * lax.rsqrt(jnp.mean(vf * vf, axis=-1, keepdims=True) + EPS) * sg_norm
    vn = vn.reshape(bsz, n, SG_CHUNK, SG_GROUPS, width // SG_GROUPS)
    mixed = jnp.einsum('gts,bnsgc->bntgc', w_s, vn) + b_s.T[:, :, None]
    return u * mixed.reshape(bsz * seq, width)


def _l2norm(t):
    return t * lax.rsqrt(jnp.sum(t * t, axis=-1, keepdims=True) + EPS)


def _dwconv(x, w):
    width = w.shape[0]
    half = width // 2
    seq = x.shape[1]
    xp = jnp.pad(x, ((0, 0), (half, half), (0, 0)))
    return sum(xp[:, k:k + seq] * w[k] for k in range(width))


def _gdn_scan(q, k, v, g, beta):
    bsz, seq = q.shape[:2]
    n = seq // GDN_CHUNK
    rep = GDN_V_HEADS // GDN_QK_HEADS
    tri_incl = np.tril(np.ones((GDN_CHUNK, GDN_CHUNK), bool))
    tri_strict = np.tril(np.ones((GDN_CHUNK, GDN_CHUNK), bool), -1)
    eye = jnp.eye(GDN_CHUNK, dtype=F32)
    hp = lax.Precision.HIGHEST

    def chunks(t):
        return jnp.moveaxis(t.reshape(bsz, n, GDN_CHUNK, *t.shape[2:]), 1, 0)

    def tr(t):
        return jnp.swapaxes(t, -1, -2)

    def step(state, inp):
        qc, kc, vc, gc, bc = inp
        qh = jnp.swapaxes(jnp.repeat(qc, rep, axis=2), 1, 2)
        kh = jnp.swapaxes(jnp.repeat(kc, rep, axis=2), 1, 2)
        vh = jnp.swapaxes(vc, 1, 2)
        gam = jnp.cumsum(jnp.swapaxes(gc, 1, 2), axis=-1)
        bet = jnp.swapaxes(bc, 1, 2)
        decay = jnp.exp(jnp.where(tri_incl, gam[..., :, None] - gam[..., None, :], -jnp.inf))
        m = jnp.where(tri_strict, jnp.matmul(kh, tr(kh), precision=hp) * decay, 0.0) * bet[..., :, None]
        e_gam = jnp.exp(gam)
        rhs = jnp.concatenate([kh * (bet * e_gam)[..., None], vh * bet[..., None]], axis=-1)
        sol = lax.linalg.triangular_solve(eye + m, rhs, left_side=True, lower=True, unit_diagonal=True)
        w_mat, u_val = sol[..., :GDN_DK], sol[..., GDN_DK:]
        u = u_val - jnp.matmul(w_mat, state, precision=hp)
        o = jnp.matmul(qh * e_gam[..., None], state, precision=hp) + jnp.matmul(
            jnp.matmul(qh, tr(kh), precision=hp) * decay, u, precision=hp)
        k_dec = kh * jnp.exp(gam[..., -1:] - gam)[..., None]
        new_state = e_gam[..., -1][..., None, None] * state + jnp.matmul(tr(k_dec), u, precision=hp)
        return new_state, o

    state0 = jnp.zeros((bsz, GDN_V_HEADS, GDN_DK, GDN_DV), F32)
    _, o = lax.scan(step, state0, (chunks(q), chunks(k), chunks(v), chunks(g), chunks(beta)))
    return jnp.transpose(o, (1, 0, 3, 2, 4)).reshape(bsz, seq, GDN_V_HEADS, GDN_DV)


def _gdn_core(proj, conv_w, a_log, dt_bias, out_norm, bsz, seq):
    qk_w = GDN_QK_HEADS * GDN_DK
    v_w = GDN_V_HEADS * GDN_DV
    conv_width = 2 * qk_w + v_w
    proj = proj.reshape(bsz, seq, -1).astype(F32)
    qkv = jax.nn.silu(_dwconv(proj[..., :conv_width], conv_w))
    z = proj[..., conv_width:conv_width + v_w]
    ab = proj[..., conv_width + v_w:].reshape(bsz, seq, 2, 2, GDN_V_HEADS)
    q = _l2norm(qkv[..., :qk_w].reshape(bsz, seq, GDN_QK_HEADS, GDN_DK)) * GDN_DK ** -0.5
    k = _l2norm(qkv[..., qk_w:2 * qk_w].reshape(bsz, seq, GDN_QK_HEADS, GDN_DK))
    v = qkv[..., 2 * qk_w:].reshape(bsz, seq, GDN_V_HEADS, GDN_DV)
    decay_rate = jnp.exp(a_log.astype(F32))
    g = -decay_rate * jax.nn.softplus(ab[:, :, :, 0] + dt_bias.astype(F32))
    beta = jax.nn.sigmoid(ab[:, :, :, 1])
    o_fwd = _gdn_scan(q, k, v, g[:, :, 0], beta[:, :, 0])
    o_bwd = jnp.flip(_gdn_scan(jnp.flip(q, 1), jnp.flip(k, 1), jnp.flip(v, 1),
                               jnp.flip(g[:, :, 1], 1), jnp.flip(beta[:, :, 1], 1)), 1)
    zg = jax.nn.silu(z.reshape(bsz, seq, GDN_V_HEADS, GDN_DV))
    o = o_fwd + o_bwd
    o = o * lax.rsqrt(jnp.mean(o * o, axis=-1, keepdims=True) + EPS) * out_norm * zg
    return o.reshape(bsz * seq, v_w)


def _s5_direction(u, a_re, a_im, log_dt, b_re, b_im, c_re, c_im):
    bsz, seq, groups = u.shape[:3]
    n = seq // S5_CHUNK
    lam = lax.complex(a_re.astype(F32), a_im.astype(F32))
    dt = jnp.exp(log_dt.astype(F32))[:, None]
    a_bar = jnp.exp(lam * dt)
    b_bar = ((a_bar - 1.0) / lam)[..., None] * lax.complex(b_re.astype(F32), b_im.astype(F32))
    c = lax.complex(c_re.astype(F32), c_im.astype(F32))
    bu = jnp.einsum('gpc,bsgc->bsgp', b_bar, u.astype(jnp.complex64), precision=lax.Precision.HIGHEST)
    bu = jnp.moveaxis(bu.reshape(bsz, n, S5_CHUNK, groups, S5_STATE), 1, 0)
    a_elems = jnp.broadcast_to(a_bar, (bsz, S5_CHUNK, groups, S5_STATE))
    powers = jnp.exp(lam[None] * dt[None] * jnp.arange(1, S5_CHUNK + 1, dtype=F32)[:, None, None])

    def binop(e1, e2):
        return (e2[0] * e1[0], e2[0] * e1[1] + e2[1])

    def step(x_prev, bu_c):
        _, xs = lax.associative_scan(binop, (a_elems, bu_c), axis=1)
        xs = xs + powers[None] * x_prev[:, None]
        y = jnp.einsum('gcp,blgp->blgc', c, xs, precision=lax.Precision.HIGHEST).real
        return xs[:, -1], y

    x0 = jnp.zeros((bsz, groups, S5_STATE), jnp.complex64)
    _, ys = lax.scan(step, x0, bu)
    return jnp.moveaxis(ys, 0, 1).reshape(bsz, seq, groups, S5_GROUP_DIM)


def _s5_core(h, a_re, a_im, log_dt, b_re, b_im, c_re, c_im, d_skip, bsz, seq):
    d = h.shape[-1]
    groups = d // S5_GROUP_DIM
    u = h.reshape(bsz, seq, groups, S5_GROUP_DIM)
    y_f = _s5_direction(u, a_re[0], a_im[0], log_dt[0], b_re[0], b_im[0], c_re[0], c_im[0])
    y_b = jnp.flip(_s5_direction(jnp.flip(u, 1), a_re[1], a_im[1], log_dt[1], b_re[1], b_im[1], c_re[1], c_im[1]), 1)
    y = (y_f + y_b).reshape(bsz * seq, d) + d_skip * h
    return jax.nn.gelu(y)


def _trunk(x, p, w, bsz, seq):
    depth = w['norm_mix'].shape[0]
    bf = lambda t: t.astype(BF16)
    for i in range(depth):
        kind, j = i % N_MIXERS, i // N_MIXERS
        gmix = w['norm_mix'][i]
        if kind == 0:
            qkv = norm_matmul(x, gmix, bf(w['na_w_qkv'][j]), out_dtype=BF16)
            att = na_attention(qkv, w['na_rpb'][j], bsz, seq)
            x = matmul_res(att, bf(w['na_w_o'][j]), x)
        elif kind == 1:
            uv = norm_matmul(x, gmix, bf(w['sg_w_in'][j]), act="gelu", out_dtype=BF16)
            x = sg_out(uv, w['sg_norm'][j], w['sg_w_s'][j], w['sg_b_s'][j], bf(w['sg_w_o'][j]), x)
        elif kind == 2:
            x = gated_deltanet(x, gmix, w['gdn_w_in'][j], w['gdn_conv_w'][j], w['gdn_a_log'][j],
                               w['gdn_dt_bias'][j], w['gdn_out_norm'][j], w['gdn_w_o'][j], bsz, seq)
        else:
            x = s5_mixer(x, gmix, w['s5_a_re'][j], w['s5_a_im'][j], w['s5_log_dt'][j], w['s5_b_re'][j],
                         w['s5_b_im'][j], w['s5_c_re'][j], w['s5_c_im'][j], w['s5_d'][j], w['s5_w_glu'][j], bsz, seq)
        x = ffn(x, w['norm_ffn'][i], bf(w['ffn_w_gu'][i]), w['ffn_conv_w'][i], w['ffn_conv_b'][i],
                bf(w['ffn_w_down'][i]), seq)
        x = ple(x, p[i], w['norm_ple'][i], bf(w['ple_w_gate'][i]), bf(w['ple_w_proj'][i]))
    return x


def kernel(x_prompt, x_sample, p_prompt, p_sample, norm_mix, norm_ffn, norm_ple, final_norm, na_w_qkv, na_w_o, na_rpb, sg_w_in, sg_norm, sg_w_s, sg_b_s, sg_w_o, gdn_w_in, gdn_conv_w, gdn_a_log, gdn_dt_bias, gdn_out_norm, gdn_w_o, s5_a_re, s5_a_im, s5_log_dt, s5_b_re, s5_b_im, s5_c_re, s5_c_im, s5_d, s5_w_glu, ffn_w_gu, ffn_conv_w, ffn_conv_b, ffn_w_down, ple_w_proj, ple_w_gate):
    w = dict(norm_mix=norm_mix, norm_ffn=norm_ffn, norm_ple=norm_ple, final_norm=final_norm,
             na_w_qkv=na_w_qkv, na_w_o=na_w_o, na_rpb=na_rpb,
             sg_w_in=sg_w_in, sg_norm=sg_norm, sg_w_s=sg_w_s, sg_b_s=sg_b_s, sg_w_o=sg_w_o,
             gdn_w_in=gdn_w_in, gdn_conv_w=gdn_conv_w, gdn_a_log=gdn_a_log, gdn_dt_bias=gdn_dt_bias,
             gdn_out_norm=gdn_out_norm, gdn_w_o=gdn_w_o,
             s5_a_re=s5_a_re, s5_a_im=s5_a_im, s5_log_dt=s5_log_dt, s5_b_re=s5_b_re, s5_b_im=s5_b_im,
             s5_c_re=s5_c_re, s5_c_im=s5_c_im, s5_d=s5_d, s5_w_glu=s5_w_glu,
             ffn_w_gu=ffn_w_gu, ffn_conv_w=ffn_conv_w, ffn_conv_b=ffn_conv_b, ffn_w_down=ffn_w_down,
             ple_w_proj=ple_w_proj, ple_w_gate=ple_w_gate)
    b1, seq, d = x_prompt.shape
    b2 = x_sample.shape[0]
    bsz = b1 + b2
    x = jnp.concatenate([x_prompt, x_sample], axis=0).reshape(bsz * seq, d)
    p = jnp.concatenate([p_prompt, p_sample], axis=1).reshape(p_prompt.shape[0], bsz * seq, -1)
    x = _trunk(x, p, w, bsz, seq)
    y1 = rmsnorm(x, final_norm, row0=0, rows=b1 * seq).reshape(b1, seq, d)
    y2 = rmsnorm(x, final_norm, row0=b1 * seq, rows=b2 * seq).reshape(b2, seq, d)
    return (y1, y2)
```

```python
import functools
import math

import jax
import jax.numpy as jnp
import numpy as np
from jax import lax
from jax.experimental import pallas as pl
from jax.experimental.pallas import tpu as pltpu

F32 = jnp.float32
BF16 = jnp.bfloat16

EPS = 1e-6
NEG_INF = -1e30
GRID_W = 64
NA_HEADS = 16
NA_WIN_R = 8
NA_WIN_C = 16
SG_CHUNK = 128
SG_GROUPS = 16
GDN_QK_HEADS = 16
GDN_V_HEADS = 32
GDN_DK = 128
GDN_DV = 128
GDN_CHUNK = 64
S5_GROUP_DIM = 16
S5_STATE = 64
N_MIXERS = 4

VMEM_LIMIT_BYTES = 56 * 1024 * 1024
HALO = 16


def _params(*sem):
    return pltpu.CompilerParams(dimension_semantics=sem, vmem_limit_bytes=VMEM_LIMIT_BYTES)


def _rms(x, g):
    return x * lax.rsqrt(jnp.mean(x * x, axis=-1, keepdims=True) + EPS) * g


def _gelu_tanh(x):
    return 0.5 * x * (1.0 + jnp.tanh(math.sqrt(2.0 / math.pi) * (x + 0.044715 * (x * x * x))))


def _sigmoid(x):
    return 1.0 / (1.0 + jnp.exp(-x))


def _rmsnorm_kernel(x_ref, g_ref, o_ref):
    o_ref[...] = _rms(x_ref[...], g_ref[...]).astype(o_ref.dtype)


def rmsnorm(x, g, out_dtype=F32, tm=512, row0=0, rows=None):
    m, d = x.shape
    rows = m if rows is None else rows
    off = row0 // tm
    return pl.pallas_call(
        _rmsnorm_kernel,
        out_shape=jax.ShapeDtypeStruct((rows, d), out_dtype),
        grid=(rows // tm,),
        in_specs=[pl.BlockSpec((tm, d), lambda i: (i + off, 0)), pl.BlockSpec((1, d), lambda i: (0, 0))],
        out_specs=pl.BlockSpec((tm, d), lambda i: (i, 0)),
        compiler_params=_params("parallel"),
        name="rmsnorm",
    )(x, g.reshape(1, d))


def _norm_matmul_kernel(x_ref, g_ref, w_ref, o_ref, hn_ref, *, act):
    @pl.when(pl.program_id(1) == 0)
    def _():
        hn_ref[...] = _rms(x_ref[...], g_ref[...]).astype(BF16)

    y = jnp.dot(hn_ref[...], w_ref[...], preferred_element_type=F32)
    if act == "gelu":
        y = _gelu_tanh(y)
    o_ref[...] = y.astype(o_ref.dtype)


def norm_matmul(x, g, w, act=None, out_dtype=F32, tm=512, tn=1024):
    m, d = x.shape
    n = w.shape[1]
    tn = next(t for t in (tn, 512, 256, 128) if n % t == 0)
    return pl.pallas_call(
        functools.partial(_norm_matmul_kernel, act=act),
        out_shape=jax.ShapeDtypeStruct((m, n), out_dtype),
        grid=(m // tm, n // tn),
        in_specs=[
            pl.BlockSpec((tm, d), lambda i, j: (i, 0)),
            pl.BlockSpec((1, d), lambda i, j: (0, 0)),
            pl.BlockSpec((d, tn), lambda i, j: (0, j)),
        ],
        out_specs=pl.BlockSpec((tm, tn), lambda i, j: (i, j)),
        scratch_shapes=[pltpu.VMEM((tm, d), BF16)],
        compiler_params=_params("parallel", "arbitrary"),
        name="norm_matmul",
    )(x, g.reshape(1, d), w)


def _matmul_res_kernel(a_ref, w_ref, r_ref, o_ref):
    o_ref[...] = r_ref[...] + jnp.dot(a_ref[...].astype(BF16), w_ref[...], preferred_element_type=F32)


def matmul_res(a, w, res, tm=512, tn=1024):
    m, k = a.shape
    n = w.shape[1]
    return pl.pallas_call(
        _matmul_res_kernel,
        out_shape=jax.ShapeDtypeStruct((m, n), F32),
        grid=(m // tm, n // tn),
        in_specs=[
            pl.BlockSpec((tm, k), lambda i, j: (i, 0)),
            pl.BlockSpec((k, tn), lambda i, j: (0, j)),
            pl.BlockSpec((tm, tn), lambda i, j: (i, j)),
        ],
        out_specs=pl.BlockSpec((tm, tn), lambda i, j: (i, j)),
        compiler_params=_params("parallel", "arbitrary"),
        name="matmul_res",
    )(a, w, res)


def _ffn_kernel(x_ref, xp_ref, xn_ref, g_ref, wg_ref, wu_ref, cw_ref, cb_ref, wd_ref, o_ref, hn_ref, acc_ref,
                *, tm, seq):
    i = pl.program_id(0)
    j = pl.program_id(1)

    @pl.when(j == 0)
    def _():
        g = g_ref[...]
        prev_ok = jnp.where((i * tm) % seq != 0, 1.0, 0.0)
        next_ok = jnp.where(((i + 1) * tm) % seq != 0, 1.0, 0.0)
        hn_ref[0:HALO, :] = (_rms(xp_ref[...], g) * prev_ok).astype(BF16)
        hn_ref[HALO:HALO + tm, :] = _rms(x_ref[...], g).astype(BF16)
        hn_ref[HALO + tm:, :] = (_rms(xn_ref[...], g) * next_ok).astype(BF16)
        acc_ref[...] = jnp.zeros_like(acc_ref)

    rows = tm + 2 * HALO
    gate = jnp.dot(hn_ref[...], wg_ref[...], preferred_element_type=F32)
    up = jnp.dot(hn_ref[HALO:HALO + tm, :], wu_ref[...], preferred_element_type=F32)
    cw = cw_ref[...]
    g_prev = pltpu.roll(gate, 1, 0)[HALO:HALO + tm]
    g_next = pltpu.roll(gate, rows - 1, 0)[HALO:HALO + tm]
    gc = cw[0:1] * g_prev + cw[1:2] * gate[HALO:HALO + tm] + cw[2:3] * g_next + cb_ref[...]
    act = (gc * _sigmoid(gc) * up).astype(BF16)
    acc_ref[...] += jnp.dot(act, wd_ref[...], preferred_element_type=F32)

    @pl.when(j == pl.num_programs(1) - 1)
    def _():
        o_ref[...] = x_ref[...] + acc_ref[...]


def ffn(x, g, w_gu, conv_w, conv_b, w_down, seq, tm=512, tf=512):
    m, d = x.shape
    f = w_down.shape[0]
    nf = f // tf
    hb = tm // HALO
    last = m // HALO - 1
    return pl.pallas_call(
        functools.partial(_ffn_kernel, tm=tm, seq=seq),
        out_shape=jax.ShapeDtypeStruct((m, d), F32),
        grid=(m // tm, nf),
        in_specs=[
            pl.BlockSpec((tm, d), lambda i, j: (i, 0)),
            pl.BlockSpec((HALO, d), lambda i, j: (jnp.maximum(i * hb - 1, 0), 0)),
            pl.BlockSpec((HALO, d), lambda i, j: (jnp.minimum((i + 1) * hb, last), 0)),
            pl.BlockSpec((1, d), lambda i, j: (0, 0)),
            pl.BlockSpec((d, tf), lambda i, j: (0, j)),
            pl.BlockSpec((d, tf), lambda i, j: (0, j + nf)),
            pl.BlockSpec((3, tf), lambda i, j: (0, j)),
            pl.BlockSpec((1, tf), lambda i, j: (0, j)),
            pl.BlockSpec((tf, d), lambda i, j: (j, 0)),
        ],
        out_specs=pl.BlockSpec((tm, d), lambda i, j: (i, 0)),
        scratch_shapes=[pltpu.VMEM((tm + 2 * HALO, d), BF16), pltpu.VMEM((tm, d), F32)],
        compiler_params=_params("parallel", "arbitrary"),
        name="ffn",
    )(x, x, x, g.reshape(1, d), w_gu, w_gu, conv_w, conv_b.reshape(1, f), w_down)


def _ple_kernel(x_ref, p_ref, g_ref, wg_ref, wp_ref, o_ref):
    x = x_ref[...]
    hn = _rms(x, g_ref[...]).astype(BF16)
    gate = _sigmoid(jnp.dot(hn, wg_ref[...], preferred_element_type=F32))
    proj = jnp.dot(p_ref[...].astype(BF16), wp_ref[...], preferred_element_type=F32)
    o_ref[...] = x + gate * proj


def ple(x, p, g, w_gate, w_proj, tm=512):
    m, d = x.shape
    pd = p.shape[1]
    return pl.pallas_call(
        _ple_kernel,
        out_shape=jax.ShapeDtypeStruct((m, d), F32),
        grid=(m // tm,),
        in_specs=[
            pl.BlockSpec((tm, d), lambda i: (i, 0)),
            pl.BlockSpec((tm, pd), lambda i: (i, 0)),
            pl.BlockSpec((1, d), lambda i: (0, 0)),
            pl.BlockSpec((d, d), lambda i: (0, 0)),
            pl.BlockSpec((pd, d), lambda i: (0, 0)),
        ],
        out_specs=pl.BlockSpec((tm, d), lambda i: (i, 0)),
        compiler_params=_params("parallel"),
        name="ple",
    )(x, p, g.reshape(1, d), w_gate, w_proj)


GDN_BLOCK = 256


def _conv_in_kernel(x_ref, xp_ref, xn_ref, g_ref, w_ref, cw_ref, o_ref, hn_ref, *, tm, seq, nq, nqk, scale):
    i = pl.program_id(0)
    j = pl.program_id(1)

    @pl.when(j == 0)
    def _():
        g = g_ref[...]
        prev_ok = jnp.where((i * tm) % seq != 0, 1.0, 0.0)
        next_ok = jnp.where(((i + 1) * tm) % seq != 0, 1.0, 0.0)
        hn_ref[0:HALO, :] = (_rms(xp_ref[...], g) * prev_ok).astype(BF16)
        hn_ref[HALO:HALO + tm, :] = _rms(x_ref[...], g).astype(BF16)
        hn_ref[HALO + tm:, :] = (_rms(xn_ref[...], g) * next_ok).astype(BF16)

    cw = cw_ref[...]
    w = w_ref[...]
    tn = w.shape[1]
    half = tm // 2
    rows = half + 2 * HALO
    for lo in (0, half):
        y = jnp.dot(hn_ref[lo:lo + rows, :], w, preferred_element_type=F32)
        y_prev = pltpu.roll(y, 1, 0)[HALO:HALO + half]
        y_next = pltpu.roll(y, rows - 1, 0)[HALO:HALO + half]
        c = cw[0:1] * y_prev + cw[1:2] * y[HALO:HALO + half] + cw[2:3] * y_next
        c = c * _sigmoid(c)
        for s in range(tn // 128):
            cs = c[:, s * 128:(s + 1) * 128]
            inv = lax.rsqrt(jnp.sum(cs * cs, axis=-1, keepdims=True) + EPS)
            f = jnp.where(j < nq, inv * scale, jnp.where(j < nqk, inv, 1.0))
            o_ref[lo:lo + half, s * 128:(s + 1) * 128] = (cs * f).astype(o_ref.dtype)


def conv_in(x, g, w, conv_w, seq, n_q, n_qk, scale, tm=512, tn=512):
    m, d = x.shape
    n = w.shape[1]
    hb = tm // HALO
    last = m // HALO - 1
    return pl.pallas_call(
        functools.partial(_conv_in_kernel, tm=tm, seq=seq, nq=n_q // tn, nqk=n_qk // tn, scale=scale),
        out_shape=jax.ShapeDtypeStruct((m, n), BF16),
        grid=(m // tm, n // tn),
        in_specs=[
            pl.BlockSpec((tm, d), lambda i, j: (i, 0)),
            pl.BlockSpec((HALO, d), lambda i, j: (jnp.maximum(i * hb - 1, 0), 0)),
            pl.BlockSpec((HALO, d), lambda i, j: (jnp.minimum((i + 1) * hb, last), 0)),
            pl.BlockSpec((1, d), lambda i, j: (0, 0)),
            pl.BlockSpec((d, tn), lambda i, j: (0, j)),
            pl.BlockSpec((3, tn), lambda i, j: (0, j)),
        ],
        out_specs=pl.BlockSpec((tm, tn), lambda i, j: (i, j)),
        scratch_shapes=[pltpu.VMEM((tm + 2 * HALO, d), BF16)],
        compiler_params=_params("parallel", "arbitrary"),
        name="gdn_conv_in",
    )(x, x, x, g.reshape(1, d), w, conv_w)


def _gdn_gates_kernel(x_ref, g_ref, w_ref, alog_ref, bias_ref, isg_ref, o_ref):
    hn = _rms(x_ref[...], g_ref[...]).astype(BF16)
    y = jnp.dot(hn, w_ref[...], preferred_element_type=F32)
    t = y + bias_ref[...]
    softplus = jnp.maximum(t, 0.0) + jnp.log1p(jnp.exp(-jnp.abs(t)))
    isg = isg_ref[...] > 0.5
    base = jnp.where(isg, -jnp.exp(alog_ref[...]) * softplus, 0.0)
    tm, n = base.shape
    lc = GDN_CHUNK
    pos = lax.broadcasted_iota(jnp.int32, (tm, n), 0) & (lc - 1)
    pre, suf = base, base
    k = 1
    while k < lc:
        pre = pre + jnp.where(pos >= k, pltpu.roll(pre, k, 0), 0.0)
        suf = suf + jnp.where(pos + k <= lc - 1, pltpu.roll(suf, tm - k, 0), 0.0)
        k *= 2
    is_bwd = lax.broadcasted_iota(jnp.int32, (1, n), 1) >= n // 2
    o_ref[:, 0:n] = jnp.where(isg, jnp.where(is_bwd, suf, pre), _sigmoid(y))
    o_ref[:, n:2 * n] = pre + suf - base


def gdn_gates(x, g, w_ab, a_log, dt_bias, tm=512):
    m, d = x.shape
    n = w_ab.shape[1]
    nh = a_log.shape[-1]
    zeros = jnp.zeros((2, 1, nh), F32)
    arrange = lambda t: jnp.concatenate([t.reshape(2, 1, nh).astype(F32), zeros], axis=1).reshape(1, n)
    isg = jnp.concatenate([jnp.ones((2, 1, nh), F32), zeros], axis=1).reshape(1, n)
    return pl.pallas_call(
        _gdn_gates_kernel,
        out_shape=jax.ShapeDtypeStruct((m, 2 * n), F32),
        grid=(m // tm,),
        in_specs=[
            pl.BlockSpec((tm, d), lambda i: (i, 0)),
            pl.BlockSpec((1, d), lambda i: (0, 0)),
            pl.BlockSpec((d, n), lambda i: (0, 0)),
            pl.BlockSpec((1, n), lambda i: (0, 0)),
            pl.BlockSpec((1, n), lambda i: (0, 0)),
            pl.BlockSpec((1, n), lambda i: (0, 0)),
        ],
        out_specs=pl.BlockSpec((tm, 2 * n), lambda i: (i, 0)),
        compiler_params=_params("parallel"),
        name="gdn_gates",
    )(x, g.reshape(1, d), w_ab, arrange(a_log), arrange(dt_bias), isg)


def _gdn_chains(q_ref, k_ref, v_ref, gc_ref, gr_ref, o_ref, rev):
    c_sz, lc = GDN_BLOCK, GDN_CHUNK
    d = 1 if rev else 0
    ii = lax.broadcasted_iota(jnp.int32, (c_sz, c_sz), 0)
    jj = lax.broadcasted_iota(jnp.int32, (c_sz, c_sz), 1)
    sh = lc.bit_length() - 1
    same = (ii >> sh) == (jj >> sh)
    incl = jnp.logical_and(same, (jj >= ii) if rev else (jj <= ii))
    eye = ii == jj
    q = q_ref[...]
    k = k_ref[...]
    kf = k.astype(F32)
    qf = q.astype(F32)
    gc = gc_ref[0, 0][:, 8 * d:8 * d + 8]
    gr = gr_ref[0, 0][8 * d:8 * d + 8, :]
    gam_c = gc[:, 0:2]
    gam_r = gr[0:2]
    end_r = gr[4:6]
    gram = lax.dot_general(k, k, (((1,), (1,)), ((), ())), preferred_element_type=F32)
    qk = lax.dot_general(q, k, (((1,), (1,)), ((), ())), preferred_element_type=F32)
    chains = []
    for hs in range(2):
        gcol = gam_c[:, hs:hs + 1]
        grow = gam_r[hs:hs + 1, :]
        bcol = gc[:, 2 + hs:3 + hs]
        dec = jnp.exp(jnp.where(incl, gcol - grow, NEG_INF))
        e_g = jnp.exp(gcol)
        vh = v_ref[:, hs * GDN_DV:(hs + 1) * GDN_DV].astype(F32)
        chains.append(dict(
            n=(-(gram * jnp.where(eye, 0.0, dec)) * bcol).astype(BF16),
            x=jnp.concatenate([kf * (bcol * e_g), vh * bcol], axis=1),
            attn=(qk * dec).astype(BF16),
            qe=qf * e_g,
            kd=kf * jnp.exp(gc[:, 4 + hs:5 + hs] - gcol),
            end=end_r[hs:hs + 1, :],
            o_ref=o_ref, hs=hs, idx=2 * d + hs, rev=rev))
    return chains


def _gdn_scan_kernel(qf_ref, kf_ref, vf_ref, gcf_ref, grf_ref,
                     qb_ref, kb_ref, vb_ref, gcb_ref, grb_ref, of_ref, ob_ref, s_ref):
    @pl.when(pl.program_id(2) == 0)
    def _():
        s_ref[...] = jnp.zeros_like(s_ref)

    c_sz, lc = GDN_BLOCK, GDN_CHUNK
    nchunk = c_sz // lc
    width = GDN_DK + GDN_DV
    chains = (_gdn_chains(qf_ref, kf_ref, vf_ref, gcf_ref, grf_ref, of_ref, False)
              + _gdn_chains(qb_ref, kb_ref, vb_ref, gcb_ref, grb_ref, ob_ref, True))
    nlev = lc.bit_length() - 1
    for lvl in range(nlev):
        for ch in chains:
            nb = ch['n']
            xb = ch['x'].astype(BF16)
            if lvl < nlev - 1:
                r = jnp.dot(nb, jnp.concatenate([xb, nb], axis=1), preferred_element_type=F32)
                ch['x'] = ch['x'] + r[:, :width]
                ch['n'] = r[:, width:].astype(BF16)
            else:
                ch['x'] = ch['x'] + jnp.dot(nb, xb, preferred_element_type=F32)
    irow = lax.broadcasted_iota(jnp.int32, (c_sz, 1), 0) >> nlev
    for ch in chains:
        wub = ch['x'].astype(BF16)
        awu = jnp.dot(ch['attn'], wub, preferred_element_type=F32)
        ch['qeff'] = (ch['qe'] - awu[:, :GDN_DK]).astype(BF16)
        ch['o_in'] = awu[:, GDN_DK:]
        ch['kwu'] = [lax.dot_general(jnp.where(irow == c, ch['kd'], 0.0).astype(BF16), wub,
                                     (((0,), (0,)), ((), ())), preferred_element_type=F32)
                     for c in range(nchunk)]
        ch['s'] = s_ref[ch['idx']]
    for step in range(nchunk):
        for ch in chains:
            c = nchunk - 1 - step if ch['rev'] else step
            r0 = c * lc
            s = ch['s']
            sb = s.astype(BF16)
            o_c = ch['o_in'][r0:r0 + lc] + jnp.dot(ch['qeff'][r0:r0 + lc], sb, preferred_element_type=F32)
            ch['o_ref'][r0:r0 + lc, ch['hs'] * GDN_DV:(ch['hs'] + 1) * GDN_DV] = o_c.astype(ch['o_ref'].dtype)
            kwu = ch['kwu'][c]
            e_end = jnp.exp(ch['end'][:, r0:r0 + 1])
            ch['s'] = (e_end * s - jnp.dot(kwu[:, :GDN_DK].astype(BF16), sb, preferred_element_type=F32)
                       + kwu[:, GDN_DK:])
    for ch in chains:
        s_ref[ch['idx']] = ch['s']


def gdn_scan(qkv, gcol, grow, bsz, seq):
    c_sz = GDN_BLOCK
    nb = seq // c_sz
    hq = GDN_QK_HEADS
    kcol = hq
    vcol = (2 * hq * GDN_DK) // (2 * GDN_DV)
    fwd = lambda b, h, c: c
    bwd = lambda b, h, c: nb - 1 - c

    def specs(pos):
        return [
            pl.BlockSpec((c_sz, GDN_DK), lambda b, h, c: (b * nb + pos(b, h, c), h)),
            pl.BlockSpec((c_sz, GDN_DK), lambda b, h, c: (b * nb + pos(b, h, c), kcol + h)),
            pl.BlockSpec((c_sz, 2 * GDN_DV), lambda b, h, c: (b * nb + pos(b, h, c), vcol + h)),
            pl.BlockSpec((1, 1, c_sz, 16), lambda b, h, c: (b, h, pos(b, h, c), 0)),
            pl.BlockSpec((1, 1, 16, c_sz), lambda b, h, c: (b, h, 0, pos(b, h, c))),
        ]

    out = jax.ShapeDtypeStruct((bsz * seq, GDN_V_HEADS * GDN_DV), BF16)
    return pl.pallas_call(
        _gdn_scan_kernel,
        out_shape=(out, out),
        grid=(bsz, hq, nb),
        in_specs=specs(fwd) + specs(bwd),
        out_specs=(
            pl.BlockSpec((c_sz, 2 * GDN_DV), lambda b, h, c: (b * nb + c, h)),
            pl.BlockSpec((c_sz, 2 * GDN_DV), lambda b, h, c: (b * nb + nb - 1 - c, h)),
        ),
        scratch_shapes=[pltpu.VMEM((4, GDN_DK, GDN_DV), F32)],
        compiler_params=_params("parallel", "parallel", "arbitrary"),
        name="gdn_scan",
    )(qkv, qkv, qkv, gcol, grow, qkv, qkv, qkv, gcol, grow)


def _gdn_out_kernel(of_ref, ob_ref, z_ref, gn_ref, w_ref, r_ref, o_ref):
    @pl.when(pl.program_id(1) == 0)
    def _():
        o_ref[...] = r_ref[...]

    gn = gn_ref[...]
    parts = []
    for h in range(of_ref.shape[1] // GDN_DV):
        sl = slice(h * GDN_DV, (h + 1) * GDN_DV)
        o = of_ref[:, sl].astype(F32) + ob_ref[:, sl].astype(F32)
        z = z_ref[:, sl].astype(F32)
        parts.append((_rms(o, gn) * (z * _sigmoid(z))).astype(BF16))
    o_ref[...] += jnp.dot(jnp.concatenate(parts, axis=1), w_ref[...], preferred_element_type=F32)


def gdn_out(o_f, o_b, z, out_norm, w_o, res, tm=512, tk=1024):
    m, k = o_f.shape
    n = w_o.shape[1]
    return pl.pallas_call(
        _gdn_out_kernel,
        out_shape=jax.ShapeDtypeStruct((m, n), F32),
        grid=(m // tm, k // tk),
        in_specs=[
            pl.BlockSpec((tm, tk), lambda i, j: (i, j)),
            pl.BlockSpec((tm, tk), lambda i, j: (i, j)),
            pl.BlockSpec((tm, tk), lambda i, j: (i, j)),
            pl.BlockSpec((1, GDN_DV), lambda i, j: (0, 0)),
            pl.BlockSpec((tk, n), lambda i, j: (j, 0)),
            pl.BlockSpec((tm, n), lambda i, j: (i, 0)),
        ],
        out_specs=pl.BlockSpec((tm, n), lambda i, j: (i, 0)),
        compiler_params=_params("parallel", "arbitrary"),
        name="gdn_out",
    )(o_f, o_b, z, out_norm.reshape(1, GDN_DV), w_o, res)


def gated_deltanet(x, gmix, w_in, conv_w, a_log, dt_bias, out_norm, w_o, bsz, seq):
    qk_w = GDN_QK_HEADS * GDN_DK
    v_w = GDN_V_HEADS * GDN_DV
    cw = 2 * qk_w + v_w
    w_in = w_in.astype(BF16)
    qkv = conv_in(x, gmix, w_in[:, :cw], conv_w, seq, qk_w, 2 * qk_w, GDN_DK ** -0.5)
    z = norm_matmul(x, gmix, w_in[:, cw:cw + v_w], out_dtype=BF16)
    gb = gdn_gates(x, gmix, w_in[:, cw + v_w:], a_log, dt_bias)
    nab = gb.shape[1] // 2
    ab = gb[:, :nab].reshape(bsz, seq, 2, 2, GDN_QK_HEADS, 2).transpose(0, 4, 1, 2, 3, 5)
    tot = gb[:, nab:].reshape(bsz, seq, 2, 2, GDN_QK_HEADS, 2)[:, :, :, 0].transpose(0, 3, 1, 2, 4)[..., None, :]
    gcol = jnp.concatenate([ab, tot, jnp.zeros_like(tot)], axis=4).reshape(bsz, GDN_QK_HEADS, seq, 16)
    grow = jnp.swapaxes(gcol, 2, 3)
    o_f, o_b = gdn_scan(qkv, gcol, grow, bsz, seq)
    return gdn_out(o_f, o_b, z, out_norm, w_o.astype(BF16), x)


def _na_bias_table(rpb, rows):
    win_r = min(NA_WIN_R, rows)
    cols = np.arange(GRID_W)
    col_start = np.clip(cols - NA_WIN_C // 2, 0, GRID_W - NA_WIN_C)
    col_valid = (cols[None, :] >= col_start[:, None]) & (cols[None, :] < col_start[:, None] + NA_WIN_C)
    dc_idx = np.clip(cols[None, :] - cols[:, None] + NA_WIN_C - 1, 0, 2 * NA_WIN_C - 2)
    bias_c = jnp.where(col_valid, rpb[:, :, dc_idx].astype(F32), NEG_INF)
    dr = np.arange(NA_WIN_R)[:, None] + np.arange(win_r)[None, :]
    tab = bias_c[:, dr]
    return jnp.transpose(tab, (0, 1, 3, 2, 4)).reshape(rpb.shape[0], NA_WIN_R, GRID_W, win_r * GRID_W)


def _na_kernel(q_ref, k_ref, v_ref, b_ref, o_ref, *, rows, win_r, scale):
    wk = win_r * GRID_W

    group = 8 if rows % 8 == 0 else 1

    def body(it, carry):
        rs = [it * group + i for i in range(group)]
        r0s = [jnp.clip(r - win_r // 2, 0, rows - win_r) for r in rs]
        scores = []
        for r, r0 in zip(rs, r0s):
            q = q_ref[pl.ds(pl.multiple_of(r * GRID_W, GRID_W), GRID_W), :]
            kw = k_ref[pl.ds(pl.multiple_of(r0 * GRID_W, GRID_W), wk), :]
            s = lax.dot_general(q, kw, (((1,), (1,)), ((), ())), preferred_element_type=F32) * scale
            scores.append(s + b_ref[0, r0 - r + NA_WIN_R - 1])
        probs, dens = [], []
        for s in scores:
            p = jnp.exp(s - jnp.max(s, axis=-1, keepdims=True))
            dens.append(jnp.sum(p, axis=-1, keepdims=True))
            probs.append(p.astype(BF16))
        for r, r0, p, den in zip(rs, r0s, probs, dens):
            vw = v_ref[pl.ds(pl.multiple_of(r0 * GRID_W, GRID_W), wk), :]
            o = jnp.dot(p, vw, preferred_element_type=F32) / den
            o_ref[pl.ds(pl.multiple_of(r * GRID_W, GRID_W), GRID_W), :] = o.astype(o_ref.dtype)
        return carry

    lax.fori_loop(0, rows // group, body, 0)


def na_attention(qkv, rpb, bsz, seq):
    d = qkv.shape[1] // 3
    dh = d // NA_HEADS
    rows = seq // GRID_W
    win_r = min(NA_WIN_R, rows)
    table = _na_bias_table(rpb, rows)
    blk = lambda off: pl.BlockSpec((seq, dh), lambda b, h: (b, off + h))
    return pl.pallas_call(
        functools.partial(_na_kernel, rows=rows, win_r=win_r, scale=dh ** -0.5),
        out_shape=jax.ShapeDtypeStruct((bsz * seq, d), BF16),
        grid=(bsz, NA_HEADS),
        in_specs=[blk(0), blk(NA_HEADS), blk(2 * NA_HEADS),
                  pl.BlockSpec((1, NA_WIN_R, GRID_W, win_r * GRID_W), lambda b, h: (h, 0, 0, 0))],
        out_specs=blk(0),
        compiler_params=_params("parallel", "parallel"),
        name="na_attention",
    )(qkv, qkv, qkv, table)


def _sg_out_kernel(u_ref, v_ref, gn_ref, ws_ref, bs_ref, w_ref, r_ref, o_ref, a_ref, *, tm):
    @pl.when(pl.program_id(1) == 0)
    def _():
        vn = _rms(v_ref[...].astype(F32), gn_ref[...]).astype(BF16)
        bs = bs_ref[...]
        gd = vn.shape[1] // SG_GROUPS
        for c in range(tm // SG_CHUNK):
            rs = slice(c * SG_CHUNK, (c + 1) * SG_CHUNK)
            for g in range(SG_GROUPS):
                cs = slice(g * gd, (g + 1) * gd)
                mixed = jnp.dot(ws_ref[g], vn[rs, cs], preferred_element_type=F32) + bs[:, g:g + 1]
                a_ref[rs, cs] = (u_ref[rs, cs].astype(F32) * mixed).astype(BF16)

    o_ref[...] = r_ref[...] + jnp.dot(a_ref[...], w_ref[...], preferred_element_type=F32)


def sg_out(uv, sg_norm, w_s, b_s, w_o, res, tm=512, tn=1024):
    m = uv.shape[0]
    width = uv.shape[1] // 2
    n = w_o.shape[1]
    return pl.pallas_call(
        functools.partial(_sg_out_kernel, tm=tm),
        out_shape=jax.ShapeDtypeStruct((m, n), F32),
        grid=(m // tm, n // tn),
        in_specs=[
            pl.BlockSpec((tm, width), lambda i, j: (i, 0)),
            pl.BlockSpec((tm, width), lambda i, j: (i, 1)),
            pl.BlockSpec((1, width), lambda i, j: (0, 0)),
            pl.BlockSpec(w_s.shape, lambda i, j: (0, 0, 0)),
            pl.BlockSpec((SG_CHUNK, SG_GROUPS), lambda i, j: (0, 0)),
            pl.BlockSpec((width, tn), lambda i, j: (0, j)),
            pl.BlockSpec((tm, tn), lambda i, j: (i, j)),
        ],
        out_specs=pl.BlockSpec((tm, tn), lambda i, j: (i, j)),
        scratch_shapes=[pltpu.VMEM((tm, width), BF16)],
        compiler_params=_params("parallel", "arbitrary"),
        name="sg_out",
    )(uv, uv, sg_norm.reshape(1, width), w_s.astype(BF16), b_s.T.astype(F32), w_o, res)


S5_L = 32
S5_W = S5_L * S5_GROUP_DIM


def _cexp(are, aim, dt, e):
    mag = jnp.exp(are * dt * e)
    ang = aim * dt * e
    return mag * jnp.cos(ang), mag * jnp.sin(ang)


def _s5_prep_kernel(arc_ref, aic_ref, arr_ref, air_ref, ldt_ref, btr_ref, bti_ref, ctr_ref, cti_ref,
                    k_ref, bm_ref, cm_ref, ap_ref, *, rev):
    hp = lax.Precision.HIGHEST
    ll, cg, w, p = S5_L, S5_GROUP_DIM, S5_W, S5_STATE
    sh = cg.bit_length() - 1
    dt = jnp.exp(ldt_ref[0])
    arc, aic = arc_ref[0], aic_ref[0]
    arr, air = arr_ref[0], air_ref[0]
    abr, abi = _cexp(arr, air, dt, 1.0)
    nr, ni = abr - 1.0, abi
    den = arr * arr + air * air
    cr, ci = (nr * arr + ni * air) / den, (ni * arr - nr * air) / den
    btr, bti = btr_ref[0], bti_ref[0]
    bbr, bbi = cr * btr - ci * bti, cr * bti + ci * btr
    lane = lax.broadcasted_iota(jnp.int32, (cg, w), 1)
    sel = jnp.where((lane & (cg - 1)) == lax.broadcasted_iota(jnp.int32, (cg, w), 0), 1.0, 0.0)
    cer = jnp.dot(ctr_ref[0], sel, precision=hp, preferred_element_type=F32)
    cei = jnp.dot(cti_ref[0], sel, precision=hp, preferred_element_type=F32)
    tl = (lax.broadcasted_iota(jnp.int32, (p, w), 1) >> sh).astype(F32)

    def cz(e):
        zr, zi = _cexp(arc, aic, dt, e)
        return cer * zr - cei * zi, cer * zi + cei * zr

    czr, czi = cz((ll - 1.0 - tl) if rev else tl)
    r = jnp.dot(bbr, czr, precision=hp, preferred_element_type=F32) - jnp.dot(bbi, czi, precision=hp,
                                                                                preferred_element_type=F32)
    lane_r = lax.broadcasted_iota(jnp.int32, (cg, w), 1)
    for s in range(ll):
        if rev:
            blk = jnp.where(lane_r < cg * (s + 1), pltpu.roll(r, (w - cg * (ll - 1 - s)) % w, 1), 0.0)
        else:
            blk = jnp.where(lane_r >= cg * s, pltpu.roll(r, cg * s, 1), 0.0)
        k_ref[0, s * cg:(s + 1) * cg, :] = blk.astype(k_ref.dtype)
    acr, aci = _cexp(arc, aic, dt, 1.0)
    c1r, c1i = czr * acr - czi * aci, czr * aci + czi * acr
    cm_ref[0, 0:p, :] = c1r.astype(cm_ref.dtype)
    cm_ref[0, p:2 * p, :] = (-c1i).astype(cm_ref.dtype)
    srow = (lax.broadcasted_iota(jnp.int32, (w, p), 0) >> sh).astype(F32)
    zr, zi = _cexp(arr, air, dt, srow if rev else (ll - 1.0 - srow))
    tbr, tbi = jnp.tile(bbr, (ll, 1)), jnp.tile(bbi, (ll, 1))
    bm_ref[0, :, 0:p] = (zr * tbr - zi * tbi).astype(bm_ref.dtype)
    bm_ref[0, :, p:2 * p] = (zr * tbi + zi * tbr).astype(bm_ref.dtype)
    ek = (ll << lax.broadcasted_iota(jnp.int32, (8, p), 0)).astype(F32)
    pr, pi = _cexp(arr, air, dt, ek)
    ap_ref[0, :, 0:p] = pr
    ap_ref[0, :, p:2 * p] = pi


def s5_prep(a_re, a_im, log_dt, b_re, b_im, c_re, c_im, rev):
    g, p = a_re.shape
    cg, w = S5_GROUP_DIM, S5_W
    col = lambda t: t.reshape(g, p, 1).astype(F32)
    row = lambda t: t.reshape(g, 1, p).astype(F32)
    tr = lambda t: jnp.swapaxes(t, 1, 2).astype(F32)
    spec = lambda s: pl.BlockSpec((1,) + s, lambda i: (i, 0, 0))
    return pl.pallas_call(
        functools.partial(_s5_prep_kernel, rev=rev),
        out_shape=(jax.ShapeDtypeStruct((g, w, w), BF16), jax.ShapeDtypeStruct((g, w, 2 * p), BF16),
                   jax.ShapeDtypeStruct((g, 2 * p, w), BF16), jax.ShapeDtypeStruct((g, 8, 2 * p), F32)),
        grid=(g,),
        in_specs=[spec((p, 1)), spec((p, 1)), spec((1, p)), spec((1, p)), spec((1, 1)),
                  spec((cg, p)), spec((cg, p)), spec((p, cg)), spec((p, cg))],
        out_specs=(spec((w, w)), spec((w, 2 * p)), spec((2 * p, w)), spec((8, 2 * p))),
        compiler_params=_params("parallel"),
        name="s5_prep",
    )(col(a_re), col(a_im), row(a_re), row(a_im), log_dt.reshape(g, 1, 1).astype(F32),
      tr(b_re), tr(b_im), tr(c_re), tr(c_im))


def _s5_chunk_scan(s, ap, nchunk, rev):
    n, w2 = s.shape
    p = w2 // 2
    m = lax.broadcasted_iota(jnp.int32, (n, w2), 0) & (nchunk - 1)
    lane = lax.broadcasted_iota(jnp.int32, (1, w2), 1)

    def shift(x, k):
        if rev:
            return jnp.where(m + k <= nchunk - 1, pltpu.roll(x, n - k, 0), 0.0)
        return jnp.where(m >= k, pltpu.roll(x, k, 0), 0.0)

    x = s
    k, lvl = 1, 0
    while k < nchunk:
        a = ap[lvl:lvl + 1, :]
        a1 = jnp.where(lane < p, a, pltpu.roll(a, p, 1))
        a2 = jnp.where(lane < p, -pltpu.roll(a, p, 1), a)
        xs = shift(x, k)
        x = x + a1 * xs + a2 * pltpu.roll(xs, p, 1)
        k, lvl = 2 * k, lvl + 1
    return shift(x, 1)


def _s5_main_kernel(u_ref, kf_ref, kb_ref, bf_ref, bb_ref, cf_ref, cb_ref, af_ref, ab_ref, y_ref, *, nchunk):
    u = u_ref[0]
    y = jnp.dot(u, kf_ref[0], preferred_element_type=F32) + jnp.dot(u, kb_ref[0], preferred_element_type=F32)
    for bm_ref, cm_ref, ap_ref, rev in ((bf_ref, cf_ref, af_ref, False), (bb_ref, cb_ref, ab_ref, True)):
        s = jnp.dot(u, bm_ref[0], preferred_element_type=F32)
        xin = _s5_chunk_scan(s, ap_ref[0], nchunk, rev)
        y = y + jnp.dot(xin.astype(BF16), cm_ref[0], preferred_element_type=F32)
    y_ref[0] = y.astype(y_ref.dtype)


def s5_main(u, prep_f, prep_b, nchunk):
    g, n, w = u.shape
    p2 = 2 * S5_STATE
    spec = lambda s: pl.BlockSpec((1,) + s, lambda i: (i, 0, 0))
    kf, bf, cf, af = prep_f
    kb, bb, cb, ab = prep_b
    return pl.pallas_call(
        functools.partial(_s5_main_kernel, nchunk=nchunk),
        out_shape=jax.ShapeDtypeStruct((g, n, w), BF16),
        grid=(g,),
        in_specs=[spec((n, w)), spec((w, w)), spec((w, w)), spec((w, p2)), spec((w, p2)),
                  spec((p2, w)), spec((p2, w)), spec((8, p2)), spec((8, p2))],
        out_specs=spec((n, w)),
        compiler_params=_params("parallel"),
        name="s5_main",
    )(u, kf, kb, bf, bb, cf, cb, af, ab)


def _s5_out_kernel(x_ref, g_ref, y_ref, d_ref, wa_ref, wb_ref, o_ref, inv_ref, acca_ref, accb_ref):
    k = pl.program_id(1)
    tk = y_ref.shape[1]

    @pl.when(k == 0)
    def _():
        x = x_ref[...]
        inv_ref[...] = jnp.broadcast_to(lax.rsqrt(jnp.mean(x * x, axis=-1, keepdims=True) + EPS), inv_ref.shape)
        acca_ref[...] = jnp.zeros_like(acca_ref)
        accb_ref[...] = jnp.zeros_like(accb_ref)

    xk = x_ref[:, pl.ds(pl.multiple_of(k * tk, tk), tk)]
    h = xk * inv_ref[:, 0:1] * g_ref[...]
    a = _gelu_tanh(y_ref[...].astype(F32) + d_ref[...] * h).astype(BF16)
    acca_ref[...] += jnp.dot(a, wa_ref[...], preferred_element_type=F32)
    accb_ref[...] += jnp.dot(a, wb_ref[...], preferred_element_type=F32)

    @pl.when(k == pl.num_programs(1) - 1)
    def _():
        o_ref[...] = x_ref[...] + acca_ref[...] * _sigmoid(accb_ref[...])


def s5_out(x, g, y, d_skip, w_glu, tm=512, tk=512):
    m, d = x.shape
    n = w_glu.shape[1] // 2
    return pl.pallas_call(
        _s5_out_kernel,
        out_shape=jax.ShapeDtypeStruct((m, n), F32),
        grid=(m // tm, d // tk),
        in_specs=[
            pl.BlockSpec((tm, d), lambda i, k: (i, 0)),
            pl.BlockSpec((1, tk), lambda i, k: (0, k)),
            pl.BlockSpec((tm, tk), lambda i, k: (i, k)),
            pl.BlockSpec((1, tk), lambda i, k: (0, k)),
            pl.BlockSpec((tk, n), lambda i, k: (k, 0)),
            pl.BlockSpec((tk, n), lambda i, k: (k, 1)),
        ],
        out_specs=pl.BlockSpec((tm, n), lambda i, k: (i, 0)),
        scratch_shapes=[pltpu.VMEM((tm, 128), F32), pltpu.VMEM((tm, n), F32), pltpu.VMEM((tm, n), F32)],
        compiler_params=_params("parallel", "arbitrary"),
        name="s5_out",
    )(x, g.reshape(1, d), y, d_skip.reshape(1, d), w_glu, w_glu)


def s5_mixer(x, gmix, a_re, a_im, log_dt, b_re, b_im, c_re, c_im, d_skip, w_glu, bsz, seq):
    m, d = x.shape
    groups = d // S5_GROUP_DIM
    nchunk = seq // S5_L
    h = rmsnorm(x, gmix, out_dtype=BF16)
    u = h.reshape(m // S5_L, S5_L, groups, S5_GROUP_DIM).transpose(2, 0, 1, 3).reshape(groups, m // S5_L, S5_W)
    prep_f = s5_prep(a_re[0], a_im[0], log_dt[0], b_re[0], b_im[0], c_re[0], c_im[0], rev=False)
    prep_b = s5_prep(a_re[1], a_im[1], log_dt[1], b_re[1], b_im[1], c_re[1], c_im[1], rev=True)
    y = s5_main(u, prep_f, prep_b, nchunk)
    y = y.reshape(groups, m // S5_L, S5_L, S5_GROUP_DIM).transpose(1, 2, 0, 3).reshape(m, d)
    return s5_out(x, gmix, y, d_skip, w_glu.astype(BF16))


def _trunk(x, p, w, bsz, seq):
    depth = w['norm_mix'].shape[0]
    bf = lambda t: t.astype(BF16)
    for i in range(depth):
        kind, j = i % N_MIXERS, i // N_MIXERS
        gmix = w['norm_mix'][i]
        if kind == 0:
            qkv = norm_matmul(x, gmix, bf(w['na_w_qkv'][j]), out_dtype=BF16)
            att = na_attention(qkv, w['na_rpb'][j], bsz, seq)
            x = matmul_res(att, bf(w['na_w_o'][j]), x)
        elif kind == 1:
            uv = norm_matmul(x, gmix, bf(w['sg_w_in'][j]), act="gelu", out_dtype=BF16)
            x = sg_out(uv, w['sg_norm'][j], w['sg_w_s'][j], w['sg_b_s'][j], bf(w['sg_w_o'][j]), x)
        elif kind == 2:
            x = gated_deltanet(x, gmix, w['gdn_w_in'][j], w['gdn_conv_w'][j], w['gdn_a_log'][j],
                               w['gdn_dt_bias'][j], w['gdn_out_norm'][j], w['gdn_w_o'][j], bsz, seq)
        else:
            x = s5_mixer(x, gmix, w['s5_a_re'][j], w['s5_a_im'][j], w['s5_log_dt'][j], w['s5_b_re'][j],
                         w['s5_b_im'][j], w['s5_c_re'][j], w['s5_c_im'][j], w['s5_d'][j], w['s5_w_glu'][j], bsz, seq)
        x = ffn(x, w['norm_ffn'][i], bf(w['ffn_w_gu'][i]), w['ffn_conv_w'][i], w['ffn_conv_b'][i],
                bf(w['ffn_w_down'][i]), seq)
        x = ple(x, p[i], w['norm_ple'][i], bf(w['ple_w_gate'][i]), bf(w['ple_w_proj'][i]))
    return x


def kernel(x_prompt, x_sample, p_prompt, p_sample, norm_mix, norm_ffn, norm_ple, final_norm, na_w_qkv, na_w_o, na_rpb, sg_w_in, sg_norm, sg_w_s, sg_b_s, sg_w_o, gdn_w_in, gdn_conv_w, gdn_a_log, gdn_dt_bias, gdn_out_norm, gdn_w_o, s5_a_re, s5_a_im, s5_log_dt, s5_b_re, s5_b_im, s5_c_re, s5_c_im, s5_d, s5_w_glu, ffn_w_gu, ffn_conv_w, ffn_conv_b, ffn_w_down, ple_w_proj, ple_w_gate):
    w = dict(norm_mix=norm_mix, norm_ffn=norm_ffn, norm_ple=norm_ple, final_norm=final_norm,
             na_w_qkv=na_w_qkv, na_w_o=na_w_o, na_rpb=na_rpb,
             sg_w_in=sg_w_in, sg_norm=sg_norm, sg_w_s=sg_w_s, sg_b_s=sg_b_s, sg_w_o=sg_w_o,
             gdn_w_in=gdn_w_in, gdn_conv_w=gdn_conv_w, gdn_a_log=gdn_a_log, gdn_dt_bias=gdn_dt_bias,
             gdn_out_norm=gdn_out_norm, gdn_w_o=gdn_w_o,
             s5_a_re=s5_a_re, s5_a_im=s5_a_im, s5_log_dt=s5_log_dt, s5_b_re=s5_b_re, s5_b_im=s5_b_im,
             s5_c_re=s5_c_re, s5_c_im=s5_c_im, s5_d=s5_d, s5_w_glu=s5_w_glu,
             ffn_w_gu=ffn_w_gu, ffn_conv_w=ffn_conv_w, ffn_conv_b=ffn_conv_b, ffn_w_down=ffn_w_down,
             ple_w_proj=ple_w_proj, ple_w_gate=ple_w_gate)
    b1, seq, d = x_prompt.shape
    b2 = x_sample.shape[0]
    bsz = b1 + b2
    x = jnp.concatenate([x_prompt, x_sample], axis=0).reshape(bsz * seq, d)
    p = jnp.concatenate([p_prompt, p_sample], axis=1).reshape(p_prompt.shape[0], bsz * seq, -1)
    x = _trunk(x, p, w, bsz, seq)
    y1 = rmsnorm(x, final_norm, row0=0, rows=b1 * seq).reshape(b1, seq, d)
    y2 = rmsnorm(x, final_norm, row0=b1 * seq, rows=b2 * seq).reshape(b2, seq, d)
    return (y1, y2)
```

```python
import functools
import math

import jax
import jax.numpy as jnp
import numpy as np
from jax import lax
from jax.experimental import pallas as pl
from jax.experimental.pallas import tpu as pltpu

F32 = jnp.float32
BF16 = jnp.bfloat16

EPS = 1e-6
NEG_INF = -1e30
GRID_W = 64
NA_HEADS = 16
NA_WIN_R = 8
NA_WIN_C = 16
SG_CHUNK = 128
SG_GROUPS = 16
GDN_QK_HEADS = 16
GDN_V_HEADS = 32
GDN_DK = 128
GDN_DV = 128
GDN_CHUNK = 64
S5_GROUP_DIM = 16
S5_STATE = 64
N_MIXERS = 4

VMEM_LIMIT_BYTES = 56 * 1024 * 1024
HALO = 16


def _params(*sem):
    return pltpu.CompilerParams(dimension_semantics=sem, vmem_limit_bytes=VMEM_LIMIT_BYTES)


def _rms(x, g):
    return x * lax.rsqrt(jnp.mean(x * x, axis=-1, keepdims=True) + EPS) * g


def _gelu_tanh(x):
    return 0.5 * x * (1.0 + jnp.tanh(math.sqrt(2.0 / math.pi) * (x + 0.044715 * (x * x * x))))


def _sigmoid(x):
    return 1.0 / (1.0 + jnp.exp(-x))


def _rmsnorm_kernel(x_ref, g_ref, o_ref):
    o_ref[...] = _rms(x_ref[...], g_ref[...]).astype(o_ref.dtype)


def rmsnorm(x, g, out_dtype=F32, tm=512, row0=0, rows=None):
    m, d = x.shape
    rows = m if rows is None else rows
    off = row0 // tm
    return pl.pallas_call(
        _rmsnorm_kernel,
        out_shape=jax.ShapeDtypeStruct((rows, d), out_dtype),
        grid=(rows // tm,),
        in_specs=[pl.BlockSpec((tm, d), lambda i: (i + off, 0)), pl.BlockSpec((1, d), lambda i: (0, 0))],
        out_specs=pl.BlockSpec((tm, d), lambda i: (i, 0)),
        compiler_params=_params("parallel"),
        name="rmsnorm",
    )(x, g.reshape(1, d))


def _as_pair(x, tm):
    if isinstance(x, tuple):
        a, b = x
        return a, b, a.shape[0] // tm, a.shape[0] + b.shape[0]
    return x, x, x.shape[0] // tm, x.shape[0]


def _pair_specs(block, n1, col):
    return [pl.BlockSpec(block, lambda i, j: (jnp.minimum(i, n1 - 1), col(i, j))),
            pl.BlockSpec(block, lambda i, j: (jnp.maximum(i - n1, 0), col(i, j)))]


def _pick(a_ref, b_ref, n1):
    return jnp.where(pl.program_id(0) < n1, a_ref[...], b_ref[...])


def _norm_matmul_kernel(xa_ref, xb_ref, g_ref, w_ref, o_ref, hn_ref, *, act, n1):
    @pl.when(pl.program_id(1) == 0)
    def _():
        hn_ref[...] = _rms(_pick(xa_ref, xb_ref, n1), g_ref[...]).astype(BF16)

    y = jnp.dot(hn_ref[...], w_ref[...], preferred_element_type=F32)
    if act == "gelu":
        y = _gelu_tanh(y)
    o_ref[...] = y.astype(o_ref.dtype)


def norm_matmul(x, g, w, act=None, out_dtype=F32, tm=512, tn=1024, col0=0, ncols=None):
    xa, xb, n1, m = _as_pair(x, tm)
    d = xa.shape[1]
    n = w.shape[1] - col0 if ncols is None else ncols
    tn = next(t for t in (tn, 512, 256, 128) if n % t == 0 and col0 % t == 0)
    cb = col0 // tn
    return pl.pallas_call(
        functools.partial(_norm_matmul_kernel, act=act, n1=n1),
        out_shape=jax.ShapeDtypeStruct((m, n), out_dtype),
        grid=(m // tm, n // tn),
        in_specs=_pair_specs((tm, d), n1, lambda i, j: 0) + [
            pl.BlockSpec((1, d), lambda i, j: (0, 0)),
            pl.BlockSpec((d, tn), lambda i, j: (0, cb + j)),
        ],
        out_specs=pl.BlockSpec((tm, tn), lambda i, j: (i, j)),
        scratch_shapes=[pltpu.VMEM((tm, d), BF16)],
        compiler_params=_params("parallel", "arbitrary"),
        name="norm_matmul",
    )(xa, xb, g.reshape(1, d), w)


def _matmul_res_kernel(a_ref, w_ref, ra_ref, rb_ref, o_ref, *, n1):
    o_ref[...] = _pick(ra_ref, rb_ref, n1) + jnp.dot(a_ref[...].astype(BF16), w_ref[...],
                                                     preferred_element_type=F32)


def matmul_res(a, w, res, tm=512, tn=1024):
    m, k = a.shape
    n = w.shape[1]
    ra, rb, n1, _ = _as_pair(res, tm)
    return pl.pallas_call(
        functools.partial(_matmul_res_kernel, n1=n1),
        out_shape=jax.ShapeDtypeStruct((m, n), F32),
        grid=(m // tm, n // tn),
        in_specs=[
            pl.BlockSpec((tm, k), lambda i, j: (i, 0)),
            pl.BlockSpec((k, tn), lambda i, j: (0, j)),
        ] + _pair_specs((tm, tn), n1, lambda i, j: j),
        out_specs=pl.BlockSpec((tm, tn), lambda i, j: (i, j)),
        compiler_params=_params("parallel", "arbitrary"),
        name="matmul_res",
    )(a, w, ra, rb)


def _ffn_kernel(x_ref, xp_ref, xn_ref, g_ref, wg_ref, wu_ref, cw_ref, cb_ref, wd_ref, o_ref, hn_ref, acc_ref,
                *, tm, seq):
    i = pl.program_id(0)
    j = pl.program_id(1)

    @pl.when(j == 0)
    def _():
        g = g_ref[...]
        prev_ok = jnp.where((i * tm) % seq != 0, 1.0, 0.0)
        next_ok = jnp.where(((i + 1) * tm) % seq != 0, 1.0, 0.0)
        hn_ref[0:HALO, :] = (_rms(xp_ref[...], g) * prev_ok).astype(BF16)
        hn_ref[HALO:HALO + tm, :] = _rms(x_ref[...], g).astype(BF16)
        hn_ref[HALO + tm:, :] = (_rms(xn_ref[...], g) * next_ok).astype(BF16)
        acc_ref[...] = jnp.zeros_like(acc_ref)

    rows = tm + 2 * HALO
    gate = jnp.dot(hn_ref[...], wg_ref[...], preferred_element_type=F32)
    up = jnp.dot(hn_ref[HALO:HALO + tm, :], wu_ref[...], preferred_element_type=F32)
    cw = cw_ref[...]
    g_prev = pltpu.roll(gate, 1, 0)[HALO:HALO + tm]
    g_next = pltpu.roll(gate, rows - 1, 0)[HALO:HALO + tm]
    gc = cw[0:1] * g_prev + cw[1:2] * gate[HALO:HALO + tm] + cw[2:3] * g_next + cb_ref[...]
    act = (gc * _sigmoid(gc) * up).astype(BF16)
    acc_ref[...] += jnp.dot(act, wd_ref[...], preferred_element_type=F32)

    @pl.when(j == pl.num_programs(1) - 1)
    def _():
        o_ref[...] = x_ref[...] + acc_ref[...]


def ffn(x, g, w_gu, conv_w, conv_b, w_down, seq, tm=512, tf=512):
    m, d = x.shape
    f = w_down.shape[0]
    nf = f // tf
    hb = tm // HALO
    last = m // HALO - 1
    return pl.pallas_call(
        functools.partial(_ffn_kernel, tm=tm, seq=seq),
        out_shape=jax.ShapeDtypeStruct((m, d), F32),
        grid=(m // tm, nf),
        in_specs=[
            pl.BlockSpec((tm, d), lambda i, j: (i, 0)),
            pl.BlockSpec((HALO, d), lambda i, j: (jnp.maximum(i * hb - 1, 0), 0)),
            pl.BlockSpec((HALO, d), lambda i, j: (jnp.minimum((i + 1) * hb, last), 0)),
            pl.BlockSpec((1, d), lambda i, j: (0, 0)),
            pl.BlockSpec((d, tf), lambda i, j: (0, j)),
            pl.BlockSpec((d, tf), lambda i, j: (0, j + nf)),
            pl.BlockSpec((3, tf), lambda i, j: (0, j)),
            pl.BlockSpec((1, tf), lambda i, j: (0, j)),
            pl.BlockSpec((tf, d), lambda i, j: (j, 0)),
        ],
        out_specs=pl.BlockSpec((tm, d), lambda i, j: (i, 0)),
        scratch_shapes=[pltpu.VMEM((tm + 2 * HALO, d), BF16), pltpu.VMEM((tm, d), F32)],
        compiler_params=_params("parallel", "arbitrary"),
        name="ffn",
    )(x, x, x, g.reshape(1, d), w_gu, w_gu, conv_w, conv_b.reshape(1, f), w_down)


def _ple_kernel(x_ref, pa_ref, pb_ref, g_ref, wg_ref, wp_ref, o_ref, *, n1):
    x = x_ref[...]
    hn = _rms(x, g_ref[...]).astype(BF16)
    gate = _sigmoid(jnp.dot(hn, wg_ref[...], preferred_element_type=F32))
    p = jnp.where(pl.program_id(0) < n1, pa_ref[0], pb_ref[0])
    proj = jnp.dot(p.astype(BF16), wp_ref[...], preferred_element_type=F32)
    o_ref[...] = x + gate * proj


def ple(x, p, layer, g, w_gate, w_proj, tm=512):
    m, d = x.shape
    pa, pb = p if isinstance(p, tuple) else (p, p)
    n1 = pa.shape[1] // tm
    pd = pa.shape[2]
    return pl.pallas_call(
        functools.partial(_ple_kernel, n1=n1),
        out_shape=jax.ShapeDtypeStruct((m, d), F32),
        grid=(m // tm,),
        in_specs=[
            pl.BlockSpec((tm, d), lambda i: (i, 0)),
            pl.BlockSpec((1, tm, pd), lambda i: (layer, jnp.minimum(i, n1 - 1), 0)),
            pl.BlockSpec((1, tm, pd), lambda i: (layer, jnp.maximum(i - n1, 0), 0)),
            pl.BlockSpec((1, d), lambda i: (0, 0)),
            pl.BlockSpec((d, d), lambda i: (0, 0)),
            pl.BlockSpec((pd, d), lambda i: (0, 0)),
        ],
        out_specs=pl.BlockSpec((tm, d), lambda i: (i, 0)),
        compiler_params=_params("parallel"),
        name="ple",
    )(x, pa, pb, g.reshape(1, d), w_gate, w_proj)


GDN_BLOCK = 256


def _conv_in_kernel(x_ref, xp_ref, xn_ref, g_ref, w_ref, cw_ref, o_ref, hn_ref, *, tm, seq, nq, nqk, scale):
    i = pl.program_id(0)
    j = pl.program_id(1)

    @pl.when(j == 0)
    def _():
        g = g_ref[...]
        prev_ok = jnp.where((i * tm) % seq != 0, 1.0, 0.0)
        next_ok = jnp.where(((i + 1) * tm) % seq != 0, 1.0, 0.0)
        hn_ref[0:HALO, :] = (_rms(xp_ref[...], g) * prev_ok).astype(BF16)
        hn_ref[HALO:HALO + tm, :] = _rms(x_ref[...], g).astype(BF16)
        hn_ref[HALO + tm:, :] = (_rms(xn_ref[...], g) * next_ok).astype(BF16)

    cw = cw_ref[...]
    w = w_ref[...]
    tn = w.shape[1]
    half = tm // 2
    rows = half + 2 * HALO
    for lo in (0, half):
        y = jnp.dot(hn_ref[lo:lo + rows, :], w, preferred_element_type=F32)
        y_prev = pltpu.roll(y, 1, 0)[HALO:HALO + half]
        y_next = pltpu.roll(y, rows - 1, 0)[HALO:HALO + half]
        c = cw[0:1] * y_prev + cw[1:2] * y[HALO:HALO + half] + cw[2:3] * y_next
        c = c * _sigmoid(c)
        for s in range(tn // 128):
            cs = c[:, s * 128:(s + 1) * 128]
            inv = lax.rsqrt(jnp.sum(cs * cs, axis=-1, keepdims=True) + EPS)
            f = jnp.where(j < nq, inv * scale, jnp.where(j < nqk, inv, 1.0))
            o_ref[lo:lo + half, s * 128:(s + 1) * 128] = (cs * f).astype(o_ref.dtype)


def conv_in(x, g, w, n, conv_w, seq, n_q, n_qk, scale, tm=512, tn=512):
    m, d = x.shape
    hb = tm // HALO
    last = m // HALO - 1
    return pl.pallas_call(
        functools.partial(_conv_in_kernel, tm=tm, seq=seq, nq=n_q // tn, nqk=n_qk // tn, scale=scale),
        out_shape=jax.ShapeDtypeStruct((m, n), BF16),
        grid=(m // tm, n // tn),
        in_specs=[
            pl.BlockSpec((tm, d), lambda i, j: (i, 0)),
            pl.BlockSpec((HALO, d), lambda i, j: (jnp.maximum(i * hb - 1, 0), 0)),
            pl.BlockSpec((HALO, d), lambda i, j: (jnp.minimum((i + 1) * hb, last), 0)),
            pl.BlockSpec((1, d), lambda i, j: (0, 0)),
            pl.BlockSpec((d, tn), lambda i, j: (0, j)),
            pl.BlockSpec((3, tn), lambda i, j: (0, j)),
        ],
        out_specs=pl.BlockSpec((tm, tn), lambda i, j: (i, j)),
        scratch_shapes=[pltpu.VMEM((tm + 2 * HALO, d), BF16)],
        compiler_params=_params("parallel", "arbitrary"),
        name="gdn_conv_in",
    )(x, x, x, g.reshape(1, d), w, conv_w)


def _gdn_gates_kernel(x_ref, g_ref, w_ref, alog_ref, bias_ref, isg_ref, o_ref):
    hn = _rms(x_ref[...], g_ref[...]).astype(BF16)
    y = jnp.dot(hn, w_ref[...], preferred_element_type=F32)
    t = y + bias_ref[...]
    softplus = jnp.maximum(t, 0.0) + jnp.log1p(jnp.exp(-jnp.abs(t)))
    isg = isg_ref[...] > 0.5
    base = jnp.where(isg, -jnp.exp(alog_ref[...]) * softplus, 0.0)
    tm, n = base.shape
    lc = GDN_CHUNK
    pos = lax.broadcasted_iota(jnp.int32, (tm, n), 0) & (lc - 1)
    pre, suf = base, base
    k = 1
    while k < lc:
        pre = pre + jnp.where(pos >= k, pltpu.roll(pre, k, 0), 0.0)
        suf = suf + jnp.where(pos + k <= lc - 1, pltpu.roll(suf, tm - k, 0), 0.0)
        k *= 2
    is_bwd = lax.broadcasted_iota(jnp.int32, (1, n), 1) >= n // 2
    o_ref[:, 0:n] = jnp.where(isg, jnp.where(is_bwd, suf, pre), _sigmoid(y))
    o_ref[:, n:2 * n] = pre + suf - base


def gdn_gates(x, g, w, col0, a_log, dt_bias, tm=512):
    m, d = x.shape
    n = w.shape[1] - col0
    nh = a_log.shape[-1]
    zeros = jnp.zeros((2, 1, nh), F32)
    arrange = lambda t: jnp.concatenate([t.reshape(2, 1, nh).astype(F32), zeros], axis=1).reshape(1, n)
    isg = jnp.concatenate([jnp.ones((2, 1, nh), F32), zeros], axis=1).reshape(1, n)
    return pl.pallas_call(
        _gdn_gates_kernel,
        out_shape=jax.ShapeDtypeStruct((m, 2 * n), F32),
        grid=(m // tm,),
        in_specs=[
            pl.BlockSpec((tm, d), lambda i: (i, 0)),
            pl.BlockSpec((1, d), lambda i: (0, 0)),
            pl.BlockSpec((d, n), lambda i: (0, col0 // n)),
            pl.BlockSpec((1, n), lambda i: (0, 0)),
            pl.BlockSpec((1, n), lambda i: (0, 0)),
            pl.BlockSpec((1, n), lambda i: (0, 0)),
        ],
        out_specs=pl.BlockSpec((tm, 2 * n), lambda i: (i, 0)),
        compiler_params=_params("parallel"),
        name="gdn_gates",
    )(x, g.reshape(1, d), w, arrange(a_log), arrange(dt_bias), isg)


def _gdn_chains(q_ref, k_ref, v_ref, gc_ref, gr_ref, o_ref, rev):
    c_sz, lc = GDN_BLOCK, GDN_CHUNK
    d = 1 if rev else 0
    ii = lax.broadcasted_iota(jnp.int32, (c_sz, c_sz), 0)
    jj = lax.broadcasted_iota(jnp.int32, (c_sz, c_sz), 1)
    sh = lc.bit_length() - 1
    same = (ii >> sh) == (jj >> sh)
    incl = jnp.logical_and(same, (jj >= ii) if rev else (jj <= ii))
    eye = ii == jj
    q = q_ref[...]
    k = k_ref[...]
    kf = k.astype(F32)
    qf = q.astype(F32)
    gc = gc_ref[0, 0][:, 8 * d:8 * d + 8]
    gr = gr_ref[0, 0][8 * d:8 * d + 8, :]
    gam_c = gc[:, 0:2]
    gam_r = gr[0:2]
    end_r = gr[4:6]
    gram = lax.dot_general(k, k, (((1,), (1,)), ((), ())), preferred_element_type=F32)
    qk = lax.dot_general(q, k, (((1,), (1,)), ((), ())), preferred_element_type=F32)
    chains = []
    for hs in range(2):
        gcol = gam_c[:, hs:hs + 1]
        grow = gam_r[hs:hs + 1, :]
        bcol = gc[:, 2 + hs:3 + hs]
        dec = jnp.exp(jnp.where(incl, gcol - grow, NEG_INF))
        e_g = jnp.exp(gcol)
        vh = v_ref[:, hs * GDN_DV:(hs + 1) * GDN_DV].astype(F32)
        chains.append(dict(
            n=(-(gram * jnp.where(eye, 0.0, dec)) * bcol).astype(BF16),
            x=jnp.concatenate([kf * (bcol * e_g), vh * bcol], axis=1),
            attn=(qk * dec).astype(BF16),
            qe=qf * e_g,
            kd=kf * jnp.exp(gc[:, 4 + hs:5 + hs] - gcol),
            end=end_r[hs:hs + 1, :],
            o_ref=o_ref, hs=hs, idx=2 * d + hs, rev=rev))
    return chains


def _gdn_scan_kernel(qf_ref, kf_ref, vf_ref, gcf_ref, grf_ref,
                     qb_ref, kb_ref, vb_ref, gcb_ref, grb_ref, of_ref, ob_ref, s_ref):
    @pl.when(pl.program_id(2) == 0)
    def _():
        s_ref[...] = jnp.zeros_like(s_ref)

    c_sz, lc = GDN_BLOCK, GDN_CHUNK
    nchunk = c_sz // lc
    width = GDN_DK + GDN_DV
    chains = (_gdn_chains(qf_ref, kf_ref, vf_ref, gcf_ref, grf_ref, of_ref, False)
              + _gdn_chains(qb_ref, kb_ref, vb_ref, gcb_ref, grb_ref, ob_ref, True))
    nlev = lc.bit_length() - 1
    for lvl in range(nlev):
        for ch in chains:
            nb = ch['n']
            xb = ch['x'].astype(BF16)
            if lvl < nlev - 1:
                r = jnp.dot(nb, jnp.concatenate([xb, nb], axis=1), preferred_element_type=F32)
                ch['x'] = ch['x'] + r[:, :width]
                ch['n'] = r[:, width:].astype(BF16)
            else:
                ch['x'] = ch['x'] + jnp.dot(nb, xb, preferred_element_type=F32)
    irow = lax.broadcasted_iota(jnp.int32, (c_sz, 1), 0) >> nlev
    for ch in chains:
        wub = ch['x'].astype(BF16)
        awu = jnp.dot(ch['attn'], wub, preferred_element_type=F32)
        ch['qeff'] = (ch['qe'] - awu[:, :GDN_DK]).astype(BF16)
        ch['o_in'] = awu[:, GDN_DK:]
        ch['kwu'] = [lax.dot_general(jnp.where(irow == c, ch['kd'], 0.0).astype(BF16), wub,
                                     (((0,), (0,)), ((), ())), preferred_element_type=F32)
                     for c in range(nchunk)]
        ch['s'] = s_ref[ch['idx']]
    for step in range(nchunk):
        for ch in chains:
            c = nchunk - 1 - step if ch['rev'] else step
            r0 = c * lc
            s = ch['s']
            sb = s.astype(BF16)
            o_c = ch['o_in'][r0:r0 + lc] + jnp.dot(ch['qeff'][r0:r0 + lc], sb, preferred_element_type=F32)
            ch['o_ref'][r0:r0 + lc, ch['hs'] * GDN_DV:(ch['hs'] + 1) * GDN_DV] = o_c.astype(ch['o_ref'].dtype)
            kwu = ch['kwu'][c]
            e_end = jnp.exp(ch['end'][:, r0:r0 + 1])
            ch['s'] = (e_end * s - jnp.dot(kwu[:, :GDN_DK].astype(BF16), sb, preferred_element_type=F32)
                       + kwu[:, GDN_DK:])
    for ch in chains:
        s_ref[ch['idx']] = ch['s']


def gdn_scan(qkv, gcol, grow, bsz, seq):
    c_sz = GDN_BLOCK
    nb = seq // c_sz
    hq = GDN_QK_HEADS
    kcol = hq
    vcol = (2 * hq * GDN_DK) // (2 * GDN_DV)
    fwd = lambda b, h, c: c
    bwd = lambda b, h, c: nb - 1 - c

    def specs(pos):
        return [
            pl.BlockSpec((c_sz, GDN_DK), lambda b, h, c: (b * nb + pos(b, h, c), h)),
            pl.BlockSpec((c_sz, GDN_DK), lambda b, h, c: (b * nb + pos(b, h, c), kcol + h)),
            pl.BlockSpec((c_sz, 2 * GDN_DV), lambda b, h, c: (b * nb + pos(b, h, c), vcol + h)),
            pl.BlockSpec((1, 1, c_sz, 16), lambda b, h, c: (b, h, pos(b, h, c), 0)),
            pl.BlockSpec((1, 1, 16, c_sz), lambda b, h, c: (b, h, 0, pos(b, h, c))),
        ]

    out = jax.ShapeDtypeStruct((bsz * seq, GDN_V_HEADS * GDN_DV), BF16)
    return pl.pallas_call(
        _gdn_scan_kernel,
        out_shape=(out, out),
        grid=(bsz, hq, nb),
        in_specs=specs(fwd) + specs(bwd),
        out_specs=(
            pl.BlockSpec((c_sz, 2 * GDN_DV), lambda b, h, c: (b * nb + c, h)),
            pl.BlockSpec((c_sz, 2 * GDN_DV), lambda b, h, c: (b * nb + nb - 1 - c, h)),
        ),
        scratch_shapes=[pltpu.VMEM((4, GDN_DK, GDN_DV), F32)],
        compiler_params=_params("parallel", "parallel", "arbitrary"),
        name="gdn_scan",
    )(qkv, qkv, qkv, gcol, grow, qkv, qkv, qkv, gcol, grow)


def _gdn_out_kernel(of_ref, ob_ref, z_ref, gn_ref, w_ref, r_ref, o_ref):
    @pl.when(pl.program_id(1) == 0)
    def _():
        o_ref[...] = r_ref[...]

    gn = gn_ref[...]
    parts = []
    for h in range(of_ref.shape[1] // GDN_DV):
        sl = slice(h * GDN_DV, (h + 1) * GDN_DV)
        o = of_ref[:, sl].astype(F32) + ob_ref[:, sl].astype(F32)
        z = z_ref[:, sl].astype(F32)
        parts.append((_rms(o, gn) * (z * _sigmoid(z))).astype(BF16))
    o_ref[...] += jnp.dot(jnp.concatenate(parts, axis=1), w_ref[...], preferred_element_type=F32)


def gdn_out(o_f, o_b, z, out_norm, w_o, res, tm=512, tk=1024):
    m, k = o_f.shape
    n = w_o.shape[1]
    return pl.pallas_call(
        _gdn_out_kernel,
        out_shape=jax.ShapeDtypeStruct((m, n), F32),
        grid=(m // tm, k // tk),
        in_specs=[
            pl.BlockSpec((tm, tk), lambda i, j: (i, j)),
            pl.BlockSpec((tm, tk), lambda i, j: (i, j)),
            pl.BlockSpec((tm, tk), lambda i, j: (i, j)),
            pl.BlockSpec((1, GDN_DV), lambda i, j: (0, 0)),
            pl.BlockSpec((tk, n), lambda i, j: (j, 0)),
            pl.BlockSpec((tm, n), lambda i, j: (i, 0)),
        ],
        out_specs=pl.BlockSpec((tm, n), lambda i, j: (i, 0)),
        compiler_params=_params("parallel", "arbitrary"),
        name="gdn_out",
    )(o_f, o_b, z, out_norm.reshape(1, GDN_DV), w_o, res)


def gated_deltanet(x, gmix, w_in, conv_w, a_log, dt_bias, out_norm, w_o, bsz, seq):
    qk_w = GDN_QK_HEADS * GDN_DK
    v_w = GDN_V_HEADS * GDN_DV
    cw = 2 * qk_w + v_w
    w_in = w_in.astype(BF16)
    qkv = conv_in(x, gmix, w_in, cw, conv_w, seq, qk_w, 2 * qk_w, GDN_DK ** -0.5)
    z = norm_matmul(x, gmix, w_in, out_dtype=BF16, col0=cw, ncols=v_w)
    gb = gdn_gates(x, gmix, w_in, cw + v_w, a_log, dt_bias)
    nab = gb.shape[1] // 2
    ab = gb[:, :nab].reshape(bsz, seq, 2, 2, GDN_QK_HEADS, 2).transpose(0, 4, 1, 2, 3, 5)
    tot = gb[:, nab:].reshape(bsz, seq, 2, 2, GDN_QK_HEADS, 2)[:, :, :, 0].transpose(0, 3, 1, 2, 4)[..., None, :]
    gcol = jnp.concatenate([ab, tot, jnp.zeros_like(tot)], axis=4).reshape(bsz, GDN_QK_HEADS, seq, 16)
    grow = jnp.swapaxes(gcol, 2, 3)
    o_f, o_b = gdn_scan(qkv, gcol, grow, bsz, seq)
    return gdn_out(o_f, o_b, z, out_norm, w_o.astype(BF16), x)


def _na_bias_table(rpb, rows):
    win_r = min(NA_WIN_R, rows)
    cols = np.arange(GRID_W)
    col_start = np.clip(cols - NA_WIN_C // 2, 0, GRID_W - NA_WIN_C)
    col_valid = (cols[None, :] >= col_start[:, None]) & (cols[None, :] < col_start[:, None] + NA_WIN_C)
    dc_idx = np.clip(cols[None, :] - cols[:, None] + NA_WIN_C - 1, 0, 2 * NA_WIN_C - 2)
    bias_c = jnp.where(col_valid, rpb[:, :, dc_idx].astype(F32), NEG_INF)
    dr = np.arange(NA_WIN_R)[:, None] + np.arange(win_r)[None, :]
    tab = bias_c[:, dr]
    return jnp.transpose(tab, (0, 1, 3, 2, 4)).reshape(rpb.shape[0], NA_WIN_R, GRID_W, win_r * GRID_W)


def _na_kernel(q_ref, k_ref, v_ref, b_ref, o_ref, *, rows, win_r, scale):
    wk = win_r * GRID_W

    group = 8 if rows % 8 == 0 else 1

    def body(it, carry):
        rs = [it * group + i for i in range(group)]
        r0s = [jnp.clip(r - win_r // 2, 0, rows - win_r) for r in rs]
        scores = []
        for r, r0 in zip(rs, r0s):
            q = q_ref[pl.ds(pl.multiple_of(r * GRID_W, GRID_W), GRID_W), :]
            kw = k_ref[pl.ds(pl.multiple_of(r0 * GRID_W, GRID_W), wk), :]
            s = lax.dot_general(q, kw, (((1,), (1,)), ((), ())), preferred_element_type=F32) * scale
            scores.append(s + b_ref[0, r0 - r + NA_WIN_R - 1])
        probs, dens = [], []
        for s in scores:
            p = jnp.exp(s - jnp.max(s, axis=-1, keepdims=True))
            dens.append(jnp.sum(p, axis=-1, keepdims=True))
            probs.append(p.astype(BF16))
        for r, r0, p, den in zip(rs, r0s, probs, dens):
            vw = v_ref[pl.ds(pl.multiple_of(r0 * GRID_W, GRID_W), wk), :]
            o = jnp.dot(p, vw, preferred_element_type=F32) / den
            o_ref[pl.ds(pl.multiple_of(r * GRID_W, GRID_W), GRID_W), :] = o.astype(o_ref.dtype)
        return carry

    lax.fori_loop(0, rows // group, body, 0)


def na_attention(qkv, rpb, bsz, seq):
    d = qkv.shape[1] // 3
    dh = d // NA_HEADS
    rows = seq // GRID_W
    win_r = min(NA_WIN_R, rows)
    table = _na_bias_table(rpb, rows)
    blk = lambda off: pl.BlockSpec((seq, dh), lambda b, h: (b, off + h))
    return pl.pallas_call(
        functools.partial(_na_kernel, rows=rows, win_r=win_r, scale=dh ** -0.5),
        out_shape=jax.ShapeDtypeStruct((bsz * seq, d), BF16),
        grid=(bsz, NA_HEADS),
        in_specs=[blk(0), blk(NA_HEADS), blk(2 * NA_HEADS),
                  pl.BlockSpec((1, NA_WIN_R, GRID_W, win_r * GRID_W), lambda b, h: (h, 0, 0, 0))],
        out_specs=blk(0),
        compiler_params=_params("parallel", "parallel"),
        name="na_attention",
    )(qkv, qkv, qkv, table)


def _sg_out_kernel(u_ref, v_ref, gn_ref, ws_ref, bs_ref, w_ref, r_ref, o_ref, a_ref, *, tm):
    @pl.when(pl.program_id(1) == 0)
    def _():
        vn = _rms(v_ref[...].astype(F32), gn_ref[...]).astype(BF16)
        bs = bs_ref[...]
        gd = vn.shape[1] // SG_GROUPS
        for c in range(tm // SG_CHUNK):
            rs = slice(c * SG_CHUNK, (c + 1) * SG_CHUNK)
            for g in range(SG_GROUPS):
                cs = slice(g * gd, (g + 1) * gd)
                mixed = jnp.dot(ws_ref[g], vn[rs, cs], preferred_element_type=F32) + bs[:, g:g + 1]
                a_ref[rs, cs] = (u_ref[rs, cs].astype(F32) * mixed).astype(BF16)

    o_ref[...] = r_ref[...] + jnp.dot(a_ref[...], w_ref[...], preferred_element_type=F32)


def sg_out(uv, sg_norm, w_s, b_s, w_o, res, tm=512, tn=1024):
    m = uv.shape[0]
    width = uv.shape[1] // 2
    n = w_o.shape[1]
    return pl.pallas_call(
        functools.partial(_sg_out_kernel, tm=tm),
        out_shape=jax.ShapeDtypeStruct((m, n), F32),
        grid=(m // tm, n // tn),
        in_specs=[
            pl.BlockSpec((tm, width), lambda i, j: (i, 0)),
            pl.BlockSpec((tm, width), lambda i, j: (i, 1)),
            pl.BlockSpec((1, width), lambda i, j: (0, 0)),
            pl.BlockSpec(w_s.shape, lambda i, j: (0, 0, 0)),
            pl.BlockSpec((SG_CHUNK, SG_GROUPS), lambda i, j: (0, 0)),
            pl.BlockSpec((width, tn), lambda i, j: (0, j)),
            pl.BlockSpec((tm, tn), lambda i, j: (i, j)),
        ],
        out_specs=pl.BlockSpec((tm, tn), lambda i, j: (i, j)),
        scratch_shapes=[pltpu.VMEM((tm, width), BF16)],
        compiler_params=_params("parallel", "arbitrary"),
        name="sg_out",
    )(uv, uv, sg_norm.reshape(1, width), w_s.astype(BF16), b_s.T.astype(F32), w_o, res)


S5_L = 32
S5_W = S5_L * S5_GROUP_DIM


def _cexp(are, aim, dt, e):
    mag = jnp.exp(are * dt * e)
    ang = aim * dt * e
    return mag * jnp.cos(ang), mag * jnp.sin(ang)


def _s5_prep_kernel(arc_ref, aic_ref, arr_ref, air_ref, ldt_ref, btr_ref, bti_ref, ctr_ref, cti_ref,
                    k_ref, bm_ref, cm_ref, ap_ref, *, rev):
    hp = lax.Precision.HIGHEST
    ll, cg, w, p = S5_L, S5_GROUP_DIM, S5_W, S5_STATE
    sh = cg.bit_length() - 1
    dt = jnp.exp(ldt_ref[0])
    arc, aic = arc_ref[0], aic_ref[0]
    arr, air = arr_ref[0], air_ref[0]
    abr, abi = _cexp(arr, air, dt, 1.0)
    nr, ni = abr - 1.0, abi
    den = arr * arr + air * air
    cr, ci = (nr * arr + ni * air) / den, (ni * arr - nr * air) / den
    btr, bti = btr_ref[0], bti_ref[0]
    bbr, bbi = cr * btr - ci * bti, cr * bti + ci * btr
    lane = lax.broadcasted_iota(jnp.int32, (cg, w), 1)
    sel = jnp.where((lane & (cg - 1)) == lax.broadcasted_iota(jnp.int32, (cg, w), 0), 1.0, 0.0)
    cer = jnp.dot(ctr_ref[0], sel, precision=hp, preferred_element_type=F32)
    cei = jnp.dot(cti_ref[0], sel, precision=hp, preferred_element_type=F32)
    tl = (lax.broadcasted_iota(jnp.int32, (p, w), 1) >> sh).astype(F32)

    def cz(e):
        zr, zi = _cexp(arc, aic, dt, e)
        return cer * zr - cei * zi, cer * zi + cei * zr

    czr, czi = cz((ll - 1.0 - tl) if rev else tl)
    r = jnp.dot(bbr, czr, precision=hp, preferred_element_type=F32) - jnp.dot(bbi, czi, precision=hp,
                                                                                preferred_element_type=F32)
    lane_r = lax.broadcasted_iota(jnp.int32, (cg, w), 1)
    for s in range(ll):
        if rev:
            blk = jnp.where(lane_r < cg * (s + 1), pltpu.roll(r, (w - cg * (ll - 1 - s)) % w, 1), 0.0)
        else:
            blk = jnp.where(lane_r >= cg * s, pltpu.roll(r, cg * s, 1), 0.0)
        k_ref[0, s * cg:(s + 1) * cg, :] = blk.astype(k_ref.dtype)
    acr, aci = _cexp(arc, aic, dt, 1.0)
    c1r, c1i = czr * acr - czi * aci, czr * aci + czi * acr
    cm_ref[0, 0:p, :] = c1r.astype(cm_ref.dtype)
    cm_ref[0, p:2 * p, :] = (-c1i).astype(cm_ref.dtype)
    srow = (lax.broadcasted_iota(jnp.int32, (w, p), 0) >> sh).astype(F32)
    zr, zi = _cexp(arr, air, dt, srow if rev else (ll - 1.0 - srow))
    tbr, tbi = jnp.tile(bbr, (ll, 1)), jnp.tile(bbi, (ll, 1))
    bm_ref[0, :, 0:p] = (zr * tbr - zi * tbi).astype(bm_ref.dtype)
    bm_ref[0, :, p:2 * p] = (zr * tbi + zi * tbr).astype(bm_ref.dtype)
    ek = (ll << lax.broadcasted_iota(jnp.int32, (8, p), 0)).astype(F32)
    pr, pi = _cexp(arr, air, dt, ek)
    ap_ref[0, :, 0:p] = pr
    ap_ref[0, :, p:2 * p] = pi


def s5_prep(a_re, a_im, log_dt, b_re, b_im, c_re, c_im, rev):
    g, p = a_re.shape
    cg, w = S5_GROUP_DIM, S5_W
    col = lambda t: t.reshape(g, p, 1).astype(F32)
    row = lambda t: t.reshape(g, 1, p).astype(F32)
    tr = lambda t: jnp.swapaxes(t, 1, 2).astype(F32)
    spec = lambda s: pl.BlockSpec((1,) + s, lambda i: (i, 0, 0))
    return pl.pallas_call(
        functools.partial(_s5_prep_kernel, rev=rev),
        out_shape=(jax.ShapeDtypeStruct((g, w, w), BF16), jax.ShapeDtypeStruct((g, w, 2 * p), BF16),
                   jax.ShapeDtypeStruct((g, 2 * p, w), BF16), jax.ShapeDtypeStruct((g, 8, 2 * p), F32)),
        grid=(g,),
        in_specs=[spec((p, 1)), spec((p, 1)), spec((1, p)), spec((1, p)), spec((1, 1)),
                  spec((cg, p)), spec((cg, p)), spec((p, cg)), spec((p, cg))],
        out_specs=(spec((w, w)), spec((w, 2 * p)), spec((2 * p, w)), spec((8, 2 * p))),
        compiler_params=_params("parallel"),
        name="s5_prep",
    )(col(a_re), col(a_im), row(a_re), row(a_im), log_dt.reshape(g, 1, 1).astype(F32),
      tr(b_re), tr(b_im), tr(c_re), tr(c_im))


def _s5_chunk_scan(s, ap, nchunk, rev):
    n, w2 = s.shape
    p = w2 // 2
    m = lax.broadcasted_iota(jnp.int32, (n, w2), 0) & (nchunk - 1)
    lane = lax.broadcasted_iota(jnp.int32, (1, w2), 1)

    def shift(x, k):
        if rev:
            return jnp.where(m + k <= nchunk - 1, pltpu.roll(x, n - k, 0), 0.0)
        return jnp.where(m >= k, pltpu.roll(x, k, 0), 0.0)

    x = s
    k, lvl = 1, 0
    while k < nchunk:
        a = ap[lvl:lvl + 1, :]
        a1 = jnp.where(lane < p, a, pltpu.roll(a, p, 1))
        a2 = jnp.where(lane < p, -pltpu.roll(a, p, 1), a)
        xs = shift(x, k)
        x = x + a1 * xs + a2 * pltpu.roll(xs, p, 1)
        k, lvl = 2 * k, lvl + 1
    return shift(x, 1)


def _s5_main_kernel(u_ref, kf_ref, kb_ref, bf_ref, bb_ref, cf_ref, cb_ref, af_ref, ab_ref, y_ref, *, nchunk):
    u = u_ref[0]
    y = jnp.dot(u, kf_ref[0], preferred_element_type=F32) + jnp.dot(u, kb_ref[0], preferred_element_type=F32)
    for bm_ref, cm_ref, ap_ref, rev in ((bf_ref, cf_ref, af_ref, False), (bb_ref, cb_ref, ab_ref, True)):
        s = jnp.dot(u, bm_ref[0], preferred_element_type=F32)
        xin = _s5_chunk_scan(s, ap_ref[0], nchunk, rev)
        y = y + jnp.dot(xin.astype(BF16), cm_ref[0], preferred_element_type=F32)
    y_ref[0] = y.astype(y_ref.dtype)


def s5_main(u, prep_f, prep_b, nchunk):
    g, n, w = u.shape
    p2 = 2 * S5_STATE
    spec = lambda s: pl.BlockSpec((1,) + s, lambda i: (i, 0, 0))
    kf, bf, cf, af = prep_f
    kb, bb, cb, ab = prep_b
    return pl.pallas_call(
        functools.partial(_s5_main_kernel, nchunk=nchunk),
        out_shape=jax.ShapeDtypeStruct((g, n, w), BF16),
        grid=(g,),
        in_specs=[spec((n, w)), spec((w, w)), spec((w, w)), spec((w, p2)), spec((w, p2)),
                  spec((p2, w)), spec((p2, w)), spec((8, p2)), spec((8, p2))],
        out_specs=spec((n, w)),
        compiler_params=_params("parallel"),
        name="s5_main",
    )(u, kf, kb, bf, bb, cf, cb, af, ab)


def _s5_out_kernel(x_ref, g_ref, y_ref, d_ref, wa_ref, wb_ref, o_ref, inv_ref, acca_ref, accb_ref):
    k = pl.program_id(1)
    tk = y_ref.shape[1]

    @pl.when(k == 0)
    def _():
        x = x_ref[...]
        inv_ref[...] = jnp.broadcast_to(lax.rsqrt(jnp.mean(x * x, axis=-1, keepdims=True) + EPS), inv_ref.shape)
        acca_ref[...] = jnp.zeros_like(acca_ref)
        accb_ref[...] = jnp.zeros_like(accb_ref)

    xk = x_ref[:, pl.ds(pl.multiple_of(k * tk, tk), tk)]
    h = xk * inv_ref[:, 0:1] * g_ref[...]
    a = _gelu_tanh(y_ref[...].astype(F32) + d_ref[...] * h).astype(BF16)
    acca_ref[...] += jnp.dot(a, wa_ref[...], preferred_element_type=F32)
    accb_ref[...] += jnp.dot(a, wb_ref[...], preferred_element_type=F32)

    @pl.when(k == pl.num_programs(1) - 1)
    def _():
        o_ref[...] = x_ref[...] + acca_ref[...] * _sigmoid(accb_ref[...])


def s5_out(x, g, y, d_skip, w_glu, tm=512, tk=512):
    m, d = x.shape
    n = w_glu.shape[1] // 2
    return pl.pallas_call(
        _s5_out_kernel,
        out_shape=jax.ShapeDtypeStruct((m, n), F32),
        grid=(m // tm, d // tk),
        in_specs=[
            pl.BlockSpec((tm, d), lambda i, k: (i, 0)),
            pl.BlockSpec((1, tk), lambda i, k: (0, k)),
            pl.BlockSpec((tm, tk), lambda i, k: (i, k)),
            pl.BlockSpec((1, tk), lambda i, k: (0, k)),
            pl.BlockSpec((tk, n), lambda i, k: (k, 0)),
            pl.BlockSpec((tk, n), lambda i, k: (k, 1)),
        ],
        out_specs=pl.BlockSpec((tm, n), lambda i, k: (i, 0)),
        scratch_shapes=[pltpu.VMEM((tm, 128), F32), pltpu.VMEM((tm, n), F32), pltpu.VMEM((tm, n), F32)],
        compiler_params=_params("parallel", "arbitrary"),
        name="s5_out",
    )(x, g.reshape(1, d), y, d_skip.reshape(1, d), w_glu, w_glu)


def s5_mixer(x, gmix, a_re, a_im, log_dt, b_re, b_im, c_re, c_im, d_skip, w_glu, bsz, seq):
    m, d = x.shape
    groups = d // S5_GROUP_DIM
    nchunk = seq // S5_L
    h = rmsnorm(x, gmix, out_dtype=BF16)
    u = h.reshape(m // S5_L, S5_L, groups, S5_GROUP_DIM).transpose(2, 0, 1, 3).reshape(groups, m // S5_L, S5_W)
    prep_f = s5_prep(a_re[0], a_im[0], log_dt[0], b_re[0], b_im[0], c_re[0], c_im[0], rev=False)
    prep_b = s5_prep(a_re[1], a_im[1], log_dt[1], b_re[1], b_im[1], c_re[1], c_im[1], rev=True)
    y = s5_main(u, prep_f, prep_b, nchunk)
    y = y.reshape(groups, m // S5_L, S5_L, S5_GROUP_DIM).transpose(1, 2, 0, 3).reshape(m, d)
    return s5_out(x, gmix, y, d_skip, w_glu.astype(BF16))


def _trunk(x, p, w, bsz, seq):
    depth = w['norm_mix'].shape[0]
    bf = lambda t: t.astype(BF16)
    for i in range(depth):
        kind, j = i % N_MIXERS, i // N_MIXERS
        gmix = w['norm_mix'][i]
        if kind == 0:
            qkv = norm_matmul(x, gmix, bf(w['na_w_qkv'][j]), out_dtype=BF16)
            att = na_attention(qkv, w['na_rpb'][j], bsz, seq)
            x = matmul_res(att, bf(w['na_w_o'][j]), x)
        elif kind == 1:
            uv = norm_matmul(x, gmix, bf(w['sg_w_in'][j]), act="gelu", out_dtype=BF16)
            x = sg_out(uv, w['sg_norm'][j], w['sg_w_s'][j], w['sg_b_s'][j], bf(w['sg_w_o'][j]), x)
        elif kind == 2:
            x = gated_deltanet(x, gmix, w['gdn_w_in'][j], w['gdn_conv_w'][j], w['gdn_a_log'][j],
                               w['gdn_dt_bias'][j], w['gdn_out_norm'][j], w['gdn_w_o'][j], bsz, seq)
        else:
            x = s5_mixer(x, gmix, w['s5_a_re'][j], w['s5_a_im'][j], w['s5_log_dt'][j], w['s5_b_re'][j],
                         w['s5_b_im'][j], w['s5_c_re'][j], w['s5_c_im'][j], w['s5_d'][j], w['s5_w_glu'][j], bsz, seq)
        x = ffn(x, w['norm_ffn'][i], bf(w['ffn_w_gu'][i]), w['ffn_conv_w'][i], w['ffn_conv_b'][i],
                bf(w['ffn_w_down'][i]), seq)
        x = ple(x, p, i, w['norm_ple'][i], bf(w['ple_w_gate'][i]), bf(w['ple_w_proj'][i]))
    return x


def kernel(x_prompt, x_sample, p_prompt, p_sample, norm_mix, norm_ffn, norm_ple, final_norm, na_w_qkv, na_w_o, na_rpb, sg_w_in, sg_norm, sg_w_s, sg_b_s, sg_w_o, gdn_w_in, gdn_conv_w, gdn_a_log, gdn_dt_bias, gdn_out_norm, gdn_w_o, s5_a_re, s5_a_im, s5_log_dt, s5_b_re, s5_b_im, s5_c_re, s5_c_im, s5_d, s5_w_glu, ffn_w_gu, ffn_conv_w, ffn_conv_b, ffn_w_down, ple_w_proj, ple_w_gate):
    w = dict(norm_mix=norm_mix, norm_ffn=norm_ffn, norm_ple=norm_ple, final_norm=final_norm,
             na_w_qkv=na_w_qkv, na_w_o=na_w_o, na_rpb=na_rpb,
             sg_w_in=sg_w_in, sg_norm=sg_norm, sg_w_s=sg_w_s, sg_b_s=sg_b_s, sg_w_o=sg_w_o,
             gdn_w_in=gdn_w_in, gdn_conv_w=gdn_conv_w, gdn_a_log=gdn_a_log, gdn_dt_bias=gdn_dt_bias,
             gdn_out_norm=gdn_out_norm, gdn_w_o=gdn_w_o,
             s5_a_re=s5_a_re, s5_a_im=s5_a_im, s5_log_dt=s5_log_dt, s5_b_re=s5_b_re, s5_b_im=s5_b_im,
             s5_c_re=s5_c_re, s5_c_im=s5_c_im, s5_d=s5_d, s5_w_glu=s5_w_glu,
             ffn_w_gu=ffn_w_gu, ffn_conv_w=ffn_conv_w, ffn_conv_b=ffn_conv_b, ffn_w_down=ffn_w_down,
             ple_w_proj=ple_w_proj, ple_w_gate=ple_w_gate)
    b1, seq, d = x_prompt.shape
    b2 = x_sample.shape[0]
    bsz = b1 + b2
    depth = p_prompt.shape[0]
    x = (x_prompt.reshape(b1 * seq, d), x_sample.reshape(b2 * seq, d))
    p = (p_prompt.reshape(depth, b1 * seq, -1), p_sample.reshape(depth, b2 * seq, -1))
    x = _trunk(x, p, w, bsz, seq)
    y1 = rmsnorm(x, final_norm, row0=0, rows=b1 * seq).reshape(b1, seq, d)
    y2 = rmsnorm(x, final_norm, row0=b1 * seq, rows=b2 * seq).reshape(b2, seq, d)
    return (y1, y2)
```

```python
import functools
import math

import jax
import jax.numpy as jnp
import numpy as np
from jax import lax
from jax.experimental import pallas as pl
from jax.experimental.pallas import tpu as pltpu

F32 = jnp.float32
BF16 = jnp.bfloat16

EPS = 1e-6
NEG_INF = -1e30
GRID_W = 64
NA_HEADS = 16
NA_WIN_R = 8
NA_WIN_C = 16
SG_CHUNK = 128
SG_GROUPS = 16
GDN_QK_HEADS = 16
GDN_V_HEADS = 32
GDN_DK = 128
GDN_DV = 128
GDN_CHUNK = 64
S5_GROUP_DIM = 16
S5_STATE = 64
N_MIXERS = 4

VMEM_LIMIT_BYTES = 56 * 1024 * 1024
HALO = 16


def _params(*sem):
    return pltpu.CompilerParams(dimension_semantics=sem, vmem_limit_bytes=VMEM_LIMIT_BYTES)


def _rms(x, g):
    return x * lax.rsqrt(jnp.mean(x * x, axis=-1, keepdims=True) + EPS) * g


def _gelu_tanh(x):
    return 0.5 * x * (1.0 + jnp.tanh(math.sqrt(2.0 / math.pi) * (x + 0.044715 * (x * x * x))))


def _sigmoid(x):
    return 1.0 / (1.0 + jnp.exp(-x))


def _rmsnorm_kernel(x_ref, g_ref, o_ref):
    o_ref[...] = _rms(x_ref[...], g_ref[...]).astype(o_ref.dtype)


def rmsnorm(x, g, out_dtype=F32, tm=512, row0=0, rows=None):
    m, d = x.shape
    rows = m if rows is None else rows
    off = row0 // tm
    return pl.pallas_call(
        _rmsnorm_kernel,
        out_shape=jax.ShapeDtypeStruct((rows, d), out_dtype),
        grid=(rows // tm,),
        in_specs=[pl.BlockSpec((tm, d), lambda i: (i + off, 0)), pl.BlockSpec((1, d), lambda i: (0, 0))],
        out_specs=pl.BlockSpec((tm, d), lambda i: (i, 0)),
        compiler_params=_params("parallel"),
        name="rmsnorm",
    )(x, g.reshape(1, d))


def _as_pair(x, tm):
    if isinstance(x, tuple):
        a, b = x
        return a, b, a.shape[0] // tm, a.shape[0] + b.shape[0]
    return x, x, None, x.shape[0]


def _pair_specs(block, n1, col):
    if n1 is None:
        return [pl.BlockSpec(block, lambda i, j: (i, col(i, j))), pl.BlockSpec(block, lambda i, j: (0, 0))]
    return [pl.BlockSpec(block, lambda i, j: (jnp.minimum(i, n1 - 1), col(i, j))),
            pl.BlockSpec(block, lambda i, j: (jnp.maximum(i - n1, 0), col(i, j)))]


def _pick(a_ref, b_ref, n1):
    if n1 is None:
        return a_ref[...]
    return jnp.where(pl.program_id(0) < n1, a_ref[...], b_ref[...])


def _norm_matmul_kernel(xa_ref, xb_ref, g_ref, w_ref, o_ref, hn_ref, *, act, n1):
    @pl.when(pl.program_id(1) == 0)
    def _():
        hn_ref[...] = _rms(_pick(xa_ref, xb_ref, n1), g_ref[...]).astype(BF16)

    y = jnp.dot(hn_ref[...], w_ref[...], preferred_element_type=F32)
    if act == "gelu":
        y = _gelu_tanh(y)
    o_ref[...] = y.astype(o_ref.dtype)


def norm_matmul(x, g, w, act=None, out_dtype=F32, tm=512, tn=1024, col0=0, ncols=None):
    xa, xb, n1, m = _as_pair(x, tm)
    d = xa.shape[1]
    n = w.shape[1] - col0 if ncols is None else ncols
    tn = next(t for t in (tn, 512, 256, 128) if n % t == 0 and col0 % t == 0)
    cb = col0 // tn
    return pl.pallas_call(
        functools.partial(_norm_matmul_kernel, act=act, n1=n1),
        out_shape=jax.ShapeDtypeStruct((m, n), out_dtype),
        grid=(m // tm, n // tn),
        in_specs=_pair_specs((tm, d), n1, lambda i, j: 0) + [
            pl.BlockSpec((1, d), lambda i, j: (0, 0)),
            pl.BlockSpec((d, tn), lambda i, j: (0, cb + j)),
        ],
        out_specs=pl.BlockSpec((tm, tn), lambda i, j: (i, j)),
        scratch_shapes=[pltpu.VMEM((tm, d), BF16)],
        compiler_params=_params("parallel", "arbitrary"),
        name="norm_matmul",
    )(xa, xb, g.reshape(1, d), w)


def _matmul_res_kernel(a_ref, w_ref, ra_ref, rb_ref, o_ref, *, n1):
    o_ref[...] = _pick(ra_ref, rb_ref, n1) + jnp.dot(a_ref[...].astype(BF16), w_ref[...],
                                                     preferred_element_type=F32)


def matmul_res(a, w, res, tm=512, tn=1024):
    m, k = a.shape
    n = w.shape[1]
    ra, rb, n1, _ = _as_pair(res, tm)
    return pl.pallas_call(
        functools.partial(_matmul_res_kernel, n1=n1),
        out_shape=jax.ShapeDtypeStruct((m, n), F32),
        grid=(m // tm, n // tn),
        in_specs=[
            pl.BlockSpec((tm, k), lambda i, j: (i, 0)),
            pl.BlockSpec((k, tn), lambda i, j: (0, j)),
        ] + _pair_specs((tm, tn), n1, lambda i, j: j),
        out_specs=pl.BlockSpec((tm, tn), lambda i, j: (i, j)),
        compiler_params=_params("parallel", "arbitrary"),
        name="matmul_res",
    )(a, w, ra, rb)


def _ffn_kernel(x_ref, xp_ref, xn_ref, g_ref, wg_ref, wu_ref, cw_ref, cb_ref, wd_ref, o_ref, hn_ref, acc_ref,
                *, tm, seq):
    i = pl.program_id(0)
    j = pl.program_id(1)

    @pl.when(j == 0)
    def _():
        g = g_ref[...]
        prev_ok = jnp.where((i * tm) % seq != 0, 1.0, 0.0)
        next_ok = jnp.where(((i + 1) * tm) % seq != 0, 1.0, 0.0)
        hn_ref[0:HALO, :] = (_rms(xp_ref[...], g) * prev_ok).astype(BF16)
        hn_ref[HALO:HALO + tm, :] = _rms(x_ref[...], g).astype(BF16)
        hn_ref[HALO + tm:, :] = (_rms(xn_ref[...], g) * next_ok).astype(BF16)
        acc_ref[...] = jnp.zeros_like(acc_ref)

    rows = tm + 2 * HALO
    gate = jnp.dot(hn_ref[...], wg_ref[0], preferred_element_type=F32)
    up = jnp.dot(hn_ref[HALO:HALO + tm, :], wu_ref[0], preferred_element_type=F32)
    cw = cw_ref[...]
    g_prev = pltpu.roll(gate, 1, 0)[HALO:HALO + tm]
    g_next = pltpu.roll(gate, rows - 1, 0)[HALO:HALO + tm]
    gc = cw[0:1] * g_prev + cw[1:2] * gate[HALO:HALO + tm] + cw[2:3] * g_next + cb_ref[...]
    act = (gc * _sigmoid(gc) * up).astype(BF16)
    acc_ref[...] += jnp.dot(act, wd_ref[0], preferred_element_type=F32)

    @pl.when(j == pl.num_programs(1) - 1)
    def _():
        o_ref[...] = x_ref[...] + acc_ref[...]


def ffn(x, g, w_gu, conv_w, conv_b, w_down, layer, seq, tm=512, tf=512):
    m, d = x.shape
    f = w_down.shape[1]
    nf = f // tf
    hb = tm // HALO
    last = m // HALO - 1
    return pl.pallas_call(
        functools.partial(_ffn_kernel, tm=tm, seq=seq),
        out_shape=jax.ShapeDtypeStruct((m, d), F32),
        grid=(m // tm, nf),
        in_specs=[
            pl.BlockSpec((tm, d), lambda i, j: (i, 0)),
            pl.BlockSpec((HALO, d), lambda i, j: (jnp.maximum(i * hb - 1, 0), 0)),
            pl.BlockSpec((HALO, d), lambda i, j: (jnp.minimum((i + 1) * hb, last), 0)),
            pl.BlockSpec((1, d), lambda i, j: (0, 0)),
            pl.BlockSpec((1, d, tf), lambda i, j: (layer, 0, j)),
            pl.BlockSpec((1, d, tf), lambda i, j: (layer, 0, j + nf)),
            pl.BlockSpec((3, tf), lambda i, j: (0, j)),
            pl.BlockSpec((1, tf), lambda i, j: (0, j)),
            pl.BlockSpec((1, tf, d), lambda i, j: (layer, j, 0)),
        ],
        out_specs=pl.BlockSpec((tm, d), lambda i, j: (i, 0)),
        scratch_shapes=[pltpu.VMEM((tm + 2 * HALO, d), BF16), pltpu.VMEM((tm, d), F32)],
        compiler_params=_params("parallel", "arbitrary"),
        name="ffn",
    )(x, x, x, g.reshape(1, d), w_gu, w_gu, conv_w, conv_b.reshape(1, f), w_down)


def _ple_kernel(x_ref, pa_ref, pb_ref, g_ref, wg_ref, wp_ref, o_ref, *, n1):
    x = x_ref[...]
    hn = _rms(x, g_ref[...]).astype(BF16)
    gate = _sigmoid(jnp.dot(hn, wg_ref[...], preferred_element_type=F32))
    p = jnp.where(pl.program_id(0) < n1, pa_ref[0], pb_ref[0])
    proj = jnp.dot(p.astype(BF16), wp_ref[...], preferred_element_type=F32)
    o_ref[...] = x + gate * proj


def ple(x, p, layer, g, w_gate, w_proj, tm=512):
    m, d = x.shape
    pa, pb = p if isinstance(p, tuple) else (p, p)
    n1 = pa.shape[1] // tm
    pd = pa.shape[2]
    return pl.pallas_call(
        functools.partial(_ple_kernel, n1=n1),
        out_shape=jax.ShapeDtypeStruct((m, d), F32),
        grid=(m // tm,),
        in_specs=[
            pl.BlockSpec((tm, d), lambda i: (i, 0)),
            pl.BlockSpec((1, tm, pd), lambda i: (layer, jnp.minimum(i, n1 - 1), 0)),
            pl.BlockSpec((1, tm, pd), lambda i: (layer, jnp.maximum(i - n1, 0), 0)),
            pl.BlockSpec((1, d), lambda i: (0, 0)),
            pl.BlockSpec((d, d), lambda i: (0, 0)),
            pl.BlockSpec((pd, d), lambda i: (0, 0)),
        ],
        out_specs=pl.BlockSpec((tm, d), lambda i: (i, 0)),
        compiler_params=_params("parallel"),
        name="ple",
    )(x, pa, pb, g.reshape(1, d), w_gate, w_proj)


GDN_BLOCK = 256


def _conv_in_kernel(x_ref, xp_ref, xn_ref, g_ref, w_ref, cw_ref, o_ref, hn_ref, *, tm, seq, nq, nqk, scale):
    i = pl.program_id(0)
    j = pl.program_id(1)

    @pl.when(j == 0)
    def _():
        g = g_ref[...]
        prev_ok = jnp.where((i * tm) % seq != 0, 1.0, 0.0)
        next_ok = jnp.where(((i + 1) * tm) % seq != 0, 1.0, 0.0)
        hn_ref[0:HALO, :] = (_rms(xp_ref[...], g) * prev_ok).astype(BF16)
        hn_ref[HALO:HALO + tm, :] = _rms(x_ref[...], g).astype(BF16)
        hn_ref[HALO + tm:, :] = (_rms(xn_ref[...], g) * next_ok).astype(BF16)

    cw = cw_ref[...]
    w = w_ref[...]
    tn = w.shape[1]
    half = tm // 2
    rows = half + 2 * HALO
    for lo in (0, half):
        y = jnp.dot(hn_ref[lo:lo + rows, :], w, preferred_element_type=F32)
        y_prev = pltpu.roll(y, 1, 0)[HALO:HALO + half]
        y_next = pltpu.roll(y, rows - 1, 0)[HALO:HALO + half]
        c = cw[0:1] * y_prev + cw[1:2] * y[HALO:HALO + half] + cw[2:3] * y_next
        c = c * _sigmoid(c)
        for s in range(tn // 128):
            cs = c[:, s * 128:(s + 1) * 128]
            inv = lax.rsqrt(jnp.sum(cs * cs, axis=-1, keepdims=True) + EPS)
            f = jnp.where(j < nq, inv * scale, jnp.where(j < nqk, inv, 1.0))
            o_ref[lo:lo + half, s * 128:(s + 1) * 128] = (cs * f).astype(o_ref.dtype)


def conv_in(x, g, w, n, conv_w, seq, n_q, n_qk, scale, tm=512, tn=512):
    m, d = x.shape
    hb = tm // HALO
    last = m // HALO - 1
    return pl.pallas_call(
        functools.partial(_conv_in_kernel, tm=tm, seq=seq, nq=n_q // tn, nqk=n_qk // tn, scale=scale),
        out_shape=jax.ShapeDtypeStruct((m, n), BF16),
        grid=(m // tm, n // tn),
        in_specs=[
            pl.BlockSpec((tm, d), lambda i, j: (i, 0)),
            pl.BlockSpec((HALO, d), lambda i, j: (jnp.maximum(i * hb - 1, 0), 0)),
            pl.BlockSpec((HALO, d), lambda i, j: (jnp.minimum((i + 1) * hb, last), 0)),
            pl.BlockSpec((1, d), lambda i, j: (0, 0)),
            pl.BlockSpec((d, tn), lambda i, j: (0, j)),
            pl.BlockSpec((3, tn), lambda i, j: (0, j)),
        ],
        out_specs=pl.BlockSpec((tm, tn), lambda i, j: (i, j)),
        scratch_shapes=[pltpu.VMEM((tm + 2 * HALO, d), BF16)],
        compiler_params=_params("parallel", "arbitrary"),
        name="gdn_conv_in",
    )(x, x, x, g.reshape(1, d), w, conv_w)


def _gdn_gates_kernel(x_ref, g_ref, w_ref, alog_ref, bias_ref, isg_ref, o_ref):
    hn = _rms(x_ref[...], g_ref[...]).astype(BF16)
    y = jnp.dot(hn, w_ref[...], preferred_element_type=F32)
    t = y + bias_ref[...]
    softplus = jnp.maximum(t, 0.0) + jnp.log1p(jnp.exp(-jnp.abs(t)))
    isg = isg_ref[...] > 0.5
    base = jnp.where(isg, -jnp.exp(alog_ref[...]) * softplus, 0.0)
    tm, n = base.shape
    lc = GDN_CHUNK
    pos = lax.broadcasted_iota(jnp.int32, (tm, n), 0) & (lc - 1)
    pre, suf = base, base
    k = 1
    while k < lc:
        pre = pre + jnp.where(pos >= k, pltpu.roll(pre, k, 0), 0.0)
        suf = suf + jnp.where(pos + k <= lc - 1, pltpu.roll(suf, tm - k, 0), 0.0)
        k *= 2
    is_bwd = lax.broadcasted_iota(jnp.int32, (1, n), 1) >= n // 2
    o_ref[:, 0:n] = jnp.where(isg, jnp.where(is_bwd, suf, pre), _sigmoid(y))
    o_ref[:, n:2 * n] = pre + suf - base


def gdn_gates(x, g, w, col0, a_log, dt_bias, tm=512):
    m, d = x.shape
    n = w.shape[1] - col0
    nh = a_log.shape[-1]
    zeros = jnp.zeros((2, 1, nh), F32)
    arrange = lambda t: jnp.concatenate([t.reshape(2, 1, nh).astype(F32), zeros], axis=1).reshape(1, n)
    isg = jnp.concatenate([jnp.ones((2, 1, nh), F32), zeros], axis=1).reshape(1, n)
    return pl.pallas_call(
        _gdn_gates_kernel,
        out_shape=jax.ShapeDtypeStruct((m, 2 * n), F32),
        grid=(m // tm,),
        in_specs=[
            pl.BlockSpec((tm, d), lambda i: (i, 0)),
            pl.BlockSpec((1, d), lambda i: (0, 0)),
            pl.BlockSpec((d, n), lambda i: (0, col0 // n)),
            pl.BlockSpec((1, n), lambda i: (0, 0)),
            pl.BlockSpec((1, n), lambda i: (0, 0)),
            pl.BlockSpec((1, n), lambda i: (0, 0)),
        ],
        out_specs=pl.BlockSpec((tm, 2 * n), lambda i: (i, 0)),
        compiler_params=_params("parallel"),
        name="gdn_gates",
    )(x, g.reshape(1, d), w, arrange(a_log), arrange(dt_bias), isg)


def _gdn_chains(q_ref, k_ref, v_ref, gc_ref, gr_ref, o_ref, rev):
    c_sz, lc = GDN_BLOCK, GDN_CHUNK
    d = 1 if rev else 0
    ii = lax.broadcasted_iota(jnp.int32, (c_sz, c_sz), 0)
    jj = lax.broadcasted_iota(jnp.int32, (c_sz, c_sz), 1)
    sh = lc.bit_length() - 1
    same = (ii >> sh) == (jj >> sh)
    incl = jnp.logical_and(same, (jj >= ii) if rev else (jj <= ii))
    eye = ii == jj
    q = q_ref[...]
    k = k_ref[...]
    kf = k.astype(F32)
    qf = q.astype(F32)
    gc = gc_ref[0, 0][:, 8 * d:8 * d + 8]
    gr = gr_ref[0, 0][8 * d:8 * d + 8, :]
    gam_c = gc[:, 0:2]
    gam_r = gr[0:2]
    end_r = gr[4:6]
    gram = lax.dot_general(k, k, (((1,), (1,)), ((), ())), preferred_element_type=F32)
    qk = lax.dot_general(q, k, (((1,), (1,)), ((), ())), preferred_element_type=F32)
    chains = []
    for hs in range(2):
        gcol = gam_c[:, hs:hs + 1]
        grow = gam_r[hs:hs + 1, :]
        bcol = gc[:, 2 + hs:3 + hs]
        dec = jnp.exp(jnp.where(incl, gcol - grow, NEG_INF))
        e_g = jnp.exp(gcol)
        vh = v_ref[:, hs * GDN_DV:(hs + 1) * GDN_DV].astype(F32)
        chains.append(dict(
            n=(-(gram * jnp.where(eye, 0.0, dec)) * bcol).astype(BF16),
            x=jnp.concatenate([kf * (bcol * e_g), vh * bcol], axis=1),
            attn=(qk * dec).astype(BF16),
            qe=qf * e_g,
            kd=kf * jnp.exp(gc[:, 4 + hs:5 + hs] - gcol),
            end=end_r[hs:hs + 1, :],
            o_ref=o_ref, hs=hs, idx=2 * d + hs, rev=rev))
    return chains


def _gdn_scan_kernel(qf_ref, kf_ref, vf_ref, gcf_ref, grf_ref,
                     qb_ref, kb_ref, vb_ref, gcb_ref, grb_ref, of_ref, ob_ref, s_ref):
    @pl.when(pl.program_id(2) == 0)
    def _():
        s_ref[...] = jnp.zeros_like(s_ref)

    c_sz, lc = GDN_BLOCK, GDN_CHUNK
    nchunk = c_sz // lc
    width = GDN_DK + GDN_DV
    chains = (_gdn_chains(qf_ref, kf_ref, vf_ref, gcf_ref, grf_ref, of_ref, False)
              + _gdn_chains(qb_ref, kb_ref, vb_ref, gcb_ref, grb_ref, ob_ref, True))
    nlev = lc.bit_length() - 1
    for lvl in range(nlev):
        for ch in chains:
            nb = ch['n']
            xb = ch['x'].astype(BF16)
            if lvl < nlev - 1:
                r = jnp.dot(nb, jnp.concatenate([xb, nb], axis=1), preferred_element_type=F32)
                ch['x'] = ch['x'] + r[:, :width]
                ch['n'] = r[:, width:].astype(BF16)
            else:
                ch['x'] = ch['x'] + jnp.dot(nb, xb, preferred_element_type=F32)
    irow = lax.broadcasted_iota(jnp.int32, (c_sz, 1), 0) >> nlev
    for ch in chains:
        wub = ch['x'].astype(BF16)
        awu = jnp.dot(ch['attn'], wub, preferred_element_type=F32)
        ch['qeff'] = (ch['qe'] - awu[:, :GDN_DK]).astype(BF16)
        ch['o_in'] = awu[:, GDN_DK:]
        ch['kwu'] = [lax.dot_general(jnp.where(irow == c, ch['kd'], 0.0).astype(BF16), wub,
                                     (((0,), (0,)), ((), ())), preferred_element_type=F32)
                     for c in range(nchunk)]
        ch['s'] = s_ref[ch['idx']]
    for step in range(nchunk):
        for ch in chains:
            c = nchunk - 1 - step if ch['rev'] else step
            r0 = c * lc
            s = ch['s']
            sb = s.astype(BF16)
            o_c = ch['o_in'][r0:r0 + lc] + jnp.dot(ch['qeff'][r0:r0 + lc], sb, preferred_element_type=F32)
            ch['o_ref'][r0:r0 + lc, ch['hs'] * GDN_DV:(ch['hs'] + 1) * GDN_DV] = o_c.astype(ch['o_ref'].dtype)
            kwu = ch['kwu'][c]
            e_end = jnp.exp(ch['end'][:, r0:r0 + 1])
            ch['s'] = (e_end * s - jnp.dot(kwu[:, :GDN_DK].astype(BF16), sb, preferred_element_type=F32)
                       + kwu[:, GDN_DK:])
    for ch in chains:
        s_ref[ch['idx']] = ch['s']


def gdn_scan(qkv, gcol, grow, bsz, seq):
    c_sz = GDN_BLOCK
    nb = seq // c_sz
    hq = GDN_QK_HEADS
    kcol = hq
    vcol = (2 * hq * GDN_DK) // (2 * GDN_DV)
    fwd = lambda b, h, c: c
    bwd = lambda b, h, c: nb - 1 - c

    def specs(pos):
        return [
            pl.BlockSpec((c_sz, GDN_DK), lambda b, h, c: (b * nb + pos(b, h, c), h)),
            pl.BlockSpec((c_sz, GDN_DK), lambda b, h, c: (b * nb + pos(b, h, c), kcol + h)),
            pl.BlockSpec((c_sz, 2 * GDN_DV), lambda b, h, c: (b * nb + pos(b, h, c), vcol + h)),
            pl.BlockSpec((1, 1, c_sz, 16), lambda b, h, c: (b, h, pos(b, h, c), 0)),
            pl.BlockSpec((1, 1, 16, c_sz), lambda b, h, c: (b, h, 0, pos(b, h, c))),
        ]

    out = jax.ShapeDtypeStruct((bsz * seq, GDN_V_HEADS * GDN_DV), BF16)
    return pl.pallas_call(
        _gdn_scan_kernel,
        out_shape=(out, out),
        grid=(bsz, hq, nb),
        in_specs=specs(fwd) + specs(bwd),
        out_specs=(
            pl.BlockSpec((c_sz, 2 * GDN_DV), lambda b, h, c: (b * nb + c, h)),
            pl.BlockSpec((c_sz, 2 * GDN_DV), lambda b, h, c: (b * nb + nb - 1 - c, h)),
        ),
        scratch_shapes=[pltpu.VMEM((4, GDN_DK, GDN_DV), F32)],
        compiler_params=_params("parallel", "parallel", "arbitrary"),
        name="gdn_scan",
    )(qkv, qkv, qkv, gcol, grow, qkv, qkv, qkv, gcol, grow)


def _gdn_out_kernel(of_ref, ob_ref, z_ref, gn_ref, w_ref, r_ref, o_ref):
    @pl.when(pl.program_id(1) == 0)
    def _():
        o_ref[...] = r_ref[...]

    gn = gn_ref[...]
    parts = []
    for h in range(of_ref.shape[1] // GDN_DV):
        sl = slice(h * GDN_DV, (h + 1) * GDN_DV)
        o = of_ref[:, sl].astype(F32) + ob_ref[:, sl].astype(F32)
        z = z_ref[:, sl].astype(F32)
        parts.append((_rms(o, gn) * (z * _sigmoid(z))).astype(BF16))
    o_ref[...] += jnp.dot(jnp.concatenate(parts, axis=1), w_ref[...], preferred_element_type=F32)


def gdn_out(o_f, o_b, z, out_norm, w_o, res, tm=512, tk=1024):
    m, k = o_f.shape
    n = w_o.shape[1]
    return pl.pallas_call(
        _gdn_out_kernel,
        out_shape=jax.ShapeDtypeStruct((m, n), F32),
        grid=(m // tm, k // tk),
        in_specs=[
            pl.BlockSpec((tm, tk), lambda i, j: (i, j)),
            pl.BlockSpec((tm, tk), lambda i, j: (i, j)),
            pl.BlockSpec((tm, tk), lambda i, j: (i, j)),
            pl.BlockSpec((1, GDN_DV), lambda i, j: (0, 0)),
            pl.BlockSpec((tk, n), lambda i, j: (j, 0)),
            pl.BlockSpec((tm, n), lambda i, j: (i, 0)),
        ],
        out_specs=pl.BlockSpec((tm, n), lambda i, j: (i, 0)),
        compiler_params=_params("parallel", "arbitrary"),
        name="gdn_out",
    )(o_f, o_b, z, out_norm.reshape(1, GDN_DV), w_o, res)


def gated_deltanet(x, gmix, w_in, conv_w, a_log, dt_bias, out_norm, w_o, bsz, seq):
    qk_w = GDN_QK_HEADS * GDN_DK
    v_w = GDN_V_HEADS * GDN_DV
    cw = 2 * qk_w + v_w
    w_in = w_in.astype(BF16)
    qkv = conv_in(x, gmix, w_in, cw, conv_w, seq, qk_w, 2 * qk_w, GDN_DK ** -0.5)
    z = norm_matmul(x, gmix, w_in, out_dtype=BF16, col0=cw, ncols=v_w)
    gb = gdn_gates(x, gmix, w_in, cw + v_w, a_log, dt_bias)
    nab = gb.shape[1] // 2
    ab = gb[:, :nab].reshape(bsz, seq, 2, 2, GDN_QK_HEADS, 2).transpose(0, 4, 1, 2, 3, 5)
    tot = gb[:, nab:].reshape(bsz, seq, 2, 2, GDN_QK_HEADS, 2)[:, :, :, 0].transpose(0, 3, 1, 2, 4)[..., None, :]
    gcol = jnp.concatenate([ab, tot, jnp.zeros_like(tot)], axis=4).reshape(bsz, GDN_QK_HEADS, seq, 16)
    grow = jnp.swapaxes(gcol, 2, 3)
    o_f, o_b = gdn_scan(qkv, gcol, grow, bsz, seq)
    return gdn_out(o_f, o_b, z, out_norm, w_o.astype(BF16), x)


def _na_bias_table(rpb, rows):
    win_r = min(NA_WIN_R, rows)
    cols = np.arange(GRID_W)
    col_start = np.clip(cols - NA_WIN_C // 2, 0, GRID_W - NA_WIN_C)
    col_valid = (cols[None, :] >= col_start[:, None]) & (cols[None, :] < col_start[:, None] + NA_WIN_C)
    dc_idx = np.clip(cols[None, :] - cols[:, None] + NA_WIN_C - 1, 0, 2 * NA_WIN_C - 2)
    bias_c = jnp.where(col_valid, rpb[:, :, dc_idx].astype(F32), NEG_INF)
    dr = np.arange(NA_WIN_R)[:, None] + np.arange(win_r)[None, :]
    tab = bias_c[:, dr]
    return jnp.transpose(tab, (0, 1, 3, 2, 4)).reshape(rpb.shape[0], NA_WIN_R, GRID_W, win_r * GRID_W)


def _na_kernel(q_ref, k_ref, v_ref, b_ref, o_ref, *, rows, win_r, scale):
    wk = win_r * GRID_W

    group = 8 if rows % 8 == 0 else 1

    def body(it, carry):
        rs = [it * group + i for i in range(group)]
        r0s = [jnp.clip(r - win_r // 2, 0, rows - win_r) for r in rs]
        scores = []
        for r, r0 in zip(rs, r0s):
            q = q_ref[pl.ds(pl.multiple_of(r * GRID_W, GRID_W), GRID_W), :]
            kw = k_ref[pl.ds(pl.multiple_of(r0 * GRID_W, GRID_W), wk), :]
            s = lax.dot_general(q, kw, (((1,), (1,)), ((), ())), preferred_element_type=F32) * scale
            scores.append(s + b_ref[0, r0 - r + NA_WIN_R - 1])
        probs, dens = [], []
        for s in scores:
            p = jnp.exp(s - jnp.max(s, axis=-1, keepdims=True))
            dens.append(jnp.sum(p, axis=-1, keepdims=True))
            probs.append(p.astype(BF16))
        for r, r0, p, den in zip(rs, r0s, probs, dens):
            vw = v_ref[pl.ds(pl.multiple_of(r0 * GRID_W, GRID_W), wk), :]
            o = jnp.dot(p, vw, preferred_element_type=F32) / den
            o_ref[pl.ds(pl.multiple_of(r * GRID_W, GRID_W), GRID_W), :] = o.astype(o_ref.dtype)
        return carry

    lax.fori_loop(0, rows // group, body, 0)


def na_attention(qkv, rpb, bsz, seq):
    d = qkv.shape[1] // 3
    dh = d // NA_HEADS
    rows = seq // GRID_W
    win_r = min(NA_WIN_R, rows)
    table = _na_bias_table(rpb, rows)
    blk = lambda off: pl.BlockSpec((seq, dh), lambda b, h: (b, off + h))
    return pl.pallas_call(
        functools.partial(_na_kernel, rows=rows, win_r=win_r, scale=dh ** -0.5),
        out_shape=jax.ShapeDtypeStruct((bsz * seq, d), BF16),
        grid=(bsz, NA_HEADS),
        in_specs=[blk(0), blk(NA_HEADS), blk(2 * NA_HEADS),
                  pl.BlockSpec((1, NA_WIN_R, GRID_W, win_r * GRID_W), lambda b, h: (h, 0, 0, 0))],
        out_specs=blk(0),
        compiler_params=_params("parallel", "parallel"),
        name="na_attention",
    )(qkv, qkv, qkv, table)


def _sg_out_kernel(u_ref, v_ref, gn_ref, ws_ref, bs_ref, w_ref, r_ref, o_ref, a_ref, *, tm):
    @pl.when(pl.program_id(1) == 0)
    def _():
        vn = _rms(v_ref[...].astype(F32), gn_ref[...]).astype(BF16)
        bs = bs_ref[...]
        gd = vn.shape[1] // SG_GROUPS
        for c in range(tm // SG_CHUNK):
            rs = slice(c * SG_CHUNK, (c + 1) * SG_CHUNK)
            for g in range(SG_GROUPS):
                cs = slice(g * gd, (g + 1) * gd)
                mixed = jnp.dot(ws_ref[g], vn[rs, cs], preferred_element_type=F32) + bs[:, g:g + 1]
                a_ref[rs, cs] = (u_ref[rs, cs].astype(F32) * mixed).astype(BF16)

    o_ref[...] = r_ref[...] + jnp.dot(a_ref[...], w_ref[...], preferred_element_type=F32)


def sg_out(uv, sg_norm, w_s, b_s, w_o, res, tm=512, tn=1024):
    m = uv.shape[0]
    width = uv.shape[1] // 2
    n = w_o.shape[1]
    return pl.pallas_call(
        functools.partial(_sg_out_kernel, tm=tm),
        out_shape=jax.ShapeDtypeStruct((m, n), F32),
        grid=(m // tm, n // tn),
        in_specs=[
            pl.BlockSpec((tm, width), lambda i, j: (i, 0)),
            pl.BlockSpec((tm, width), lambda i, j: (i, 1)),
            pl.BlockSpec((1, width), lambda i, j: (0, 0)),
            pl.BlockSpec(w_s.shape, lambda i, j: (0, 0, 0)),
            pl.BlockSpec((SG_CHUNK, SG_GROUPS), lambda i, j: (0, 0)),
            pl.BlockSpec((width, tn), lambda i, j: (0, j)),
            pl.BlockSpec((tm, tn), lambda i, j: (i, j)),
        ],
        out_specs=pl.BlockSpec((tm, tn), lambda i, j: (i, j)),
        scratch_shapes=[pltpu.VMEM((tm, width), BF16)],
        compiler_params=_params("parallel", "arbitrary"),
        name="sg_out",
    )(uv, uv, sg_norm.reshape(1, width), w_s.astype(BF16), b_s.T.astype(F32), w_o, res)


S5_L = 32
S5_W = S5_L * S5_GROUP_DIM


def _cexp(are, aim, dt, e):
    mag = jnp.exp(are * dt * e)
    ang = aim * dt * e
    return mag * jnp.cos(ang), mag * jnp.sin(ang)


def _s5_prep_kernel(arc_ref, aic_ref, arr_ref, air_ref, ldt_ref, btr_ref, bti_ref, ctr_ref, cti_ref,
                    k_ref, bm_ref, cm_ref, ap_ref, *, rev):
    hp = lax.Precision.HIGHEST
    ll, cg, w, p = S5_L, S5_GROUP_DIM, S5_W, S5_STATE
    sh = cg.bit_length() - 1
    dt = jnp.exp(ldt_ref[0])
    arc, aic = arc_ref[0], aic_ref[0]
    arr, air = arr_ref[0], air_ref[0]
    abr, abi = _cexp(arr, air, dt, 1.0)
    nr, ni = abr - 1.0, abi
    den = arr * arr + air * air
    cr, ci = (nr * arr + ni * air) / den, (ni * arr - nr * air) / den
    btr, bti = btr_ref[0], bti_ref[0]
    bbr, bbi = cr * btr - ci * bti, cr * bti + ci * btr
    lane = lax.broadcasted_iota(jnp.int32, (cg, w), 1)
    sel = jnp.where((lane & (cg - 1)) == lax.broadcasted_iota(jnp.int32, (cg, w), 0), 1.0, 0.0)
    cer = jnp.dot(ctr_ref[0], sel, precision=hp, preferred_element_type=F32)
    cei = jnp.dot(cti_ref[0], sel, precision=hp, preferred_element_type=F32)
    tl = (lax.broadcasted_iota(jnp.int32, (p, w), 1) >> sh).astype(F32)

    def cz(e):
        zr, zi = _cexp(arc, aic, dt, e)
        return cer * zr - cei * zi, cer * zi + cei * zr

    czr, czi = cz((ll - 1.0 - tl) if rev else tl)
    r = jnp.dot(bbr, czr, precision=hp, preferred_element_type=F32) - jnp.dot(bbi, czi, precision=hp,
                                                                                preferred_element_type=F32)
    lane_r = lax.broadcasted_iota(jnp.int32, (cg, w), 1)
    for s in range(ll):
        if rev:
            blk = jnp.where(lane_r < cg * (s + 1), pltpu.roll(r, (w - cg * (ll - 1 - s)) % w, 1), 0.0)
        else:
            blk = jnp.where(lane_r >= cg * s, pltpu.roll(r, cg * s, 1), 0.0)
        k_ref[0, s * cg:(s + 1) * cg, :] = blk.astype(k_ref.dtype)
    acr, aci = _cexp(arc, aic, dt, 1.0)
    c1r, c1i = czr * acr - czi * aci, czr * aci + czi * acr
    cm_ref[0, 0:p, :] = c1r.astype(cm_ref.dtype)
    cm_ref[0, p:2 * p, :] = (-c1i).astype(cm_ref.dtype)
    srow = (lax.broadcasted_iota(jnp.int32, (w, p), 0) >> sh).astype(F32)
    zr, zi = _cexp(arr, air, dt, srow if rev else (ll - 1.0 - srow))
    tbr, tbi = jnp.tile(bbr, (ll, 1)), jnp.tile(bbi, (ll, 1))
    bm_ref[0, :, 0:p] = (zr * tbr - zi * tbi).astype(bm_ref.dtype)
    bm_ref[0, :, p:2 * p] = (zr * tbi + zi * tbr).astype(bm_ref.dtype)
    ek = (ll << lax.broadcasted_iota(jnp.int32, (8, p), 0)).astype(F32)
    pr, pi = _cexp(arr, air, dt, ek)
    ap_ref[0, :, 0:p] = pr
    ap_ref[0, :, p:2 * p] = pi


def s5_prep(a_re, a_im, log_dt, b_re, b_im, c_re, c_im, rev):
    g, p = a_re.shape
    cg, w = S5_GROUP_DIM, S5_W
    col = lambda t: t.reshape(g, p, 1).astype(F32)
    row = lambda t: t.reshape(g, 1, p).astype(F32)
    tr = lambda t: jnp.swapaxes(t, 1, 2).astype(F32)
    spec = lambda s: pl.BlockSpec((1,) + s, lambda i: (i, 0, 0))
    return pl.pallas_call(
        functools.partial(_s5_prep_kernel, rev=rev),
        out_shape=(jax.ShapeDtypeStruct((g, w, w), BF16), jax.ShapeDtypeStruct((g, w, 2 * p), BF16),
                   jax.ShapeDtypeStruct((g, 2 * p, w), BF16), jax.ShapeDtypeStruct((g, 8, 2 * p), F32)),
        grid=(g,),
        in_specs=[spec((p, 1)), spec((p, 1)), spec((1, p)), spec((1, p)), spec((1, 1)),
                  spec((cg, p)), spec((cg, p)), spec((p, cg)), spec((p, cg))],
        out_specs=(spec((w, w)), spec((w, 2 * p)), spec((2 * p, w)), spec((8, 2 * p))),
        compiler_params=_params("parallel"),
        name="s5_prep",
    )(col(a_re), col(a_im), row(a_re), row(a_im), log_dt.reshape(g, 1, 1).astype(F32),
      tr(b_re), tr(b_im), tr(c_re), tr(c_im))


def _s5_chunk_scan(s, ap, nchunk, rev):
    n, w2 = s.shape
    p = w2 // 2
    m = lax.broadcasted_iota(jnp.int32, (n, w2), 0) & (nchunk - 1)
    lane = lax.broadcasted_iota(jnp.int32, (1, w2), 1)

    def shift(x, k):
        if rev:
            return jnp.where(m + k <= nchunk - 1, pltpu.roll(x, n - k, 0), 0.0)
        return jnp.where(m >= k, pltpu.roll(x, k, 0), 0.0)

    x = s
    k, lvl = 1, 0
    while k < nchunk:
        a = ap[lvl:lvl + 1, :]
        a1 = jnp.where(lane < p, a, pltpu.roll(a, p, 1))
        a2 = jnp.where(lane < p, -pltpu.roll(a, p, 1), a)
        xs = shift(x, k)
        x = x + a1 * xs + a2 * pltpu.roll(xs, p, 1)
        k, lvl = 2 * k, lvl + 1
    return shift(x, 1)


def _s5_main_kernel(u_ref, kf_ref, kb_ref, bf_ref, bb_ref, cf_ref, cb_ref, af_ref, ab_ref, y_ref, *, nchunk):
    u = u_ref[0]
    y = jnp.dot(u, kf_ref[0], preferred_element_type=F32) + jnp.dot(u, kb_ref[0], preferred_element_type=F32)
    for bm_ref, cm_ref, ap_ref, rev in ((bf_ref, cf_ref, af_ref, False), (bb_ref, cb_ref, ab_ref, True)):
        s = jnp.dot(u, bm_ref[0], preferred_element_type=F32)
        xin = _s5_chunk_scan(s, ap_ref[0], nchunk, rev)
        y = y + jnp.dot(xin.astype(BF16), cm_ref[0], preferred_element_type=F32)
    y_ref[0] = y.astype(y_ref.dtype)


def s5_main(u, prep_f, prep_b, nchunk):
    g, n, w = u.shape
    p2 = 2 * S5_STATE
    spec = lambda s: pl.BlockSpec((1,) + s, lambda i: (i, 0, 0))
    kf, bf, cf, af = prep_f
    kb, bb, cb, ab = prep_b
    return pl.pallas_call(
        functools.partial(_s5_main_kernel, nchunk=nchunk),
        out_shape=jax.ShapeDtypeStruct((g, n, w), BF16),
        grid=(g,),
        in_specs=[spec((n, w)), spec((w, w)), spec((w, w)), spec((w, p2)), spec((w, p2)),
                  spec((p2, w)), spec((p2, w)), spec((8, p2)), spec((8, p2))],
        out_specs=spec((n, w)),
        compiler_params=_params("parallel"),
        name="s5_main",
    )(u, kf, kb, bf, bb, cf, cb, af, ab)


def _s5_out_kernel(x_ref, g_ref, y_ref, d_ref, wa_ref, wb_ref, o_ref, inv_ref, acca_ref, accb_ref):
    k = pl.program_id(1)
    tk = y_ref.shape[1]

    @pl.when(k == 0)
    def _():
        x = x_ref[...]
        inv_ref[...] = jnp.broadcast_to(lax.rsqrt(jnp.mean(x * x, axis=-1, keepdims=True) + EPS), inv_ref.shape)
        acca_ref[...] = jnp.zeros_like(acca_ref)
        accb_ref[...] = jnp.zeros_like(accb_ref)

    xk = x_ref[:, pl.ds(pl.multiple_of(k * tk, tk), tk)]
    h = xk * inv_ref[:, 0:1] * g_ref[...]
    a = _gelu_tanh(y_ref[...].astype(F32) + d_ref[...] * h).astype(BF16)
    acca_ref[...] += jnp.dot(a, wa_ref[...], preferred_element_type=F32)
    accb_ref[...] += jnp.dot(a, wb_ref[...], preferred_element_type=F32)

    @pl.when(k == pl.num_programs(1) - 1)
    def _():
        o_ref[...] = x_ref[...] + acca_ref[...] * _sigmoid(accb_ref[...])


def s5_out(x, g, y, d_skip, w_glu, tm=512, tk=512):
    m, d = x.shape
    n = w_glu.shape[1] // 2
    return pl.pallas_call(
        _s5_out_kernel,
        out_shape=jax.ShapeDtypeStruct((m, n), F32),
        grid=(m // tm, d // tk),
        in_specs=[
            pl.BlockSpec((tm, d), lambda i, k: (i, 0)),
            pl.BlockSpec((1, tk), lambda i, k: (0, k)),
            pl.BlockSpec((tm, tk), lambda i, k: (i, k)),
            pl.BlockSpec((1, tk), lambda i, k: (0, k)),
            pl.BlockSpec((tk, n), lambda i, k: (k, 0)),
            pl.BlockSpec((tk, n), lambda i, k: (k, 1)),
        ],
        out_specs=pl.BlockSpec((tm, n), lambda i, k: (i, 0)),
        scratch_shapes=[pltpu.VMEM((tm, 128), F32), pltpu.VMEM((tm, n), F32), pltpu.VMEM((tm, n), F32)],
        compiler_params=_params("parallel", "arbitrary"),
        name="s5_out",
    )(x, g.reshape(1, d), y, d_skip.reshape(1, d), w_glu, w_glu)


def s5_mixer(x, gmix, a_re, a_im, log_dt, b_re, b_im, c_re, c_im, d_skip, w_glu, bsz, seq):
    m, d = x.shape
    groups = d // S5_GROUP_DIM
    nchunk = seq // S5_L
    h = rmsnorm(x, gmix, out_dtype=BF16)
    u = h.reshape(m // S5_L, S5_L, groups, S5_GROUP_DIM).transpose(2, 0, 1, 3).reshape(groups, m // S5_L, S5_W)
    prep_f = s5_prep(a_re[0], a_im[0], log_dt[0], b_re[0], b_im[0], c_re[0], c_im[0], rev=False)
    prep_b = s5_prep(a_re[1], a_im[1], log_dt[1], b_re[1], b_im[1], c_re[1], c_im[1], rev=True)
    y = s5_main(u, prep_f, prep_b, nchunk)
    y = y.reshape(groups, m // S5_L, S5_L, S5_GROUP_DIM).transpose(1, 2, 0, 3).reshape(m, d)
    return s5_out(x, gmix, y, d_skip, w_glu.astype(BF16))


def _trunk(x, p, w, bsz, seq):
    depth = w['norm_mix'].shape[0]
    bf = lambda t: t.astype(BF16)
    w_gu, w_down = bf(w['ffn_w_gu']), bf(w['ffn_w_down'])
    for i in range(depth):
        kind, j = i % N_MIXERS, i // N_MIXERS
        gmix = w['norm_mix'][i]
        if kind == 0:
            qkv = norm_matmul(x, gmix, bf(w['na_w_qkv'][j]), out_dtype=BF16)
            att = na_attention(qkv, w['na_rpb'][j], bsz, seq)
            x = matmul_res(att, bf(w['na_w_o'][j]), x)
        elif kind == 1:
            uv = norm_matmul(x, gmix, bf(w['sg_w_in'][j]), act="gelu", out_dtype=BF16)
            x = sg_out(uv, w['sg_norm'][j], w['sg_w_s'][j], w['sg_b_s'][j], bf(w['sg_w_o'][j]), x)
        elif kind == 2:
            x = gated_deltanet(x, gmix, w['gdn_w_in'][j], w['gdn_conv_w'][j], w['gdn_a_log'][j],
                               w['gdn_dt_bias'][j], w['gdn_out_norm'][j], w['gdn_w_o'][j], bsz, seq)
        else:
            x = s5_mixer(x, gmix, w['s5_a_re'][j], w['s5_a_im'][j], w['s5_log_dt'][j], w['s5_b_re'][j],
                         w['s5_b_im'][j], w['s5_c_re'][j], w['s5_c_im'][j], w['s5_d'][j], w['s5_w_glu'][j], bsz, seq)
        x = ffn(x, w['norm_ffn'][i], w_gu, w['ffn_conv_w'][i], w['ffn_conv_b'][i], w_down, i, seq)
        x = ple(x, p, i, w['norm_ple'][i], bf(w['ple_w_gate'][i]), bf(w['ple_w_proj'][i]))
    return x


def kernel(x_prompt, x_sample, p_prompt, p_sample, norm_mix, norm_ffn, norm_ple, final_norm, na_w_qkv, na_w_o, na_rpb, sg_w_in, sg_norm, sg_w_s, sg_b_s, sg_w_o, gdn_w_in, gdn_conv_w, gdn_a_log, gdn_dt_bias, gdn_out_norm, gdn_w_o, s5_a_re, s5_a_im, s5_log_dt, s5_b_re, s5_b_im, s5_c_re, s5_c_im, s5_d, s5_w_glu, ffn_w_gu, ffn_conv_w, ffn_conv_b, ffn_w_down, ple_w_proj, ple_w_gate):
    w = dict(norm_mix=norm_mix, norm_ffn=norm_ffn, norm_ple=norm_ple, final_norm=final_norm,
             na_w_qkv=na_w_qkv, na_w_o=na_w_o, na_rpb=na_rpb,
             sg_w_in=sg_w_in, sg_norm=sg_norm, sg_w_s=sg_w_s, sg_b_s=sg_b_s, sg_w_o=sg_w_o,
             gdn_w_in=gdn_w_in, gdn_conv_w=gdn_conv_w, gdn_a_log=gdn_a_log, gdn_dt_bias=gdn_dt_bias,
             gdn_out_norm=gdn_out_norm, gdn_w_o=gdn_w_o,
             s5_a_re=s5_a_re, s5_a_im=s5_a_im, s5_log_dt=s5_log_dt, s5_b_re=s5_b_re, s5_b_im=s5_b_im,
             s5_c_re=s5_c_re, s5_c_im=s5_c_im, s5_d=s5_d, s5_w_glu=s5_w_glu,
             ffn_w_gu=ffn_w_gu, ffn_conv_w=ffn_conv_w, ffn_conv_b=ffn_conv_b, ffn_w_down=ffn_w_down,
             ple_w_proj=ple_w_proj, ple_w_gate=ple_w_gate)
    b1, seq, d = x_prompt.shape
    b2 = x_sample.shape[0]
    bsz = b1 + b2
    depth = p_prompt.shape[0]
    x = (x_prompt.reshape(b1 * seq, d), x_sample.reshape(b2 * seq, d))
    p = (p_prompt.reshape(depth, b1 * seq, -1), p_sample.reshape(depth, b2 * seq, -1))
    x = _trunk(x, p, w, bsz, seq)
    y1 = rmsnorm(x, final_norm, row0=0, rows=b1 * seq).reshape(b1, seq, d)
    y2 = rmsnorm(x, final_norm, row0=b1 * seq, rows=b2 * seq).reshape(b2, seq, d)
    return (y1, y2)
```

```python
import functools
import math

import jax
import jax.numpy as jnp
import numpy as np
from jax import lax
from jax.experimental import pallas as pl
from jax.experimental.pallas import tpu as pltpu

F32 = jnp.float32
BF16 = jnp.bfloat16

EPS = 1e-6
NEG_INF = -1e30
GRID_W = 64
NA_HEADS = 16
NA_WIN_R = 8
NA_WIN_C = 16
SG_CHUNK = 128
SG_GROUPS = 16
GDN_QK_HEADS = 16
GDN_V_HEADS = 32
GDN_DK = 128
GDN_DV = 128
GDN_CHUNK = 64
S5_GROUP_DIM = 16
S5_STATE = 64
N_MIXERS = 4

VMEM_LIMIT_BYTES = 56 * 1024 * 1024
HALO = 16


def _params(*sem):
    return pltpu.CompilerParams(dimension_semantics=sem, vmem_limit_bytes=VMEM_LIMIT_BYTES)


def _rms(x, g):
    return x * lax.rsqrt(jnp.mean(x * x, axis=-1, keepdims=True) + EPS) * g


def _gelu_tanh(x):
    return 0.5 * x * (1.0 + jnp.tanh(math.sqrt(2.0 / math.pi) * (x + 0.044715 * (x * x * x))))


def _sigmoid(x):
    return 1.0 / (1.0 + jnp.exp(-x))


def _rmsnorm_kernel(x_ref, g_ref, o_ref):
    o_ref[...] = _rms(x_ref[...], g_ref[...]).astype(o_ref.dtype)


def rmsnorm(x, g, out_dtype=F32, tm=512, row0=0, rows=None):
    m, d = x.shape
    rows = m if rows is None else rows
    off = row0 // tm
    return pl.pallas_call(
        _rmsnorm_kernel,
        out_shape=jax.ShapeDtypeStruct((rows, d), out_dtype),
        grid=(rows // tm,),
        in_specs=[pl.BlockSpec((tm, d), lambda i: (i + off, 0)), pl.BlockSpec((1, d), lambda i: (0, 0))],
        out_specs=pl.BlockSpec((tm, d), lambda i: (i, 0)),
        compiler_params=_params("parallel"),
        name="rmsnorm",
    )(x, g.reshape(1, d))


def _as_pair(x, tm):
    if isinstance(x, tuple):
        a, b = x
        return a, b, a.shape[0] // tm, a.shape[0] + b.shape[0]
    return x, x, None, x.shape[0]


def _pair_specs(block, n1, col):
    if n1 is None:
        return [pl.BlockSpec(block, lambda i, j: (i, col(i, j))), pl.BlockSpec(block, lambda i, j: (0, 0))]
    return [pl.BlockSpec(block, lambda i, j: (jnp.minimum(i, n1 - 1), col(i, j))),
            pl.BlockSpec(block, lambda i, j: (jnp.maximum(i - n1, 0), col(i, j)))]


def _pick(a_ref, b_ref, n1):
    if n1 is None:
        return a_ref[...]
    return jnp.where(pl.program_id(0) < n1, a_ref[...], b_ref[...])


def _norm_matmul_kernel(xa_ref, xb_ref, g_ref, w_ref, o_ref, hn_ref, *, act, n1):
    @pl.when(pl.program_id(1) == 0)
    def _():
        hn_ref[...] = _rms(_pick(xa_ref, xb_ref, n1), g_ref[...]).astype(BF16)

    y = jnp.dot(hn_ref[...], w_ref[...], preferred_element_type=F32)
    if act == "gelu":
        y = _gelu_tanh(y)
    o_ref[...] = y.astype(o_ref.dtype)


def norm_matmul(x, g, w, act=None, out_dtype=F32, tm=512, tn=1024, col0=0, ncols=None):
    xa, xb, n1, m = _as_pair(x, tm)
    d = xa.shape[1]
    n = w.shape[1] - col0 if ncols is None else ncols
    tn = next(t for t in (tn, 512, 256, 128) if n % t == 0 and col0 % t == 0)
    cb = col0 // tn
    return pl.pallas_call(
        functools.partial(_norm_matmul_kernel, act=act, n1=n1),
        out_shape=jax.ShapeDtypeStruct((m, n), out_dtype),
        grid=(m // tm, n // tn),
        in_specs=_pair_specs((tm, d), n1, lambda i, j: 0) + [
            pl.BlockSpec((1, d), lambda i, j: (0, 0)),
            pl.BlockSpec((d, tn), lambda i, j: (0, cb + j)),
        ],
        out_specs=pl.BlockSpec((tm, tn), lambda i, j: (i, j)),
        scratch_shapes=[pltpu.VMEM((tm, d), BF16)],
        compiler_params=_params("parallel", "arbitrary"),
        name="norm_matmul",
    )(xa, xb, g.reshape(1, d), w)


def _matmul_res_kernel(a_ref, w_ref, ra_ref, rb_ref, o_ref, *, n1):
    o_ref[...] = _pick(ra_ref, rb_ref, n1) + jnp.dot(a_ref[...].astype(BF16), w_ref[...],
                                                     preferred_element_type=F32)


def matmul_res(a, w, res, tm=512, tn=1024):
    m, k = a.shape
    n = w.shape[1]
    ra, rb, n1, _ = _as_pair(res, tm)
    return pl.pallas_call(
        functools.partial(_matmul_res_kernel, n1=n1),
        out_shape=jax.ShapeDtypeStruct((m, n), F32),
        grid=(m // tm, n // tn),
        in_specs=[
            pl.BlockSpec((tm, k), lambda i, j: (i, 0)),
            pl.BlockSpec((k, tn), lambda i, j: (0, j)),
        ] + _pair_specs((tm, tn), n1, lambda i, j: j),
        out_specs=pl.BlockSpec((tm, tn), lambda i, j: (i, j)),
        compiler_params=_params("parallel", "arbitrary"),
        name="matmul_res",
    )(a, w, ra, rb)


def _ffn_kernel(x_ref, xp_ref, xn_ref, g_ref, wg_ref, wu_ref, cw_ref, cb_ref, wd_ref, o_ref, hn_ref, acc_ref,
                *, tm, seq):
    i = pl.program_id(0)
    j = pl.program_id(1)

    @pl.when(j == 0)
    def _():
        g = g_ref[...]
        prev_ok = jnp.where((i * tm) % seq != 0, 1.0, 0.0)
        next_ok = jnp.where(((i + 1) * tm) % seq != 0, 1.0, 0.0)
        hn_ref[0:HALO, :] = (_rms(xp_ref[...], g) * prev_ok).astype(BF16)
        hn_ref[HALO:HALO + tm, :] = _rms(x_ref[...], g).astype(BF16)
        hn_ref[HALO + tm:, :] = (_rms(xn_ref[...], g) * next_ok).astype(BF16)
        acc_ref[...] = jnp.zeros_like(acc_ref)

    rows = tm + 2 * HALO
    gate = jnp.dot(hn_ref[...], wg_ref[0], preferred_element_type=F32)
    up = jnp.dot(hn_ref[HALO:HALO + tm, :], wu_ref[0], preferred_element_type=F32)
    cw = cw_ref[...]
    g_prev = pltpu.roll(gate, 1, 0)[HALO:HALO + tm]
    g_next = pltpu.roll(gate, rows - 1, 0)[HALO:HALO + tm]
    gc = cw[0:1] * g_prev + cw[1:2] * gate[HALO:HALO + tm] + cw[2:3] * g_next + cb_ref[...]
    act = (gc * _sigmoid(gc) * up).astype(BF16)
    acc_ref[...] += jnp.dot(act, wd_ref[0], preferred_element_type=F32)

    @pl.when(j == pl.num_programs(1) - 1)
    def _():
        o_ref[...] = x_ref[...] + acc_ref[...]


def ffn(x, g, w_gu, conv_w, conv_b, w_down, layer, seq, tm=512, tf=512):
    m, d = x.shape
    f = w_down.shape[1]
    nf = f // tf
    hb = tm // HALO
    last = m // HALO - 1
    return pl.pallas_call(
        functools.partial(_ffn_kernel, tm=tm, seq=seq),
        out_shape=jax.ShapeDtypeStruct((m, d), F32),
        grid=(m // tm, nf),
        in_specs=[
            pl.BlockSpec((tm, d), lambda i, j: (i, 0)),
            pl.BlockSpec((HALO, d), lambda i, j: (jnp.maximum(i * hb - 1, 0), 0)),
            pl.BlockSpec((HALO, d), lambda i, j: (jnp.minimum((i + 1) * hb, last), 0)),
            pl.BlockSpec((1, d), lambda i, j: (0, 0)),
            pl.BlockSpec((1, d, tf), lambda i, j: (layer, 0, j)),
            pl.BlockSpec((1, d, tf), lambda i, j: (layer, 0, j + nf)),
            pl.BlockSpec((3, tf), lambda i, j: (0, j)),
            pl.BlockSpec((1, tf), lambda i, j: (0, j)),
            pl.BlockSpec((1, tf, d), lambda i, j: (layer, j, 0)),
        ],
        out_specs=pl.BlockSpec((tm, d), lambda i, j: (i, 0)),
        scratch_shapes=[pltpu.VMEM((tm + 2 * HALO, d), BF16), pltpu.VMEM((tm, d), F32)],
        compiler_params=_params("parallel", "arbitrary"),
        name="ffn",
    )(x, x, x, g.reshape(1, d), w_gu, w_gu, conv_w, conv_b.reshape(1, f), w_down)


def _ple_kernel(x_ref, pa_ref, pb_ref, g_ref, wg_ref, wp_ref, o_ref, *, n1):
    x = x_ref[...]
    hn = _rms(x, g_ref[...]).astype(BF16)
    gate = _sigmoid(jnp.dot(hn, wg_ref[...], preferred_element_type=F32))
    p = jnp.where(pl.program_id(0) < n1, pa_ref[0], pb_ref[0])
    proj = jnp.dot(p.astype(BF16), wp_ref[...], preferred_element_type=F32)
    o_ref[...] = x + gate * proj


def ple(x, p, layer, g, w_gate, w_proj, tm=512):
    m, d = x.shape
    pa, pb = p if isinstance(p, tuple) else (p, p)
    n1 = pa.shape[1] // tm
    pd = pa.shape[2]
    return pl.pallas_call(
        functools.partial(_ple_kernel, n1=n1),
        out_shape=jax.ShapeDtypeStruct((m, d), F32),
        grid=(m // tm,),
        in_specs=[
            pl.BlockSpec((tm, d), lambda i: (i, 0)),
            pl.BlockSpec((1, tm, pd), lambda i: (layer, jnp.minimum(i, n1 - 1), 0)),
            pl.BlockSpec((1, tm, pd), lambda i: (layer, jnp.maximum(i - n1, 0), 0)),
            pl.BlockSpec((1, d), lambda i: (0, 0)),
            pl.BlockSpec((d, d), lambda i: (0, 0)),
            pl.BlockSpec((pd, d), lambda i: (0, 0)),
        ],
        out_specs=pl.BlockSpec((tm, d), lambda i: (i, 0)),
        compiler_params=_params("parallel"),
        name="ple",
    )(x, pa, pb, g.reshape(1, d), w_gate, w_proj)


GDN_BLOCK = 256


def _conv_in_kernel(x_ref, xp_ref, xn_ref, g_ref, w_ref, cw_ref, o_ref, hn_ref, *, tm, seq, nq, nqk, scale):
    i = pl.program_id(0)
    j = pl.program_id(1)

    @pl.when(j == 0)
    def _():
        g = g_ref[...]
        prev_ok = jnp.where((i * tm) % seq != 0, 1.0, 0.0)
        next_ok = jnp.where(((i + 1) * tm) % seq != 0, 1.0, 0.0)
        hn_ref[0:HALO, :] = (_rms(xp_ref[...], g) * prev_ok).astype(BF16)
        hn_ref[HALO:HALO + tm, :] = _rms(x_ref[...], g).astype(BF16)
        hn_ref[HALO + tm:, :] = (_rms(xn_ref[...], g) * next_ok).astype(BF16)

    cw = cw_ref[...]
    w = w_ref[...]
    tn = w.shape[1]
    half = tm // 2
    rows = half + 2 * HALO
    for lo in (0, half):
        y = jnp.dot(hn_ref[lo:lo + rows, :], w, preferred_element_type=F32)
        y_prev = pltpu.roll(y, 1, 0)[HALO:HALO + half]
        y_next = pltpu.roll(y, rows - 1, 0)[HALO:HALO + half]
        c = cw[0:1] * y_prev + cw[1:2] * y[HALO:HALO + half] + cw[2:3] * y_next
        c = c * _sigmoid(c)
        for s in range(tn // 128):
            cs = c[:, s * 128:(s + 1) * 128]
            inv = lax.rsqrt(jnp.sum(cs * cs, axis=-1, keepdims=True) + EPS)
            f = jnp.where(j < nq, inv * scale, jnp.where(j < nqk, inv, 1.0))
            o_ref[lo:lo + half, s * 128:(s + 1) * 128] = (cs * f).astype(o_ref.dtype)


def conv_in(x, g, w, n, conv_w, seq, n_q, n_qk, scale, tm=512, tn=512):
    m, d = x.shape
    hb = tm // HALO
    last = m // HALO - 1
    return pl.pallas_call(
        functools.partial(_conv_in_kernel, tm=tm, seq=seq, nq=n_q // tn, nqk=n_qk // tn, scale=scale),
        out_shape=jax.ShapeDtypeStruct((m, n), BF16),
        grid=(m // tm, n // tn),
        in_specs=[
            pl.BlockSpec((tm, d), lambda i, j: (i, 0)),
            pl.BlockSpec((HALO, d), lambda i, j: (jnp.maximum(i * hb - 1, 0), 0)),
            pl.BlockSpec((HALO, d), lambda i, j: (jnp.minimum((i + 1) * hb, last), 0)),
            pl.BlockSpec((1, d), lambda i, j: (0, 0)),
            pl.BlockSpec((d, tn), lambda i, j: (0, j)),
            pl.BlockSpec((3, tn), lambda i, j: (0, j)),
        ],
        out_specs=pl.BlockSpec((tm, tn), lambda i, j: (i, j)),
        scratch_shapes=[pltpu.VMEM((tm + 2 * HALO, d), BF16)],
        compiler_params=_params("parallel", "arbitrary"),
        name="gdn_conv_in",
    )(x, x, x, g.reshape(1, d), w, conv_w)


def _gdn_gates_kernel(x_ref, g_ref, w_ref, alog_ref, bias_ref, isg_ref, o_ref):
    hn = _rms(x_ref[...], g_ref[...]).astype(BF16)
    y = jnp.dot(hn, w_ref[...], preferred_element_type=F32)
    t = y + bias_ref[...]
    softplus = jnp.maximum(t, 0.0) + jnp.log1p(jnp.exp(-jnp.abs(t)))
    isg = isg_ref[...] > 0.5
    base = jnp.where(isg, -jnp.exp(alog_ref[...]) * softplus, 0.0)
    tm, n = base.shape
    lc = GDN_CHUNK
    pos = lax.broadcasted_iota(jnp.int32, (tm, n), 0) & (lc - 1)
    pre, suf = base, base
    k = 1
    while k < lc:
        pre = pre + jnp.where(pos >= k, pltpu.roll(pre, k, 0), 0.0)
        suf = suf + jnp.where(pos + k <= lc - 1, pltpu.roll(suf, tm - k, 0), 0.0)
        k *= 2
    is_bwd = lax.broadcasted_iota(jnp.int32, (1, n), 1) >= n // 2
    o_ref[:, 0:n] = jnp.where(isg, jnp.where(is_bwd, suf, pre), _sigmoid(y))
    o_ref[:, n:2 * n] = pre + suf - base


def gdn_gates(x, g, w, col0, a_log, dt_bias, tm=512):
    m, d = x.shape
    n = w.shape[1] - col0
    nh = a_log.shape[-1]
    zeros = jnp.zeros((2, 1, nh), F32)
    arrange = lambda t: jnp.concatenate([t.reshape(2, 1, nh).astype(F32), zeros], axis=1).reshape(1, n)
    isg = jnp.concatenate([jnp.ones((2, 1, nh), F32), zeros], axis=1).reshape(1, n)
    return pl.pallas_call(
        _gdn_gates_kernel,
        out_shape=jax.ShapeDtypeStruct((m, 2 * n), F32),
        grid=(m // tm,),
        in_specs=[
            pl.BlockSpec((tm, d), lambda i: (i, 0)),
            pl.BlockSpec((1, d), lambda i: (0, 0)),
            pl.BlockSpec((d, n), lambda i: (0, col0 // n)),
            pl.BlockSpec((1, n), lambda i: (0, 0)),
            pl.BlockSpec((1, n), lambda i: (0, 0)),
            pl.BlockSpec((1, n), lambda i: (0, 0)),
        ],
        out_specs=pl.BlockSpec((tm, 2 * n), lambda i: (i, 0)),
        compiler_params=_params("parallel"),
        name="gdn_gates",
    )(x, g.reshape(1, d), w, arrange(a_log), arrange(dt_bias), isg)


def _gdn_chains(q_ref, k_ref, v_ref, gc_ref, gr_ref, o_ref, rev):
    c_sz, lc = GDN_BLOCK, GDN_CHUNK
    d = 1 if rev else 0
    ii = lax.broadcasted_iota(jnp.int32, (c_sz, c_sz), 0)
    jj = lax.broadcasted_iota(jnp.int32, (c_sz, c_sz), 1)
    sh = lc.bit_length() - 1
    same = (ii >> sh) == (jj >> sh)
    incl = jnp.logical_and(same, (jj >= ii) if rev else (jj <= ii))
    eye = ii == jj
    q = q_ref[...]
    k = k_ref[...]
    kf = k.astype(F32)
    qf = q.astype(F32)
    gc = gc_ref[0, 0][:, 8 * d:8 * d + 8]
    gr = gr_ref[0, 0][8 * d:8 * d + 8, :]
    gam_c = gc[:, 0:2]
    gam_r = gr[0:2]
    end_r = gr[4:6]
    gram = lax.dot_general(k, k, (((1,), (1,)), ((), ())), preferred_element_type=F32)
    qk = lax.dot_general(q, k, (((1,), (1,)), ((), ())), preferred_element_type=F32)
    chains = []
    for hs in range(2):
        gcol = gam_c[:, hs:hs + 1]
        grow = gam_r[hs:hs + 1, :]
        bcol = gc[:, 2 + hs:3 + hs]
        dec = jnp.exp(jnp.where(incl, gcol - grow, NEG_INF))
        e_g = jnp.exp(gcol)
        vh = v_ref[:, hs * GDN_DV:(hs + 1) * GDN_DV].astype(F32)
        chains.append(dict(
            n=(-(gram * jnp.where(eye, 0.0, dec)) * bcol).astype(BF16),
            x=jnp.concatenate([kf * (bcol * e_g), vh * bcol], axis=1),
            attn=(qk * dec).astype(BF16),
            qe=qf * e_g,
            kd=kf * jnp.exp(gc[:, 4 + hs:5 + hs] - gcol),
            end=end_r[hs:hs + 1, :],
            o_ref=o_ref, hs=hs, idx=2 * d + hs, rev=rev))
    return chains


def _gdn_scan_kernel(qf_ref, kf_ref, vf_ref, gcf_ref, grf_ref,
                     qb_ref, kb_ref, vb_ref, gcb_ref, grb_ref, of_ref, ob_ref, s_ref):
    @pl.when(pl.program_id(2) == 0)
    def _():
        s_ref[...] = jnp.zeros_like(s_ref)

    c_sz, lc = GDN_BLOCK, GDN_CHUNK
    nchunk = c_sz // lc
    width = GDN_DK + GDN_DV
    chains = (_gdn_chains(qf_ref, kf_ref, vf_ref, gcf_ref, grf_ref, of_ref, False)
              + _gdn_chains(qb_ref, kb_ref, vb_ref, gcb_ref, grb_ref, ob_ref, True))
    nlev = lc.bit_length() - 1
    for lvl in range(nlev):
        for ch in chains:
            nb = ch['n']
            xb = ch['x'].astype(BF16)
            if lvl < nlev - 1:
                r = jnp.dot(nb, jnp.concatenate([xb, nb], axis=1), preferred_element_type=F32)
                ch['x'] = ch['x'] + r[:, :width]
                ch['n'] = r[:, width:].astype(BF16)
            else:
                ch['x'] = ch['x'] + jnp.dot(nb, xb, preferred_element_type=F32)
    irow = lax.broadcasted_iota(jnp.int32, (c_sz, 1), 0) >> nlev
    for ch in chains:
        wub = ch['x'].astype(BF16)
        awu = jnp.dot(ch['attn'], wub, preferred_element_type=F32)
        ch['qeff'] = (ch['qe'] - awu[:, :GDN_DK]).astype(BF16)
        ch['o_in'] = awu[:, GDN_DK:]
        ch['kwu'] = [lax.dot_general(jnp.where(irow == c, ch['kd'], 0.0).astype(BF16), wub,
                                     (((0,), (0,)), ((), ())), preferred_element_type=F32)
                     for c in range(nchunk)]
        ch['s'] = s_ref[ch['idx']]
    for step in range(nchunk):
        for ch in chains:
            c = nchunk - 1 - step if ch['rev'] else step
            r0 = c * lc
            s = ch['s']
            sb = s.astype(BF16)
            o_c = ch['o_in'][r0:r0 + lc] + jnp.dot(ch['qeff'][r0:r0 + lc], sb, preferred_element_type=F32)
            ch['o_ref'][r0:r0 + lc, ch['hs'] * GDN_DV:(ch['hs'] + 1) * GDN_DV] = o_c.astype(ch['o_ref'].dtype)
            kwu = ch['kwu'][c]
            e_end = jnp.exp(ch['end'][:, r0:r0 + 1])
            ch['s'] = (e_end * s - jnp.dot(kwu[:, :GDN_DK].astype(BF16), sb, preferred_element_type=F32)
                       + kwu[:, GDN_DK:])
    for ch in chains:
        s_ref[ch['idx']] = ch['s']


def gdn_scan(qkv, gcol, grow, bsz, seq):
    c_sz = GDN_BLOCK
    nb = seq // c_sz
    hq = GDN_QK_HEADS
    kcol = hq
    vcol = (2 * hq * GDN_DK) // (2 * GDN_DV)
    fwd = lambda b, h, c: c
    bwd = lambda b, h, c: nb - 1 - c

    def specs(pos):
        return [
            pl.BlockSpec((c_sz, GDN_DK), lambda b, h, c: (b * nb + pos(b, h, c), h)),
            pl.BlockSpec((c_sz, GDN_DK), lambda b, h, c: (b * nb + pos(b, h, c), kcol + h)),
            pl.BlockSpec((c_sz, 2 * GDN_DV), lambda b, h, c: (b * nb + pos(b, h, c), vcol + h)),
            pl.BlockSpec((1, 1, c_sz, 16), lambda b, h, c: (b, h, pos(b, h, c), 0)),
            pl.BlockSpec((1, 1, 16, c_sz), lambda b, h, c: (b, h, 0, pos(b, h, c))),
        ]

    out = jax.ShapeDtypeStruct((bsz * seq, GDN_V_HEADS * GDN_DV), BF16)
    return pl.pallas_call(
        _gdn_scan_kernel,
        out_shape=(out, out),
        grid=(bsz, hq, nb),
        in_specs=specs(fwd) + specs(bwd),
        out_specs=(
            pl.BlockSpec((c_sz, 2 * GDN_DV), lambda b, h, c: (b * nb + c, h)),
            pl.BlockSpec((c_sz, 2 * GDN_DV), lambda b, h, c: (b * nb + nb - 1 - c, h)),
        ),
        scratch_shapes=[pltpu.VMEM((4, GDN_DK, GDN_DV), F32)],
        compiler_params=_params("parallel", "parallel", "arbitrary"),
        name="gdn_scan",
    )(qkv, qkv, qkv, gcol, grow, qkv, qkv, qkv, gcol, grow)


def _gdn_out_kernel(of_ref, ob_ref, z_ref, gn_ref, w_ref, r_ref, o_ref):
    @pl.when(pl.program_id(1) == 0)
    def _():
        o_ref[...] = r_ref[...]

    gn = gn_ref[...]
    parts = []
    for h in range(of_ref.shape[1] // GDN_DV):
        sl = slice(h * GDN_DV, (h + 1) * GDN_DV)
        o = of_ref[:, sl].astype(F32) + ob_ref[:, sl].astype(F32)
        z = z_ref[:, sl].astype(F32)
        parts.append((_rms(o, gn) * (z * _sigmoid(z))).astype(BF16))
    o_ref[...] += jnp.dot(jnp.concatenate(parts, axis=1), w_ref[...], preferred_element_type=F32)


def gdn_out(o_f, o_b, z, out_norm, w_o, res, tm=512, tk=1024):
    m, k = o_f.shape
    n = w_o.shape[1]
    return pl.pallas_call(
        _gdn_out_kernel,
        out_shape=jax.ShapeDtypeStruct((m, n), F32),
        grid=(m // tm, k // tk),
        in_specs=[
            pl.BlockSpec((tm, tk), lambda i, j: (i, j)),
            pl.BlockSpec((tm, tk), lambda i, j: (i, j)),
            pl.BlockSpec((tm, tk), lambda i, j: (i, j)),
            pl.BlockSpec((1, GDN_DV), lambda i, j: (0, 0)),
            pl.BlockSpec((tk, n), lambda i, j: (j, 0)),
            pl.BlockSpec((tm, n), lambda i, j: (i, 0)),
        ],
        out_specs=pl.BlockSpec((tm, n), lambda i, j: (i, 0)),
        compiler_params=_params("parallel", "arbitrary"),
        name="gdn_out",
    )(o_f, o_b, z, out_norm.reshape(1, GDN_DV), w_o, res)


def gated_deltanet(x, gmix, w_in, conv_w, a_log, dt_bias, out_norm, w_o, bsz, seq):
    qk_w = GDN_QK_HEADS * GDN_DK
    v_w = GDN_V_HEADS * GDN_DV
    cw = 2 * qk_w + v_w
    w_in = w_in.astype(BF16)
    qkv = conv_in(x, gmix, w_in, cw, conv_w, seq, qk_w, 2 * qk_w, GDN_DK ** -0.5)
    z = norm_matmul(x, gmix, w_in, out_dtype=BF16, col0=cw, ncols=v_w)
    gb = gdn_gates(x, gmix, w_in, cw + v_w, a_log, dt_bias)
    nab = gb.shape[1] // 2
    ab = gb[:, :nab].reshape(bsz, seq, 2, 2, GDN_QK_HEADS, 2).transpose(0, 4, 1, 2, 3, 5)
    tot = gb[:, nab:].reshape(bsz, seq, 2, 2, GDN_QK_HEADS, 2)[:, :, :, 0].transpose(0, 3, 1, 2, 4)[..., None, :]
    gcol = jnp.concatenate([ab, tot, jnp.zeros_like(tot)], axis=4).reshape(bsz, GDN_QK_HEADS, seq, 16)
    grow = jnp.swapaxes(gcol, 2, 3)
    o_f, o_b = gdn_scan(qkv, gcol, grow, bsz, seq)
    return gdn_out(o_f, o_b, z, out_norm, w_o.astype(BF16), x)


def _na_bias_table(rpb, rows):
    win_r = min(NA_WIN_R, rows)
    cols = np.arange(GRID_W)
    col_start = np.clip(cols - NA_WIN_C // 2, 0, GRID_W - NA_WIN_C)
    col_valid = (cols[None, :] >= col_start[:, None]) & (cols[None, :] < col_start[:, None] + NA_WIN_C)
    dc_idx = np.clip(cols[None, :] - cols[:, None] + NA_WIN_C - 1, 0, 2 * NA_WIN_C - 2)
    bias_c = jnp.where(col_valid, rpb[:, :, dc_idx].astype(F32), NEG_INF)
    dr = np.arange(NA_WIN_R)[:, None] + np.arange(win_r)[None, :]
    tab = bias_c[:, dr]
    return jnp.transpose(tab, (0, 1, 3, 2, 4)).reshape(rpb.shape[0], NA_WIN_R, GRID_W, win_r * GRID_W)


def _na_kernel(q_ref, k_ref, v_ref, b_ref, o_ref, *, rows, win_r, scale):
    wk = win_r * GRID_W

    group = 16 if rows % 16 == 0 else 1

    def body(it, carry):
        rs = [it * group + i for i in range(group)]
        r0s = [jnp.clip(r - win_r // 2, 0, rows - win_r) for r in rs]
        scores = []
        for r, r0 in zip(rs, r0s):
            q = q_ref[pl.ds(pl.multiple_of(r * GRID_W, GRID_W), GRID_W), :]
            kw = k_ref[pl.ds(pl.multiple_of(r0 * GRID_W, GRID_W), wk), :]
            s = lax.dot_general(q, kw, (((1,), (1,)), ((), ())), preferred_element_type=F32) * scale
            scores.append(s + b_ref[0, r0 - r + NA_WIN_R - 1])
        probs, dens = [], []
        for s in scores:
            p = jnp.exp(s - jnp.max(s, axis=-1, keepdims=True))
            dens.append(jnp.sum(p, axis=-1, keepdims=True))
            probs.append(p.astype(BF16))
        for r, r0, p, den in zip(rs, r0s, probs, dens):
            vw = v_ref[pl.ds(pl.multiple_of(r0 * GRID_W, GRID_W), wk), :]
            o = jnp.dot(p, vw, preferred_element_type=F32) / den
            o_ref[pl.ds(pl.multiple_of(r * GRID_W, GRID_W), GRID_W), :] = o.astype(o_ref.dtype)
        return carry

    lax.fori_loop(0, rows // group, body, 0)


def na_attention(qkv, rpb, bsz, seq):
    d = qkv.shape[1] // 3
    dh = d // NA_HEADS
    rows = seq // GRID_W
    win_r = min(NA_WIN_R, rows)
    table = _na_bias_table(rpb, rows)
    blk = lambda off: pl.BlockSpec((seq, dh), lambda b, h: (b, off + h))
    return pl.pallas_call(
        functools.partial(_na_kernel, rows=rows, win_r=win_r, scale=dh ** -0.5),
        out_shape=jax.ShapeDtypeStruct((bsz * seq, d), BF16),
        grid=(bsz, NA_HEADS),
        in_specs=[blk(0), blk(NA_HEADS), blk(2 * NA_HEADS),
                  pl.BlockSpec((1, NA_WIN_R, GRID_W, win_r * GRID_W), lambda b, h: (h, 0, 0, 0))],
        out_specs=blk(0),
        compiler_params=_params("parallel", "parallel"),
        name="na_attention",
    )(qkv, qkv, qkv, table)


def _sg_out_kernel(u_ref, v_ref, gn_ref, ws_ref, bs_ref, w_ref, r_ref, o_ref, a_ref, *, tm):
    @pl.when(pl.program_id(1) == 0)
    def _():
        vn = _rms(v_ref[...].astype(F32), gn_ref[...]).astype(BF16)
        bs = bs_ref[...]
        gd = vn.shape[1] // SG_GROUPS
        for c in range(tm // SG_CHUNK):
            rs = slice(c * SG_CHUNK, (c + 1) * SG_CHUNK)
            for g in range(SG_GROUPS):
                cs = slice(g * gd, (g + 1) * gd)
                mixed = jnp.dot(ws_ref[g], vn[rs, cs], preferred_element_type=F32) + bs[:, g:g + 1]
                a_ref[rs, cs] = (u_ref[rs, cs].astype(F32) * mixed).astype(BF16)

    o_ref[...] = r_ref[...] + jnp.dot(a_ref[...], w_ref[...], preferred_element_type=F32)


def sg_out(uv, sg_norm, w_s, b_s, w_o, res, tm=512, tn=1024):
    m = uv.shape[0]
    width = uv.shape[1] // 2
    n = w_o.shape[1]
    return pl.pallas_call(
        functools.partial(_sg_out_kernel, tm=tm),
        out_shape=jax.ShapeDtypeStruct((m, n), F32),
        grid=(m // tm, n // tn),
        in_specs=[
            pl.BlockSpec((tm, width), lambda i, j: (i, 0)),
            pl.BlockSpec((tm, width), lambda i, j: (i, 1)),
            pl.BlockSpec((1, width), lambda i, j: (0, 0)),
            pl.BlockSpec(w_s.shape, lambda i, j: (0, 0, 0)),
            pl.BlockSpec((SG_CHUNK, SG_GROUPS), lambda i, j: (0, 0)),
            pl.BlockSpec((width, tn), lambda i, j: (0, j)),
            pl.BlockSpec((tm, tn), lambda i, j: (i, j)),
        ],
        out_specs=pl.BlockSpec((tm, tn), lambda i, j: (i, j)),
        scratch_shapes=[pltpu.VMEM((tm, width), BF16)],
        compiler_params=_params("parallel", "arbitrary"),
        name="sg_out",
    )(uv, uv, sg_norm.reshape(1, width), w_s.astype(BF16), b_s.T.astype(F32), w_o, res)


S5_L = 32
S5_W = S5_L * S5_GROUP_DIM


def _cexp(are, aim, dt, e):
    mag = jnp.exp(are * dt * e)
    ang = aim * dt * e
    return mag * jnp.cos(ang), mag * jnp.sin(ang)


def _s5_prep_kernel(arc_ref, aic_ref, arr_ref, air_ref, ldt_ref, btr_ref, bti_ref, ctr_ref, cti_ref,
                    k_ref, bm_ref, cm_ref, ap_ref, *, rev):
    hp = lax.Precision.HIGHEST
    ll, cg, w, p = S5_L, S5_GROUP_DIM, S5_W, S5_STATE
    sh = cg.bit_length() - 1
    dt = jnp.exp(ldt_ref[0])
    arc, aic = arc_ref[0], aic_ref[0]
    arr, air = arr_ref[0], air_ref[0]
    abr, abi = _cexp(arr, air, dt, 1.0)
    nr, ni = abr - 1.0, abi
    den = arr * arr + air * air
    cr, ci = (nr * arr + ni * air) / den, (ni * arr - nr * air) / den
    btr, bti = btr_ref[0], bti_ref[0]
    bbr, bbi = cr * btr - ci * bti, cr * bti + ci * btr
    lane = lax.broadcasted_iota(jnp.int32, (cg, w), 1)
    sel = jnp.where((lane & (cg - 1)) == lax.broadcasted_iota(jnp.int32, (cg, w), 0), 1.0, 0.0)
    cer = jnp.dot(ctr_ref[0], sel, precision=hp, preferred_element_type=F32)
    cei = jnp.dot(cti_ref[0], sel, precision=hp, preferred_element_type=F32)
    tl = (lax.broadcasted_iota(jnp.int32, (p, w), 1) >> sh).astype(F32)

    def cz(e):
        zr, zi = _cexp(arc, aic, dt, e)
        return cer * zr - cei * zi, cer * zi + cei * zr

    czr, czi = cz((ll - 1.0 - tl) if rev else tl)
    r = jnp.dot(bbr, czr, precision=hp, preferred_element_type=F32) - jnp.dot(bbi, czi, precision=hp,
                                                                                preferred_element_type=F32)
    lane_r = lax.broadcasted_iota(jnp.int32, (cg, w), 1)
    for s in range(ll):
        if rev:
            blk = jnp.where(lane_r < cg * (s + 1), pltpu.roll(r, (w - cg * (ll - 1 - s)) % w, 1), 0.0)
        else:
            blk = jnp.where(lane_r >= cg * s, pltpu.roll(r, cg * s, 1), 0.0)
        k_ref[0, s * cg:(s + 1) * cg, :] = blk.astype(k_ref.dtype)
    acr, aci = _cexp(arc, aic, dt, 1.0)
    c1r, c1i = czr * acr - czi * aci, czr * aci + czi * acr
    cm_ref[0, 0:p, :] = c1r.astype(cm_ref.dtype)
    cm_ref[0, p:2 * p, :] = (-c1i).astype(cm_ref.dtype)
    srow = (lax.broadcasted_iota(jnp.int32, (w, p), 0) >> sh).astype(F32)
    zr, zi = _cexp(arr, air, dt, srow if rev else (ll - 1.0 - srow))
    tbr, tbi = jnp.tile(bbr, (ll, 1)), jnp.tile(bbi, (ll, 1))
    bm_ref[0, :, 0:p] = (zr * tbr - zi * tbi).astype(bm_ref.dtype)
    bm_ref[0, :, p:2 * p] = (zr * tbi + zi * tbr).astype(bm_ref.dtype)
    ek = (ll << lax.broadcasted_iota(jnp.int32, (8, p), 0)).astype(F32)
    pr, pi = _cexp(arr, air, dt, ek)
    ap_ref[0, :, 0:p] = pr
    ap_ref[0, :, p:2 * p] = pi


def s5_prep(a_re, a_im, log_dt, b_re, b_im, c_re, c_im, rev):
    g, p = a_re.shape
    cg, w = S5_GROUP_DIM, S5_W
    col = lambda t: t.reshape(g, p, 1).astype(F32)
    row = lambda t: t.reshape(g, 1, p).astype(F32)
    tr = lambda t: jnp.swapaxes(t, 1, 2).astype(F32)
    spec = lambda s: pl.BlockSpec((1,) + s, lambda i: (i, 0, 0))
    return pl.pallas_call(
        functools.partial(_s5_prep_kernel, rev=rev),
        out_shape=(jax.ShapeDtypeStruct((g, w, w), BF16), jax.ShapeDtypeStruct((g, w, 2 * p), BF16),
                   jax.ShapeDtypeStruct((g, 2 * p, w), BF16), jax.ShapeDtypeStruct((g, 8, 2 * p), F32)),
        grid=(g,),
        in_specs=[spec((p, 1)), spec((p, 1)), spec((1, p)), spec((1, p)), spec((1, 1)),
                  spec((cg, p)), spec((cg, p)), spec((p, cg)), spec((p, cg))],
        out_specs=(spec((w, w)), spec((w, 2 * p)), spec((2 * p, w)), spec((8, 2 * p))),
        compiler_params=_params("parallel"),
        name="s5_prep",
    )(col(a_re), col(a_im), row(a_re), row(a_im), log_dt.reshape(g, 1, 1).astype(F32),
      tr(b_re), tr(b_im), tr(c_re), tr(c_im))


def _s5_chunk_scan(s, ap, nchunk, rev):
    n, w2 = s.shape
    p = w2 // 2
    m = lax.broadcasted_iota(jnp.int32, (n, w2), 0) & (nchunk - 1)
    lane = lax.broadcasted_iota(jnp.int32, (1, w2), 1)

    def shift(x, k):
        if rev:
            return jnp.where(m + k <= nchunk - 1, pltpu.roll(x, n - k, 0), 0.0)
        return jnp.where(m >= k, pltpu.roll(x, k, 0), 0.0)

    x = s
    k, lvl = 1, 0
    while k < nchunk:
        a = ap[lvl:lvl + 1, :]
        a1 = jnp.where(lane < p, a, pltpu.roll(a, p, 1))
        a2 = jnp.where(lane < p, -pltpu.roll(a, p, 1), a)
        xs = shift(x, k)
        x = x + a1 * xs + a2 * pltpu.roll(xs, p, 1)
        k, lvl = 2 * k, lvl + 1
    return shift(x, 1)


def _s5_main_kernel(u_ref, kf_ref, kb_ref, bf_ref, bb_ref, cf_ref, cb_ref, af_ref, ab_ref, y_ref, *, nchunk):
    u = u_ref[0]
    y = jnp.dot(u, kf_ref[0], preferred_element_type=F32) + jnp.dot(u, kb_ref[0], preferred_element_type=F32)
    for bm_ref, cm_ref, ap_ref, rev in ((bf_ref, cf_ref, af_ref, False), (bb_ref, cb_ref, ab_ref, True)):
        s = jnp.dot(u, bm_ref[0], preferred_element_type=F32)
        xin = _s5_chunk_scan(s, ap_ref[0], nchunk, rev)
        y = y + jnp.dot(xin.astype(BF16), cm_ref[0], preferred_element_type=F32)
    y_ref[0] = y.astype(y_ref.dtype)


def s5_main(u, prep_f, prep_b, nchunk):
    g, n, w = u.shape
    p2 = 2 * S5_STATE
    spec = lambda s: pl.BlockSpec((1,) + s, lambda i: (i, 0, 0))
    kf, bf, cf, af = prep_f
    kb, bb, cb, ab = prep_b
    return pl.pallas_call(
        functools.partial(_s5_main_kernel, nchunk=nchunk),
        out_shape=jax.ShapeDtypeStruct((g, n, w), BF16),
        grid=(g,),
        in_specs=[spec((n, w)), spec((w, w)), spec((w, w)), spec((w, p2)), spec((w, p2)),
                  spec((p2, w)), spec((p2, w)), spec((8, p2)), spec((8, p2))],
        out_specs=spec((n, w)),
        compiler_params=_params("parallel"),
        name="s5_main",
    )(u, kf, kb, bf, bb, cf, cb, af, ab)


def _s5_out_kernel(x_ref, g_ref, y_ref, d_ref, wa_ref, wb_ref, o_ref, inv_ref, acca_ref, accb_ref):
    k = pl.program_id(1)
    tk = y_ref.shape[1]

    @pl.when(k == 0)
    def _():
        x = x_ref[...]
        inv_ref[...] = jnp.broadcast_to(lax.rsqrt(jnp.mean(x * x, axis=-1, keepdims=True) + EPS), inv_ref.shape)
        acca_ref[...] = jnp.zeros_like(acca_ref)
        accb_ref[...] = jnp.zeros_like(accb_ref)

    xk = x_ref[:, pl.ds(pl.multiple_of(k * tk, tk), tk)]
    h = xk * inv_ref[:, 0:1] * g_ref[...]
    a = _gelu_tanh(y_ref[...].astype(F32) + d_ref[...] * h).astype(BF16)
    acca_ref[...] += jnp.dot(a, wa_ref[...], preferred_element_type=F32)
    accb_ref[...] += jnp.dot(a, wb_ref[...], preferred_element_type=F32)

    @pl.when(k == pl.num_programs(1) - 1)
    def _():
        o_ref[...] = x_ref[...] + acca_ref[...] * _sigmoid(accb_ref[...])


def s5_out(x, g, y, d_skip, w_glu, tm=512, tk=512):
    m, d = x.shape
    n = w_glu.shape[1] // 2
    return pl.pallas_call(
        _s5_out_kernel,
        out_shape=jax.ShapeDtypeStruct((m, n), F32),
        grid=(m // tm, d // tk),
        in_specs=[
            pl.BlockSpec((tm, d), lambda i, k: (i, 0)),
            pl.BlockSpec((1, tk), lambda i, k: (0, k)),
            pl.BlockSpec((tm, tk), lambda i, k: (i, k)),
            pl.BlockSpec((1, tk), lambda i, k: (0, k)),
            pl.BlockSpec((tk, n), lambda i, k: (k, 0)),
            pl.BlockSpec((tk, n), lambda i, k: (k, 1)),
        ],
        out_specs=pl.BlockSpec((tm, n), lambda i, k: (i, 0)),
        scratch_shapes=[pltpu.VMEM((tm, 128), F32), pltpu.VMEM((tm, n), F32), pltpu.VMEM((tm, n), F32)],
        compiler_params=_params("parallel", "arbitrary"),
        name="s5_out",
    )(x, g.reshape(1, d), y, d_skip.reshape(1, d), w_glu, w_glu)


def s5_mixer(x, gmix, a_re, a_im, log_dt, b_re, b_im, c_re, c_im, d_skip, w_glu, bsz, seq):
    m, d = x.shape
    groups = d // S5_GROUP_DIM
    nchunk = seq // S5_L
    h = rmsnorm(x, gmix, out_dtype=BF16)
    u = h.reshape(m // S5_L, S5_L, groups, S5_GROUP_DIM).transpose(2, 0, 1, 3).reshape(groups, m // S5_L, S5_W)
    prep_f = s5_prep(a_re[0], a_im[0], log_dt[0], b_re[0], b_im[0], c_re[0], c_im[0], rev=False)
    prep_b = s5_prep(a_re[1], a_im[1], log_dt[1], b_re[1], b_im[1], c_re[1], c_im[1], rev=True)
    y = s5_main(u, prep_f, prep_b, nchunk)
    y = y.reshape(groups, m // S5_L, S5_L, S5_GROUP_DIM).transpose(1, 2, 0, 3).reshape(m, d)
    return s5_out(x, gmix, y, d_skip, w_glu.astype(BF16))


def _trunk(x, p, w, bsz, seq):
    depth = w['norm_mix'].shape[0]
    bf = lambda t: t.astype(BF16)
    w_gu, w_down = bf(w['ffn_w_gu']), bf(w['ffn_w_down'])
    for i in range(depth):
        kind, j = i % N_MIXERS, i // N_MIXERS
        gmix = w['norm_mix'][i]
        if kind == 0:
            qkv = norm_matmul(x, gmix, bf(w['na_w_qkv'][j]), out_dtype=BF16)
            att = na_attention(qkv, w['na_rpb'][j], bsz, seq)
            x = matmul_res(att, bf(w['na_w_o'][j]), x)
        elif kind == 1:
            uv = norm_matmul(x, gmix, bf(w['sg_w_in'][j]), act="gelu", out_dtype=BF16)
            x = sg_out(uv, w['sg_norm'][j], w['sg_w_s'][j], w['sg_b_s'][j], bf(w['sg_w_o'][j]), x)
        elif kind == 2:
            x = gated_deltanet(x, gmix, w['gdn_w_in'][j], w['gdn_conv_w'][j], w['gdn_a_log'][j],
                               w['gdn_dt_bias'][j], w['gdn_out_norm'][j], w['gdn_w_o'][j], bsz, seq)
        else:
            x = s5_mixer(x, gmix, w['s5_a_re'][j], w['s5_a_im'][j], w['s5_log_dt'][j], w['s5_b_re'][j],
                         w['s5_b_im'][j], w['s5_c_re'][j], w['s5_c_im'][j], w['s5_d'][j], w['s5_w_glu'][j], bsz, seq)
        x = ffn(x, w['norm_ffn'][i], w_gu, w['ffn_conv_w'][i], w['ffn_conv_b'][i], w_down, i, seq)
        x = ple(x, p, i, w['norm_ple'][i], bf(w['ple_w_gate'][i]), bf(w['ple_w_proj'][i]))
    return x


def kernel(x_prompt, x_sample, p_prompt, p_sample, norm_mix, norm_ffn, norm_ple, final_norm, na_w_qkv, na_w_o, na_rpb, sg_w_in, sg_norm, sg_w_s, sg_b_s, sg_w_o, gdn_w_in, gdn_conv_w, gdn_a_log, gdn_dt_bias, gdn_out_norm, gdn_w_o, s5_a_re, s5_a_im, s5_log_dt, s5_b_re, s5_b_im, s5_c_re, s5_c_im, s5_d, s5_w_glu, ffn_w_gu, ffn_conv_w, ffn_conv_b, ffn_w_down, ple_w_proj, ple_w_gate):
    w = dict(norm_mix=norm_mix, norm_ffn=norm_ffn, norm_ple=norm_ple, final_norm=final_norm,
             na_w_qkv=na_w_qkv, na_w_o=na_w_o, na_rpb=na_rpb,
             sg_w_in=sg_w_in, sg_norm=sg_norm, sg_w_s=sg_w_s, sg_b_s=sg_b_s, sg_w_o=sg_w_o,
             gdn_w_in=gdn_w_in, gdn_conv_w=gdn_conv_w, gdn_a_log=gdn_a_log, gdn_dt_bias=gdn_dt_bias,
             gdn_out_norm=gdn_out_norm, gdn_w_o=gdn_w_o,
             s5_a_re=s5_a_re, s5_a_im=s5_a_im, s5_log_dt=s5_log_dt, s5_b_re=s5_b_re, s5_b_im=s5_b_im,
             s5_c_re=s5_c_re, s5_c_im=s5_c_im, s5_d=s5_d, s5_w_glu=s5_w_glu,
             ffn_w_gu=ffn_w_gu, ffn_conv_w=ffn_conv_w, ffn_conv_b=ffn_conv_b, ffn_w_down=ffn_w_down,
             ple_w_proj=ple_w_proj, ple_w_gate=ple_w_gate)
    b1, seq, d = x_prompt.shape
    b2 = x_sample.shape[0]
    bsz = b1 + b2
    depth = p_prompt.shape[0]
    x = (x_prompt.reshape(b1 * seq, d), x_sample.reshape(b2 * seq, d))
    p = (p_prompt.reshape(depth, b1 * seq, -1), p_sample.reshape(depth, b2 * seq, -1))
    x = _trunk(x, p, w, bsz, seq)
    y1 = rmsnorm(x, final_norm, row0=0, rows=b1 * seq).reshape(b1, seq, d)
    y2 = rmsnorm(x, final_norm, row0=b1 * seq, rows=b2 * seq).reshape(b2, seq, d)
    return (y1, y2)
```

```python
import functools
import math

import jax
import jax.numpy as jnp
import numpy as np
from jax import lax
from jax.experimental import pallas as pl
from jax.experimental.pallas import tpu as pltpu

F32 = jnp.float32
BF16 = jnp.bfloat16

EPS = 1e-6
NEG_INF = -1e30
GRID_W = 64
NA_HEADS = 16
NA_WIN_R = 8
NA_WIN_C = 16
SG_CHUNK = 128
SG_GROUPS = 16
GDN_QK_HEADS = 16
GDN_V_HEADS = 32
GDN_DK = 128
GDN_DV = 128
GDN_CHUNK = 64
S5_GROUP_DIM = 16
S5_STATE = 64
N_MIXERS = 4

VMEM_LIMIT_BYTES = 56 * 1024 * 1024
HALO = 16


def _params(*sem):
    return pltpu.CompilerParams(dimension_semantics=sem, vmem_limit_bytes=VMEM_LIMIT_BYTES)


def _rms(x, g):
    return x * lax.rsqrt(jnp.mean(x * x, axis=-1, keepdims=True) + EPS) * g


def _gelu_tanh(x):
    return 0.5 * x * (1.0 + jnp.tanh(math.sqrt(2.0 / math.pi) * (x + 0.044715 * (x * x * x))))


def _sigmoid(x):
    return 1.0 / (1.0 + jnp.exp(-x))


def _rmsnorm_kernel(x_ref, g_ref, o_ref):
    o_ref[...] = _rms(x_ref[...], g_ref[...]).astype(o_ref.dtype)


def rmsnorm(x, g, out_dtype=F32, tm=512, row0=0, rows=None):
    m, d = x.shape
    rows = m if rows is None else rows
    off = row0 // tm
    return pl.pallas_call(
        _rmsnorm_kernel,
        out_shape=jax.ShapeDtypeStruct((rows, d), out_dtype),
        grid=(rows // tm,),
        in_specs=[pl.BlockSpec((tm, d), lambda i: (i + off, 0)), pl.BlockSpec((1, d), lambda i: (0, 0))],
        out_specs=pl.BlockSpec((tm, d), lambda i: (i, 0)),
        compiler_params=_params("parallel"),
        name="rmsnorm",
    )(x, g.reshape(1, d))


def _as_pair(x, tm):
    if isinstance(x, tuple):
        a, b = x
        return a, b, a.shape[0] // tm, a.shape[0] + b.shape[0]
    return x, x, None, x.shape[0]


def _pair_specs(block, n1, col):
    if n1 is None:
        return [pl.BlockSpec(block, lambda i, j: (i, col(i, j))), pl.BlockSpec(block, lambda i, j: (0, 0))]
    return [pl.BlockSpec(block, lambda i, j: (jnp.minimum(i, n1 - 1), col(i, j))),
            pl.BlockSpec(block, lambda i, j: (jnp.maximum(i - n1, 0), col(i, j)))]


def _pick(a_ref, b_ref, n1):
    if n1 is None:
        return a_ref[...]
    return jnp.where(pl.program_id(0) < n1, a_ref[...], b_ref[...])


def _norm_matmul_kernel(xa_ref, xb_ref, g_ref, w_ref, o_ref, hn_ref, *, act, n1):
    @pl.when(pl.program_id(1) == 0)
    def _():
        hn_ref[...] = _rms(_pick(xa_ref, xb_ref, n1), g_ref[...]).astype(BF16)

    y = jnp.dot(hn_ref[...], w_ref[...], preferred_element_type=F32)
    if act == "gelu":
        y = _gelu_tanh(y)
    o_ref[...] = y.astype(o_ref.dtype)


def norm_matmul(x, g, w, act=None, out_dtype=F32, tm=512, tn=1024, col0=0, ncols=None):
    xa, xb, n1, m = _as_pair(x, tm)
    d = xa.shape[1]
    n = w.shape[1] - col0 if ncols is None else ncols
    tn = next(t for t in (tn, 512, 256, 128) if n % t == 0 and col0 % t == 0)
    cb = col0 // tn
    return pl.pallas_call(
        functools.partial(_norm_matmul_kernel, act=act, n1=n1),
        out_shape=jax.ShapeDtypeStruct((m, n), out_dtype),
        grid=(m // tm, n // tn),
        in_specs=_pair_specs((tm, d), n1, lambda i, j: 0) + [
            pl.BlockSpec((1, d), lambda i, j: (0, 0)),
            pl.BlockSpec((d, tn), lambda i, j: (0, cb + j)),
        ],
        out_specs=pl.BlockSpec((tm, tn), lambda i, j: (i, j)),
        scratch_shapes=[pltpu.VMEM((tm, d), BF16)],
        compiler_params=_params("parallel", "arbitrary"),
        name="norm_matmul",
    )(xa, xb, g.reshape(1, d), w)


def _matmul_res_kernel(a_ref, w_ref, ra_ref, rb_ref, o_ref, *, n1):
    o_ref[...] = _pick(ra_ref, rb_ref, n1) + jnp.dot(a_ref[...].astype(BF16), w_ref[...],
                                                     preferred_element_type=F32)


def matmul_res(a, w, res, tm=512, tn=1024):
    m, k = a.shape
    n = w.shape[1]
    ra, rb, n1, _ = _as_pair(res, tm)
    return pl.pallas_call(
        functools.partial(_matmul_res_kernel, n1=n1),
        out_shape=jax.ShapeDtypeStruct((m, n), F32),
        grid=(m // tm, n // tn),
        in_specs=[
            pl.BlockSpec((tm, k), lambda i, j: (i, 0)),
            pl.BlockSpec((k, tn), lambda i, j: (0, j)),
        ] + _pair_specs((tm, tn), n1, lambda i, j: j),
        out_specs=pl.BlockSpec((tm, tn), lambda i, j: (i, j)),
        compiler_params=_params("parallel", "arbitrary"),
        name="matmul_res",
    )(a, w, ra, rb)


def _ffn_kernel(x_ref, xp_ref, xn_ref, g_ref, wg_ref, wu_ref, cw_ref, cb_ref, wd_ref, o_ref, hn_ref, acc_ref,
                *, tm, seq):
    i = pl.program_id(0)
    j = pl.program_id(1)

    @pl.when(j == 0)
    def _():
        g = g_ref[...]
        prev_ok = jnp.where((i * tm) % seq != 0, 1.0, 0.0)
        next_ok = jnp.where(((i + 1) * tm) % seq != 0, 1.0, 0.0)
        hn_ref[0:HALO, :] = (_rms(xp_ref[...], g) * prev_ok).astype(BF16)
        hn_ref[HALO:HALO + tm, :] = _rms(x_ref[...], g).astype(BF16)
        hn_ref[HALO + tm:, :] = (_rms(xn_ref[...], g) * next_ok).astype(BF16)
        acc_ref[...] = jnp.zeros_like(acc_ref)

    rows = tm + 2 * HALO
    gate = jnp.dot(hn_ref[...], wg_ref[0], preferred_element_type=F32)
    up = jnp.dot(hn_ref[HALO:HALO + tm, :], wu_ref[0], preferred_element_type=F32)
    cw = cw_ref[...]
    g_prev = pltpu.roll(gate, 1, 0)[HALO:HALO + tm]
    g_next = pltpu.roll(gate, rows - 1, 0)[HALO:HALO + tm]
    gc = cw[0:1] * g_prev + cw[1:2] * gate[HALO:HALO + tm] + cw[2:3] * g_next + cb_ref[...]
    act = (gc * _sigmoid(gc) * up).astype(BF16)
    acc_ref[...] += jnp.dot(act, wd_ref[0], preferred_element_type=F32)

    @pl.when(j == pl.num_programs(1) - 1)
    def _():
        o_ref[...] = x_ref[...] + acc_ref[...]


def ffn(x, g, w_gu, conv_w, conv_b, w_down, layer, seq, tm=512, tf=512):
    m, d = x.shape
    f = w_down.shape[1]
    nf = f // tf
    hb = tm // HALO
    last = m // HALO - 1
    return pl.pallas_call(
        functools.partial(_ffn_kernel, tm=tm, seq=seq),
        out_shape=jax.ShapeDtypeStruct((m, d), F32),
        grid=(m // tm, nf),
        in_specs=[
            pl.BlockSpec((tm, d), lambda i, j: (i, 0)),
            pl.BlockSpec((HALO, d), lambda i, j: (jnp.maximum(i * hb - 1, 0), 0)),
            pl.BlockSpec((HALO, d), lambda i, j: (jnp.minimum((i + 1) * hb, last), 0)),
            pl.BlockSpec((1, d), lambda i, j: (0, 0)),
            pl.BlockSpec((1, d, tf), lambda i, j: (layer, 0, j)),
            pl.BlockSpec((1, d, tf), lambda i, j: (layer, 0, j + nf)),
            pl.BlockSpec((3, tf), lambda i, j: (0, j)),
            pl.BlockSpec((1, tf), lambda i, j: (0, j)),
            pl.BlockSpec((1, tf, d), lambda i, j: (layer, j, 0)),
        ],
        out_specs=pl.BlockSpec((tm, d), lambda i, j: (i, 0)),
        scratch_shapes=[pltpu.VMEM((tm + 2 * HALO, d), BF16), pltpu.VMEM((tm, d), F32)],
        compiler_params=_params("parallel", "arbitrary"),
        name="ffn",
    )(x, x, x, g.reshape(1, d), w_gu, w_gu, conv_w, conv_b.reshape(1, f), w_down)


def _ple_kernel(x_ref, pa_ref, pb_ref, g_ref, wg_ref, wp_ref, o_ref, *, n1):
    x = x_ref[...]
    hn = _rms(x, g_ref[...]).astype(BF16)
    gate = _sigmoid(jnp.dot(hn, wg_ref[...], preferred_element_type=F32))
    p = jnp.where(pl.program_id(0) < n1, pa_ref[0], pb_ref[0])
    proj = jnp.dot(p.astype(BF16), wp_ref[...], preferred_element_type=F32)
    o_ref[...] = x + gate * proj


def ple(x, p, layer, g, w_gate, w_proj, tm=512):
    m, d = x.shape
    pa, pb = p if isinstance(p, tuple) else (p, p)
    n1 = pa.shape[1] // tm
    pd = pa.shape[2]
    return pl.pallas_call(
        functools.partial(_ple_kernel, n1=n1),
        out_shape=jax.ShapeDtypeStruct((m, d), F32),
        grid=(m // tm,),
        in_specs=[
            pl.BlockSpec((tm, d), lambda i: (i, 0)),
            pl.BlockSpec((1, tm, pd), lambda i: (layer, jnp.minimum(i, n1 - 1), 0)),
            pl.BlockSpec((1, tm, pd), lambda i: (layer, jnp.maximum(i - n1, 0), 0)),
            pl.BlockSpec((1, d), lambda i: (0, 0)),
            pl.BlockSpec((d, d), lambda i: (0, 0)),
            pl.BlockSpec((pd, d), lambda i: (0, 0)),
        ],
        out_specs=pl.BlockSpec((tm, d), lambda i: (i, 0)),
        compiler_params=_params("parallel"),
        name="ple",
    )(x, pa, pb, g.reshape(1, d), w_gate, w_proj)


GDN_BLOCK = 256


def _conv_in_kernel(x_ref, xp_ref, xn_ref, g_ref, w_ref, cw_ref, o_ref, hn_ref, *, tm, seq, nq, nqk, scale):
    i = pl.program_id(0)
    j = pl.program_id(1)

    @pl.when(j == 0)
    def _():
        g = g_ref[...]
        prev_ok = jnp.where((i * tm) % seq != 0, 1.0, 0.0)
        next_ok = jnp.where(((i + 1) * tm) % seq != 0, 1.0, 0.0)
        hn_ref[0:HALO, :] = (_rms(xp_ref[...], g) * prev_ok).astype(BF16)
        hn_ref[HALO:HALO + tm, :] = _rms(x_ref[...], g).astype(BF16)
        hn_ref[HALO + tm:, :] = (_rms(xn_ref[...], g) * next_ok).astype(BF16)

    cw = cw_ref[...]
    w = w_ref[...]
    tn = w.shape[1]
    half = tm // 2
    rows = half + 2 * HALO
    for lo in (0, half):
        y = jnp.dot(hn_ref[lo:lo + rows, :], w, preferred_element_type=F32)
        y_prev = pltpu.roll(y, 1, 0)[HALO:HALO + half]
        y_next = pltpu.roll(y, rows - 1, 0)[HALO:HALO + half]
        c = cw[0:1] * y_prev + cw[1:2] * y[HALO:HALO + half] + cw[2:3] * y_next
        c = c * _sigmoid(c)
        for s in range(tn // 128):
            cs = c[:, s * 128:(s + 1) * 128]
            inv = lax.rsqrt(jnp.sum(cs * cs, axis=-1, keepdims=True) + EPS)
            f = jnp.where(j < nq, inv * scale, jnp.where(j < nqk, inv, 1.0))
            o_ref[lo:lo + half, s * 128:(s + 1) * 128] = (cs * f).astype(o_ref.dtype)


def conv_in(x, g, w, n, conv_w, seq, n_q, n_qk, scale, tm=512, tn=1024):
    m, d = x.shape
    hb = tm // HALO
    last = m // HALO - 1
    return pl.pallas_call(
        functools.partial(_conv_in_kernel, tm=tm, seq=seq, nq=n_q // tn, nqk=n_qk // tn, scale=scale),
        out_shape=jax.ShapeDtypeStruct((m, n), BF16),
        grid=(m // tm, n // tn),
        in_specs=[
            pl.BlockSpec((tm, d), lambda i, j: (i, 0)),
            pl.BlockSpec((HALO, d), lambda i, j: (jnp.maximum(i * hb - 1, 0), 0)),
            pl.BlockSpec((HALO, d), lambda i, j: (jnp.minimum((i + 1) * hb, last), 0)),
            pl.BlockSpec((1, d), lambda i, j: (0, 0)),
            pl.BlockSpec((d, tn), lambda i, j: (0, j)),
            pl.BlockSpec((3, tn), lambda i, j: (0, j)),
        ],
        out_specs=pl.BlockSpec((tm, tn), lambda i, j: (i, j)),
        scratch_shapes=[pltpu.VMEM((tm + 2 * HALO, d), BF16)],
        compiler_params=_params("parallel", "arbitrary"),
        name="gdn_conv_in",
    )(x, x, x, g.reshape(1, d), w, conv_w)


def _gdn_gates_kernel(x_ref, g_ref, w_ref, alog_ref, bias_ref, isg_ref, o_ref):
    hn = _rms(x_ref[...], g_ref[...]).astype(BF16)
    y = jnp.dot(hn, w_ref[...], preferred_element_type=F32)
    t = y + bias_ref[...]
    softplus = jnp.maximum(t, 0.0) + jnp.log1p(jnp.exp(-jnp.abs(t)))
    isg = isg_ref[...] > 0.5
    base = jnp.where(isg, -jnp.exp(alog_ref[...]) * softplus, 0.0)
    tm, n = base.shape
    lc = GDN_CHUNK
    pos = lax.broadcasted_iota(jnp.int32, (tm, n), 0) & (lc - 1)
    pre, suf = base, base
    k = 1
    while k < lc:
        pre = pre + jnp.where(pos >= k, pltpu.roll(pre, k, 0), 0.0)
        suf = suf + jnp.where(pos + k <= lc - 1, pltpu.roll(suf, tm - k, 0), 0.0)
        k *= 2
    is_bwd = lax.broadcasted_iota(jnp.int32, (1, n), 1) >= n // 2
    o_ref[:, 0:n] = jnp.where(isg, jnp.where(is_bwd, suf, pre), _sigmoid(y))
    o_ref[:, n:2 * n] = pre + suf - base


def gdn_gates(x, g, w, col0, a_log, dt_bias, tm=512):
    m, d = x.shape
    n = w.shape[1] - col0
    nh = a_log.shape[-1]
    zeros = jnp.zeros((2, 1, nh), F32)
    arrange = lambda t: jnp.concatenate([t.reshape(2, 1, nh).astype(F32), zeros], axis=1).reshape(1, n)
    isg = jnp.concatenate([jnp.ones((2, 1, nh), F32), zeros], axis=1).reshape(1, n)
    return pl.pallas_call(
        _gdn_gates_kernel,
        out_shape=jax.ShapeDtypeStruct((m, 2 * n), F32),
        grid=(m // tm,),
        in_specs=[
            pl.BlockSpec((tm, d), lambda i: (i, 0)),
            pl.BlockSpec((1, d), lambda i: (0, 0)),
            pl.BlockSpec((d, n), lambda i: (0, col0 // n)),
            pl.BlockSpec((1, n), lambda i: (0, 0)),
            pl.BlockSpec((1, n), lambda i: (0, 0)),
            pl.BlockSpec((1, n), lambda i: (0, 0)),
        ],
        out_specs=pl.BlockSpec((tm, 2 * n), lambda i: (i, 0)),
        compiler_params=_params("parallel"),
        name="gdn_gates",
    )(x, g.reshape(1, d), w, arrange(a_log), arrange(dt_bias), isg)


def _gdn_chains(q_ref, k_ref, v_ref, gc_ref, gr_ref, o_ref, rev):
    c_sz, lc = GDN_BLOCK, GDN_CHUNK
    d = 1 if rev else 0
    ii = lax.broadcasted_iota(jnp.int32, (c_sz, c_sz), 0)
    jj = lax.broadcasted_iota(jnp.int32, (c_sz, c_sz), 1)
    sh = lc.bit_length() - 1
    same = (ii >> sh) == (jj >> sh)
    incl = jnp.logical_and(same, (jj >= ii) if rev else (jj <= ii))
    eye = ii == jj
    q = q_ref[...]
    k = k_ref[...]
    kf = k.astype(F32)
    qf = q.astype(F32)
    gc = gc_ref[0, 0][:, 8 * d:8 * d + 8]
    gr = gr_ref[0, 0][8 * d:8 * d + 8, :]
    gam_c = gc[:, 0:2]
    gam_r = gr[0:2]
    end_r = gr[4:6]
    gram = lax.dot_general(k, k, (((1,), (1,)), ((), ())), preferred_element_type=F32)
    qk = lax.dot_general(q, k, (((1,), (1,)), ((), ())), preferred_element_type=F32)
    chains = []
    for hs in range(2):
        gcol = gam_c[:, hs:hs + 1]
        grow = gam_r[hs:hs + 1, :]
        bcol = gc[:, 2 + hs:3 + hs]
        dec = jnp.exp(jnp.where(incl, gcol - grow, NEG_INF))
        e_g = jnp.exp(gcol)
        vh = v_ref[:, hs * GDN_DV:(hs + 1) * GDN_DV].astype(F32)
        chains.append(dict(
            n=(-(gram * jnp.where(eye, 0.0, dec)) * bcol).astype(BF16),
            x=jnp.concatenate([kf * (bcol * e_g), vh * bcol], axis=1),
            attn=(qk * dec).astype(BF16),
            qe=qf * e_g,
            kd=kf * jnp.exp(gc[:, 4 + hs:5 + hs] - gcol),
            end=end_r[hs:hs + 1, :],
            o_ref=o_ref, hs=hs, idx=2 * d + hs, rev=rev))
    return chains


def _gdn_scan_kernel(qf_ref, kf_ref, vf_ref, gcf_ref, grf_ref,
                     qb_ref, kb_ref, vb_ref, gcb_ref, grb_ref, of_ref, ob_ref, s_ref):
    @pl.when(pl.program_id(2) == 0)
    def _():
        s_ref[...] = jnp.zeros_like(s_ref)

    c_sz, lc = GDN_BLOCK, GDN_CHUNK
    nchunk = c_sz // lc
    width = GDN_DK + GDN_DV
    chains = (_gdn_chains(qf_ref, kf_ref, vf_ref, gcf_ref, grf_ref, of_ref, False)
              + _gdn_chains(qb_ref, kb_ref, vb_ref, gcb_ref, grb_ref, ob_ref, True))
    nlev = lc.bit_length() - 1
    for lvl in range(nlev):
        for ch in chains:
            nb = ch['n']
            xb = ch['x'].astype(BF16)
            if lvl < nlev - 1:
                r = jnp.dot(nb, jnp.concatenate([xb, nb], axis=1), preferred_element_type=F32)
                ch['x'] = ch['x'] + r[:, :width]
                ch['n'] = r[:, width:].astype(BF16)
            else:
                ch['x'] = ch['x'] + jnp.dot(nb, xb, preferred_element_type=F32)
    irow = lax.broadcasted_iota(jnp.int32, (c_sz, 1), 0) >> nlev
    for ch in chains:
        wub = ch['x'].astype(BF16)
        awu = jnp.dot(ch['attn'], wub, preferred_element_type=F32)
        ch['qeff'] = (ch['qe'] - awu[:, :GDN_DK]).astype(BF16)
        ch['o_in'] = awu[:, GDN_DK:]
        ch['kwu'] = [lax.dot_general(jnp.where(irow == c, ch['kd'], 0.0).astype(BF16), wub,
                                     (((0,), (0,)), ((), ())), preferred_element_type=F32)
                     for c in range(nchunk)]
        ch['s'] = s_ref[ch['idx']]
    for step in range(nchunk):
        for ch in chains:
            c = nchunk - 1 - step if ch['rev'] else step
            r0 = c * lc
            s = ch['s']
            sb = s.astype(BF16)
            o_c = ch['o_in'][r0:r0 + lc] + jnp.dot(ch['qeff'][r0:r0 + lc], sb, preferred_element_type=F32)
            ch['o_ref'][r0:r0 + lc, ch['hs'] * GDN_DV:(ch['hs'] + 1) * GDN_DV] = o_c.astype(ch['o_ref'].dtype)
            kwu = ch['kwu'][c]
            e_end = jnp.exp(ch['end'][:, r0:r0 + 1])
            ch['s'] = (e_end * s - jnp.dot(kwu[:, :GDN_DK].astype(BF16), sb, preferred_element_type=F32)
                       + kwu[:, GDN_DK:])
    for ch in chains:
        s_ref[ch['idx']] = ch['s']


def gdn_scan(qkv, gcol, grow, bsz, seq):
    c_sz = GDN_BLOCK
    nb = seq // c_sz
    hq = GDN_QK_HEADS
    kcol = hq
    vcol = (2 * hq * GDN_DK) // (2 * GDN_DV)
    fwd = lambda b, h, c: c
    bwd = lambda b, h, c: nb - 1 - c

    def specs(pos):
        return [
            pl.BlockSpec((c_sz, GDN_DK), lambda b, h, c: (b * nb + pos(b, h, c), h)),
            pl.BlockSpec((c_sz, GDN_DK), lambda b, h, c: (b * nb + pos(b, h, c), kcol + h)),
            pl.BlockSpec((c_sz, 2 * GDN_DV), lambda b, h, c: (b * nb + pos(b, h, c), vcol + h)),
            pl.BlockSpec((1, 1, c_sz, 16), lambda b, h, c: (b, h, pos(b, h, c), 0)),
            pl.BlockSpec((1, 1, 16, c_sz), lambda b, h, c: (b, h, 0, pos(b, h, c))),
        ]

    out = jax.ShapeDtypeStruct((bsz * seq, GDN_V_HEADS * GDN_DV), BF16)
    return pl.pallas_call(
        _gdn_scan_kernel,
        out_shape=(out, out),
        grid=(bsz, hq, nb),
        in_specs=specs(fwd) + specs(bwd),
        out_specs=(
            pl.BlockSpec((c_sz, 2 * GDN_DV), lambda b, h, c: (b * nb + c, h)),
            pl.BlockSpec((c_sz, 2 * GDN_DV), lambda b, h, c: (b * nb + nb - 1 - c, h)),
        ),
        scratch_shapes=[pltpu.VMEM((4, GDN_DK, GDN_DV), F32)],
        compiler_params=_params("parallel", "parallel", "arbitrary"),
        name="gdn_scan",
    )(qkv, qkv, qkv, gcol, grow, qkv, qkv, qkv, gcol, grow)


def _gdn_out_kernel(of_ref, ob_ref, z_ref, gn_ref, w_ref, r_ref, o_ref):
    @pl.when(pl.program_id(1) == 0)
    def _():
        o_ref[...] = r_ref[...]

    gn = gn_ref[...]
    parts = []
    for h in range(of_ref.shape[1] // GDN_DV):
        sl = slice(h * GDN_DV, (h + 1) * GDN_DV)
        o = of_ref[:, sl].astype(F32) + ob_ref[:, sl].astype(F32)
        z = z_ref[:, sl].astype(F32)
        parts.append((_rms(o, gn) * (z * _sigmoid(z))).astype(BF16))
    o_ref[...] += jnp.dot(jnp.concatenate(parts, axis=1), w_ref[...], preferred_element_type=F32)


def gdn_out(o_f, o_b, z, out_norm, w_o, res, tm=512, tk=1024):
    m, k = o_f.shape
    n = w_o.shape[1]
    return pl.pallas_call(
        _gdn_out_kernel,
        out_shape=jax.ShapeDtypeStruct((m, n), F32),
        grid=(m // tm, k // tk),
        in_specs=[
            pl.BlockSpec((tm, tk), lambda i, j: (i, j)),
            pl.BlockSpec((tm, tk), lambda i, j: (i, j)),
            pl.BlockSpec((tm, tk), lambda i, j: (i, j)),
            pl.BlockSpec((1, GDN_DV), lambda i, j: (0, 0)),
            pl.BlockSpec((tk, n), lambda i, j: (j, 0)),
            pl.BlockSpec((tm, n), lambda i, j: (i, 0)),
        ],
        out_specs=pl.BlockSpec((tm, n), lambda i, j: (i, 0)),
        compiler_params=_params("parallel", "arbitrary"),
        name="gdn_out",
    )(o_f, o_b, z, out_norm.reshape(1, GDN_DV), w_o, res)


def gated_deltanet(x, gmix, w_in, conv_w, a_log, dt_bias, out_norm, w_o, bsz, seq):
    qk_w = GDN_QK_HEADS * GDN_DK
    v_w = GDN_V_HEADS * GDN_DV
    cw = 2 * qk_w + v_w
    w_in = w_in.astype(BF16)
    qkv = conv_in(x, gmix, w_in, cw, conv_w, seq, qk_w, 2 * qk_w, GDN_DK ** -0.5)
    z = norm_matmul(x, gmix, w_in, out_dtype=BF16, col0=cw, ncols=v_w)
    gb = gdn_gates(x, gmix, w_in, cw + v_w, a_log, dt_bias)
    nab = gb.shape[1] // 2
    ab = gb[:, :nab].reshape(bsz, seq, 2, 2, GDN_QK_HEADS, 2).transpose(0, 4, 1, 2, 3, 5)
    tot = gb[:, nab:].reshape(bsz, seq, 2, 2, GDN_QK_HEADS, 2)[:, :, :, 0].transpose(0, 3, 1, 2, 4)[..., None, :]
    gcol = jnp.concatenate([ab, tot, jnp.zeros_like(tot)], axis=4).reshape(bsz, GDN_QK_HEADS, seq, 16)
    grow = jnp.swapaxes(gcol, 2, 3)
    o_f, o_b = gdn_scan(qkv, gcol, grow, bsz, seq)
    return gdn_out(o_f, o_b, z, out_norm, w_o.astype(BF16), x)


def _na_bias_table(rpb, rows):
    win_r = min(NA_WIN_R, rows)
    cols = np.arange(GRID_W)
    col_start = np.clip(cols - NA_WIN_C // 2, 0, GRID_W - NA_WIN_C)
    col_valid = (cols[None, :] >= col_start[:, None]) & (cols[None, :] < col_start[:, None] + NA_WIN_C)
    dc_idx = np.clip(cols[None, :] - cols[:, None] + NA_WIN_C - 1, 0, 2 * NA_WIN_C - 2)
    bias_c = jnp.where(col_valid, rpb[:, :, dc_idx].astype(F32), NEG_INF)
    dr = np.arange(NA_WIN_R)[:, None] + np.arange(win_r)[None, :]
    tab = bias_c[:, dr]
    return jnp.transpose(tab, (0, 1, 3, 2, 4)).reshape(rpb.shape[0], NA_WIN_R, GRID_W, win_r * GRID_W)


def _na_kernel(q_ref, k_ref, v_ref, b_ref, o_ref, *, rows, win_r, scale):
    wk = win_r * GRID_W

    group = 16 if rows % 16 == 0 else 1

    def body(it, carry):
        rs = [it * group + i for i in range(group)]
        r0s = [jnp.clip(r - win_r // 2, 0, rows - win_r) for r in rs]
        scores = []
        for r, r0 in zip(rs, r0s):
            q = q_ref[pl.ds(pl.multiple_of(r * GRID_W, GRID_W), GRID_W), :]
            kw = k_ref[pl.ds(pl.multiple_of(r0 * GRID_W, GRID_W), wk), :]
            s = lax.dot_general(q, kw, (((1,), (1,)), ((), ())), preferred_element_type=F32) * scale
            scores.append(s + b_ref[0, r0 - r + NA_WIN_R - 1])
        probs, dens = [], []
        for s in scores:
            p = jnp.exp(s - jnp.max(s, axis=-1, keepdims=True))
            dens.append(jnp.sum(p, axis=-1, keepdims=True))
            probs.append(p.astype(BF16))
        for r, r0, p, den in zip(rs, r0s, probs, dens):
            vw = v_ref[pl.ds(pl.multiple_of(r0 * GRID_W, GRID_W), wk), :]
            o = jnp.dot(p, vw, preferred_element_type=F32) / den
            o_ref[pl.ds(pl.multiple_of(r * GRID_W, GRID_W), GRID_W), :] = o.astype(o_ref.dtype)
        return carry

    lax.fori_loop(0, rows // group, body, 0)


def na_attention(qkv, rpb, bsz, seq):
    d = qkv.shape[1] // 3
    dh = d // NA_HEADS
    rows = seq // GRID_W
    win_r = min(NA_WIN_R, rows)
    table = _na_bias_table(rpb, rows)
    blk = lambda off: pl.BlockSpec((seq, dh), lambda b, h: (b, off + h))
    return pl.pallas_call(
        functools.partial(_na_kernel, rows=rows, win_r=win_r, scale=dh ** -0.5),
        out_shape=jax.ShapeDtypeStruct((bsz * seq, d), BF16),
        grid=(bsz, NA_HEADS),
        in_specs=[blk(0), blk(NA_HEADS), blk(2 * NA_HEADS),
                  pl.BlockSpec((1, NA_WIN_R, GRID_W, win_r * GRID_W), lambda b, h: (h, 0, 0, 0))],
        out_specs=blk(0),
        compiler_params=_params("parallel", "parallel"),
        name="na_attention",
    )(qkv, qkv, qkv, table)


def _sg_out_kernel(u_ref, v_ref, gn_ref, ws_ref, bs_ref, w_ref, r_ref, o_ref, a_ref, *, tm):
    @pl.when(pl.program_id(1) == 0)
    def _():
        vn = _rms(v_ref[...].astype(F32), gn_ref[...]).astype(BF16)
        bs = bs_ref[...]
        gd = vn.shape[1] // SG_GROUPS
        for c in range(tm // SG_CHUNK):
            rs = slice(c * SG_CHUNK, (c + 1) * SG_CHUNK)
            for g in range(SG_GROUPS):
                cs = slice(g * gd, (g + 1) * gd)
                mixed = jnp.dot(ws_ref[g], vn[rs, cs], preferred_element_type=F32) + bs[:, g:g + 1]
                a_ref[rs, cs] = (u_ref[rs, cs].astype(F32) * mixed).astype(BF16)

    o_ref[...] = r_ref[...] + jnp.dot(a_ref[...], w_ref[...], preferred_element_type=F32)


def sg_out(uv, sg_norm, w_s, b_s, w_o, res, tm=512, tn=1024):
    m = uv.shape[0]
    width = uv.shape[1] // 2
    n = w_o.shape[1]
    return pl.pallas_call(
        functools.partial(_sg_out_kernel, tm=tm),
        out_shape=jax.ShapeDtypeStruct((m, n), F32),
        grid=(m // tm, n // tn),
        in_specs=[
            pl.BlockSpec((tm, width), lambda i, j: (i, 0)),
            pl.BlockSpec((tm, width), lambda i, j: (i, 1)),
            pl.BlockSpec((1, width), lambda i, j: (0, 0)),
            pl.BlockSpec(w_s.shape, lambda i, j: (0, 0, 0)),
            pl.BlockSpec((SG_CHUNK, SG_GROUPS), lambda i, j: (0, 0)),
            pl.BlockSpec((width, tn), lambda i, j: (0, j)),
            pl.BlockSpec((tm, tn), lambda i, j: (i, j)),
        ],
        out_specs=pl.BlockSpec((tm, tn), lambda i, j: (i, j)),
        scratch_shapes=[pltpu.VMEM((tm, width), BF16)],
        compiler_params=_params("parallel", "arbitrary"),
        name="sg_out",
    )(uv, uv, sg_norm.reshape(1, width), w_s.astype(BF16), b_s.T.astype(F32), w_o, res)


S5_L = 32
S5_W = S5_L * S5_GROUP_DIM


def _cexp(are, aim, dt, e):
    mag = jnp.exp(are * dt * e)
    ang = aim * dt * e
    return mag * jnp.cos(ang), mag * jnp.sin(ang)


def _s5_prep_kernel(arc_ref, aic_ref, arr_ref, air_ref, ldt_ref, btr_ref, bti_ref, ctr_ref, cti_ref,
                    k_ref, bm_ref, cm_ref, ap_ref, *, rev):
    hp = lax.Precision.HIGHEST
    ll, cg, w, p = S5_L, S5_GROUP_DIM, S5_W, S5_STATE
    sh = cg.bit_length() - 1
    dt = jnp.exp(ldt_ref[0])
    arc, aic = arc_ref[0], aic_ref[0]
    arr, air = arr_ref[0], air_ref[0]
    abr, abi = _cexp(arr, air, dt, 1.0)
    nr, ni = abr - 1.0, abi
    den = arr * arr + air * air
    cr, ci = (nr * arr + ni * air) / den, (ni * arr - nr * air) / den
    btr, bti = btr_ref[0], bti_ref[0]
    bbr, bbi = cr * btr - ci * bti, cr * bti + ci * btr
    lane = lax.broadcasted_iota(jnp.int32, (cg, w), 1)
    sel = jnp.where((lane & (cg - 1)) == lax.broadcasted_iota(jnp.int32, (cg, w), 0), 1.0, 0.0)
    cer = jnp.dot(ctr_ref[0], sel, precision=hp, preferred_element_type=F32)
    cei = jnp.dot(cti_ref[0], sel, precision=hp, preferred_element_type=F32)
    tl = (lax.broadcasted_iota(jnp.int32, (p, w), 1) >> sh).astype(F32)

    def cz(e):
        zr, zi = _cexp(arc, aic, dt, e)
        return cer * zr - cei * zi, cer * zi + cei * zr

    czr, czi = cz((ll - 1.0 - tl) if rev else tl)
    r = jnp.dot(bbr, czr, precision=hp, preferred_element_type=F32) - jnp.dot(bbi, czi, precision=hp,
                                                                                preferred_element_type=F32)
    lane_r = lax.broadcasted_iota(jnp.int32, (cg, w), 1)
    for s in range(ll):
        if rev:
            blk = jnp.where(lane_r < cg * (s + 1), pltpu.roll(r, (w - cg * (ll - 1 - s)) % w, 1), 0.0)
        else:
            blk = jnp.where(lane_r >= cg * s, pltpu.roll(r, cg * s, 1), 0.0)
        k_ref[0, s * cg:(s + 1) * cg, :] = blk.astype(k_ref.dtype)
    acr, aci = _cexp(arc, aic, dt, 1.0)
    c1r, c1i = czr * acr - czi * aci, czr * aci + czi * acr
    cm_ref[0, 0:p, :] = c1r.astype(cm_ref.dtype)
    cm_ref[0, p:2 * p, :] = (-c1i).astype(cm_ref.dtype)
    srow = (lax.broadcasted_iota(jnp.int32, (w, p), 0) >> sh).astype(F32)
    zr, zi = _cexp(arr, air, dt, srow if rev else (ll - 1.0 - srow))
    tbr, tbi = jnp.tile(bbr, (ll, 1)), jnp.tile(bbi, (ll, 1))
    bm_ref[0, :, 0:p] = (zr * tbr - zi * tbi).astype(bm_ref.dtype)
    bm_ref[0, :, p:2 * p] = (zr * tbi + zi * tbr).astype(bm_ref.dtype)
    ek = (ll << lax.broadcasted_iota(jnp.int32, (8, p), 0)).astype(F32)
    pr, pi = _cexp(arr, air, dt, ek)
    ap_ref[0, :, 0:p] = pr
    ap_ref[0, :, p:2 * p] = pi


def s5_prep(a_re, a_im, log_dt, b_re, b_im, c_re, c_im, rev):
    g, p = a_re.shape
    cg, w = S5_GROUP_DIM, S5_W
    col = lambda t: t.reshape(g, p, 1).astype(F32)
    row = lambda t: t.reshape(g, 1, p).astype(F32)
    tr = lambda t: jnp.swapaxes(t, 1, 2).astype(F32)
    spec = lambda s: pl.BlockSpec((1,) + s, lambda i: (i, 0, 0))
    return pl.pallas_call(
        functools.partial(_s5_prep_kernel, rev=rev),
        out_shape=(jax.ShapeDtypeStruct((g, w, w), BF16), jax.ShapeDtypeStruct((g, w, 2 * p), BF16),
                   jax.ShapeDtypeStruct((g, 2 * p, w), BF16), jax.ShapeDtypeStruct((g, 8, 2 * p), F32)),
        grid=(g,),
        in_specs=[spec((p, 1)), spec((p, 1)), spec((1, p)), spec((1, p)), spec((1, 1)),
                  spec((cg, p)), spec((cg, p)), spec((p, cg)), spec((p, cg))],
        out_specs=(spec((w, w)), spec((w, 2 * p)), spec((2 * p, w)), spec((8, 2 * p))),
        compiler_params=_params("parallel"),
        name="s5_prep",
    )(col(a_re), col(a_im), row(a_re), row(a_im), log_dt.reshape(g, 1, 1).astype(F32),
      tr(b_re), tr(b_im), tr(c_re), tr(c_im))


def _s5_chunk_scan(s, ap, nchunk, rev):
    n, w2 = s.shape
    p = w2 // 2
    m = lax.broadcasted_iota(jnp.int32, (n, w2), 0) & (nchunk - 1)
    lane = lax.broadcasted_iota(jnp.int32, (1, w2), 1)

    def shift(x, k):
        if rev:
            return jnp.where(m + k <= nchunk - 1, pltpu.roll(x, n - k, 0), 0.0)
        return jnp.where(m >= k, pltpu.roll(x, k, 0), 0.0)

    x = s
    k, lvl = 1, 0
    while k < nchunk:
        a = ap[lvl:lvl + 1, :]
        a1 = jnp.where(lane < p, a, pltpu.roll(a, p, 1))
        a2 = jnp.where(lane < p, -pltpu.roll(a, p, 1), a)
        xs = shift(x, k)
        x = x + a1 * xs + a2 * pltpu.roll(xs, p, 1)
        k, lvl = 2 * k, lvl + 1
    return shift(x, 1)


def _s5_main_kernel(u_ref, kf_ref, kb_ref, bf_ref, bb_ref, cf_ref, cb_ref, af_ref, ab_ref, y_ref, *, nchunk):
    u = u_ref[0]
    y = jnp.dot(u, kf_ref[0], preferred_element_type=F32) + jnp.dot(u, kb_ref[0], preferred_element_type=F32)
    for bm_ref, cm_ref, ap_ref, rev in ((bf_ref, cf_ref, af_ref, False), (bb_ref, cb_ref, ab_ref, True)):
        s = jnp.dot(u, bm_ref[0], preferred_element_type=F32)
        xin = _s5_chunk_scan(s, ap_ref[0], nchunk, rev)
        y = y + jnp.dot(xin.astype(BF16), cm_ref[0], preferred_element_type=F32)
    y_ref[0] = y.astype(y_ref.dtype)


def s5_main(u, prep_f, prep_b, nchunk):
    g, n, w = u.shape
    p2 = 2 * S5_STATE
    spec = lambda s: pl.BlockSpec((1,) + s, lambda i: (i, 0, 0))
    kf, bf, cf, af = prep_f
    kb, bb, cb, ab = prep_b
    return pl.pallas_call(
        functools.partial(_s5_main_kernel, nchunk=nchunk),
        out_shape=jax.ShapeDtypeStruct((g, n, w), BF16),
        grid=(g,),
        in_specs=[spec((n, w)), spec((w, w)), spec((w, w)), spec((w, p2)), spec((w, p2)),
                  spec((p2, w)), spec((p2, w)), spec((8, p2)), spec((8, p2))],
        out_specs=spec((n, w)),
        compiler_params=_params("parallel"),
        name="s5_main",
    )(u, kf, kb, bf, bb, cf, cb, af, ab)


def _s5_out_kernel(x_ref, g_ref, y_ref, d_ref, wa_ref, wb_ref, o_ref, inv_ref, acca_ref, accb_ref):
    k = pl.program_id(1)
    tk = y_ref.shape[1]

    @pl.when(k == 0)
    def _():
        x = x_ref[...]
        inv_ref[...] = jnp.broadcast_to(lax.rsqrt(jnp.mean(x * x, axis=-1, keepdims=True) + EPS), inv_ref.shape)
        acca_ref[...] = jnp.zeros_like(acca_ref)
        accb_ref[...] = jnp.zeros_like(accb_ref)

    xk = x_ref[:, pl.ds(pl.multiple_of(k * tk, tk), tk)]
    h = xk * inv_ref[:, 0:1] * g_ref[...]
    a = _gelu_tanh(y_ref[...].astype(F32) + d_ref[...] * h).astype(BF16)
    acca_ref[...] += jnp.dot(a, wa_ref[...], preferred_element_type=F32)
    accb_ref[...] += jnp.dot(a, wb_ref[...], preferred_element_type=F32)

    @pl.when(k == pl.num_programs(1) - 1)
    def _():
        o_ref[...] = x_ref[...] + acca_ref[...] * _sigmoid(accb_ref[...])


def s5_out(x, g, y, d_skip, w_glu, tm=512, tk=512):
    m, d = x.shape
    n = w_glu.shape[1] // 2
    return pl.pallas_call(
        _s5_out_kernel,
        out_shape=jax.ShapeDtypeStruct((m, n), F32),
        grid=(m // tm, d // tk),
        in_specs=[
            pl.BlockSpec((tm, d), lambda i, k: (i, 0)),
            pl.BlockSpec((1, tk), lambda i, k: (0, k)),
            pl.BlockSpec((tm, tk), lambda i, k: (i, k)),
            pl.BlockSpec((1, tk), lambda i, k: (0, k)),
            pl.BlockSpec((tk, n), lambda i, k: (k, 0)),
            pl.BlockSpec((tk, n), lambda i, k: (k, 1)),
        ],
        out_specs=pl.BlockSpec((tm, n), lambda i, k: (i, 0)),
        scratch_shapes=[pltpu.VMEM((tm, 128), F32), pltpu.VMEM((tm, n), F32), pltpu.VMEM((tm, n), F32)],
        compiler_params=_params("parallel", "arbitrary"),
        name="s5_out",
    )(x, g.reshape(1, d), y, d_skip.reshape(1, d), w_glu, w_glu)


def s5_mixer(x, gmix, a_re, a_im, log_dt, b_re, b_im, c_re, c_im, d_skip, w_glu, bsz, seq):
    m, d = x.shape
    groups = d // S5_GROUP_DIM
    nchunk = seq // S5_L
    h = rmsnorm(x, gmix, out_dtype=BF16)
    u = h.reshape(m // S5_L, S5_L, groups, S5_GROUP_DIM).transpose(2, 0, 1, 3).reshape(groups, m // S5_L, S5_W)
    prep_f = s5_prep(a_re[0], a_im[0], log_dt[0], b_re[0], b_im[0], c_re[0], c_im[0], rev=False)
    prep_b = s5_prep(a_re[1], a_im[1], log_dt[1], b_re[1], b_im[1], c_re[1], c_im[1], rev=True)
    y = s5_main(u, prep_f, prep_b, nchunk)
    y = y.reshape(groups, m // S5_L, S5_L, S5_GROUP_DIM).transpose(1, 2, 0, 3).reshape(m, d)
    return s5_out(x, gmix, y, d_skip, w_glu.astype(BF16))


def _trunk(x, p, w, bsz, seq):
    depth = w['norm_mix'].shape[0]
    bf = lambda t: t.astype(BF16)
    w_gu, w_down = bf(w['ffn_w_gu']), bf(w['ffn_w_down'])
    for i in range(depth):
        kind, j = i % N_MIXERS, i // N_MIXERS
        gmix = w['norm_mix'][i]
        if kind == 0:
            qkv = norm_matmul(x, gmix, bf(w['na_w_qkv'][j]), out_dtype=BF16)
            att = na_attention(qkv, w['na_rpb'][j], bsz, seq)
            x = matmul_res(att, bf(w['na_w_o'][j]), x)
        elif kind == 1:
            uv = norm_matmul(x, gmix, bf(w['sg_w_in'][j]), act="gelu", out_dtype=BF16)
            x = sg_out(uv, w['sg_norm'][j], w['sg_w_s'][j], w['sg_b_s'][j], bf(w['sg_w_o'][j]), x)
        elif kind == 2:
            x = gated_deltanet(x, gmix, w['gdn_w_in'][j], w['gdn_conv_w'][j], w['gdn_a_log'][j],
                               w['gdn_dt_bias'][j], w['gdn_out_norm'][j], w['gdn_w_o'][j], bsz, seq)
        else:
            x = s5_mixer(x, gmix, w['s5_a_re'][j], w['s5_a_im'][j], w['s5_log_dt'][j], w['s5_b_re'][j],
                         w['s5_b_im'][j], w['s5_c_re'][j], w['s5_c_im'][j], w['s5_d'][j], w['s5_w_glu'][j], bsz, seq)
        x = ffn(x, w['norm_ffn'][i], w_gu, w['ffn_conv_w'][i], w['ffn_conv_b'][i], w_down, i, seq)
        x = ple(x, p, i, w['norm_ple'][i], bf(w['ple_w_gate'][i]), bf(w['ple_w_proj'][i]))
    return x


def kernel(x_prompt, x_sample, p_prompt, p_sample, norm_mix, norm_ffn, norm_ple, final_norm, na_w_qkv, na_w_o, na_rpb, sg_w_in, sg_norm, sg_w_s, sg_b_s, sg_w_o, gdn_w_in, gdn_conv_w, gdn_a_log, gdn_dt_bias, gdn_out_norm, gdn_w_o, s5_a_re, s5_a_im, s5_log_dt, s5_b_re, s5_b_im, s5_c_re, s5_c_im, s5_d, s5_w_glu, ffn_w_gu, ffn_conv_w, ffn_conv_b, ffn_w_down, ple_w_proj, ple_w_gate):
    w = dict(norm_mix=norm_mix, norm_ffn=norm_ffn, norm_ple=norm_ple, final_norm=final_norm,
             na_w_qkv=na_w_qkv, na_w_o=na_w_o, na_rpb=na_rpb,
             sg_w_in=sg_w_in, sg_norm=sg_norm, sg_w_s=sg_w_s, sg_b_s=sg_b_s, sg_w_o=sg_w_o,
             gdn_w_in=gdn_w_in, gdn_conv_w=gdn_conv_w, gdn_a_log=gdn_a_log, gdn_dt_bias=gdn_dt_bias,
             gdn_out_norm=gdn_out_norm, gdn_w_o=gdn_w_o,
             s5_a_re=s5_a_re, s5_a_im=s5_a_im, s5_log_dt=s5_log_dt, s5_b_re=s5_b_re, s5_b_im=s5_b_im,
             s5_c_re=s5_c_re, s5_c_im=s5_c_im, s5_d=s5_d, s5_w_glu=s5_w_glu,
             ffn_w_gu=ffn_w_gu, ffn_conv_w=ffn_conv_w, ffn_conv_b=ffn_conv_b, ffn_w_down=ffn_w_down,
             ple_w_proj=ple_w_proj, ple_w_gate=ple_w_gate)
    b1, seq, d = x_prompt.shape
    b2 = x_sample.shape[0]
    bsz = b1 + b2
    depth = p_prompt.shape[0]
    x = (x_prompt.reshape(b1 * seq, d), x_sample.reshape(b2 * seq, d))
    p = (p_prompt.reshape(depth, b1 * seq, -1), p_sample.reshape(depth, b2 * seq, -1))
    x = _trunk(x, p, w, bsz, seq)
    y1 = rmsnorm(x, final_norm, row0=0, rows=b1 * seq).reshape(b1, seq, d)
    y2 = rmsnorm(x, final_norm, row0=b1 * seq, rows=b2 * seq).reshape(b2, seq, d)
    return (y1, y2)
```

```python
import functools
import math

import jax
import jax.numpy as jnp
import numpy as np
from jax import lax
from jax.experimental import pallas as pl
from jax.experimental.pallas import tpu as pltpu

F32 = jnp.float32
BF16 = jnp.bfloat16

EPS = 1e-6
NEG_INF = -1e30
GRID_W = 64
NA_HEADS = 16
NA_WIN_R = 8
NA_WIN_C = 16
SG_CHUNK = 128
SG_GROUPS = 16
GDN_QK_HEADS = 16
GDN_V_HEADS = 32
GDN_DK = 128
GDN_DV = 128
GDN_CHUNK = 64
S5_GROUP_DIM = 16
S5_STATE = 64
N_MIXERS = 4

VMEM_LIMIT_BYTES = 56 * 1024 * 1024
HALO = 16


def _params(*sem):
    return pltpu.CompilerParams(dimension_semantics=sem, vmem_limit_bytes=VMEM_LIMIT_BYTES)


def _rms(x, g):
    return x * lax.rsqrt(jnp.mean(x * x, axis=-1, keepdims=True) + EPS) * g


def _gelu_tanh(x):
    return 0.5 * x * (1.0 + jnp.tanh(math.sqrt(2.0 / math.pi) * (x + 0.044715 * (x * x * x))))


def _sigmoid(x):
    return 1.0 / (1.0 + jnp.exp(-x))


def _rmsnorm_kernel(x_ref, g_ref, o_ref):
    o_ref[...] = _rms(x_ref[...], g_ref[...]).astype(o_ref.dtype)


def rmsnorm(x, g, out_dtype=F32, tm=512, row0=0, rows=None):
    m, d = x.shape
    rows = m if rows is None else rows
    off = row0 // tm
    return pl.pallas_call(
        _rmsnorm_kernel,
        out_shape=jax.ShapeDtypeStruct((rows, d), out_dtype),
        grid=(rows // tm,),
        in_specs=[pl.BlockSpec((tm, d), lambda i: (i + off, 0)), pl.BlockSpec((1, d), lambda i: (0, 0))],
        out_specs=pl.BlockSpec((tm, d), lambda i: (i, 0)),
        compiler_params=_params("parallel"),
        name="rmsnorm",
    )(x, g.reshape(1, d))


def _as_pair(x, tm):
    if isinstance(x, tuple):
        a, b = x
        return a, b, a.shape[0] // tm, a.shape[0] + b.shape[0]
    return x, x, None, x.shape[0]


def _pair_specs(block, n1, col):
    if n1 is None:
        return [pl.BlockSpec(block, lambda i, j: (i, col(i, j))), pl.BlockSpec(block, lambda i, j: (0, 0))]
    return [pl.BlockSpec(block, lambda i, j: (jnp.minimum(i, n1 - 1), col(i, j))),
            pl.BlockSpec(block, lambda i, j: (jnp.maximum(i - n1, 0), col(i, j)))]


def _pick(a_ref, b_ref, n1):
    if n1 is None:
        return a_ref[...]
    return jnp.where(pl.program_id(0) < n1, a_ref[...], b_ref[...])


def _norm_matmul_kernel(xa_ref, xb_ref, g_ref, w_ref, o_ref, hn_ref, *, act, n1):
    @pl.when(pl.program_id(1) == 0)
    def _():
        hn_ref[...] = _rms(_pick(xa_ref, xb_ref, n1), g_ref[...]).astype(BF16)

    y = jnp.dot(hn_ref[...], w_ref[...], preferred_element_type=F32)
    if act == "gelu":
        y = _gelu_tanh(y)
    o_ref[...] = y.astype(o_ref.dtype)


def norm_matmul(x, g, w, act=None, out_dtype=F32, tm=512, tn=1024, col0=0, ncols=None):
    xa, xb, n1, m = _as_pair(x, tm)
    d = xa.shape[1]
    n = w.shape[1] - col0 if ncols is None else ncols
    tn = next(t for t in (tn, 512, 256, 128) if n % t == 0 and col0 % t == 0)
    cb = col0 // tn
    return pl.pallas_call(
        functools.partial(_norm_matmul_kernel, act=act, n1=n1),
        out_shape=jax.ShapeDtypeStruct((m, n), out_dtype),
        grid=(m // tm, n // tn),
        in_specs=_pair_specs((tm, d), n1, lambda i, j: 0) + [
            pl.BlockSpec((1, d), lambda i, j: (0, 0)),
            pl.BlockSpec((d, tn), lambda i, j: (0, cb + j)),
        ],
        out_specs=pl.BlockSpec((tm, tn), lambda i, j: (i, j)),
        scratch_shapes=[pltpu.VMEM((tm, d), BF16)],
        compiler_params=_params("parallel", "arbitrary"),
        name="norm_matmul",
    )(xa, xb, g.reshape(1, d), w)


def _matmul_res_kernel(a_ref, w_ref, ra_ref, rb_ref, o_ref, *, n1):
    o_ref[...] = _pick(ra_ref, rb_ref, n1) + jnp.dot(a_ref[...].astype(BF16), w_ref[...],
                                                     preferred_element_type=F32)


def matmul_res(a, w, res, tm=512, tn=1024):
    m, k = a.shape
    n = w.shape[1]
    ra, rb, n1, _ = _as_pair(res, tm)
    return pl.pallas_call(
        functools.partial(_matmul_res_kernel, n1=n1),
        out_shape=jax.ShapeDtypeStruct((m, n), F32),
        grid=(m // tm, n // tn),
        in_specs=[
            pl.BlockSpec((tm, k), lambda i, j: (i, 0)),
            pl.BlockSpec((k, tn), lambda i, j: (0, j)),
        ] + _pair_specs((tm, tn), n1, lambda i, j: j),
        out_specs=pl.BlockSpec((tm, tn), lambda i, j: (i, j)),
        compiler_params=_params("parallel", "arbitrary"),
        name="matmul_res",
    )(a, w, ra, rb)


def _ffn_kernel(x_ref, xp_ref, xn_ref, g_ref, wg_ref, wu_ref, cw_ref, cb_ref, wd_ref, o_ref, hn_ref, acc_ref,
                *, tm, seq):
    i = pl.program_id(0)
    j = pl.program_id(1)

    @pl.when(j == 0)
    def _():
        g = g_ref[...]
        prev_ok = jnp.where((i * tm) % seq != 0, 1.0, 0.0)
        next_ok = jnp.where(((i + 1) * tm) % seq != 0, 1.0, 0.0)
        hn_ref[0:HALO, :] = (_rms(xp_ref[...], g) * prev_ok).astype(BF16)
        hn_ref[HALO:HALO + tm, :] = _rms(x_ref[...], g).astype(BF16)
        hn_ref[HALO + tm:, :] = (_rms(xn_ref[...], g) * next_ok).astype(BF16)
        acc_ref[...] = jnp.zeros_like(acc_ref)

    rows = tm + 2 * HALO
    gate = jnp.dot(hn_ref[...], wg_ref[0], preferred_element_type=F32)
    up = jnp.dot(hn_ref[HALO:HALO + tm, :], wu_ref[0], preferred_element_type=F32)
    cw = cw_ref[...]
    g_prev = pltpu.roll(gate, 1, 0)[HALO:HALO + tm]
    g_next = pltpu.roll(gate, rows - 1, 0)[HALO:HALO + tm]
    gc = cw[0:1] * g_prev + cw[1:2] * gate[HALO:HALO + tm] + cw[2:3] * g_next + cb_ref[...]
    act = (gc * _sigmoid(gc) * up).astype(BF16)
    acc_ref[...] += jnp.dot(act, wd_ref[0], preferred_element_type=F32)

    @pl.when(j == pl.num_programs(1) - 1)
    def _():
        o_ref[...] = x_ref[...] + acc_ref[...]


def ffn(x, g, w_gu, conv_w, conv_b, w_down, layer, seq, tm=512, tf=512):
    m, d = x.shape
    f = w_down.shape[1]
    nf = f // tf
    hb = tm // HALO
    last = m // HALO - 1
    return pl.pallas_call(
        functools.partial(_ffn_kernel, tm=tm, seq=seq),
        out_shape=jax.ShapeDtypeStruct((m, d), F32),
        grid=(m // tm, nf),
        in_specs=[
            pl.BlockSpec((tm, d), lambda i, j: (i, 0)),
            pl.BlockSpec((HALO, d), lambda i, j: (jnp.maximum(i * hb - 1, 0), 0)),
            pl.BlockSpec((HALO, d), lambda i, j: (jnp.minimum((i + 1) * hb, last), 0)),
            pl.BlockSpec((1, d), lambda i, j: (0, 0)),
            pl.BlockSpec((1, d, tf), lambda i, j: (layer, 0, j)),
            pl.BlockSpec((1, d, tf), lambda i, j: (layer, 0, j + nf)),
            pl.BlockSpec((3, tf), lambda i, j: (0, j)),
            pl.BlockSpec((1, tf), lambda i, j: (0, j)),
            pl.BlockSpec((1, tf, d), lambda i, j: (layer, j, 0)),
        ],
        out_specs=pl.BlockSpec((tm, d), lambda i, j: (i, 0)),
        scratch_shapes=[pltpu.VMEM((tm + 2 * HALO, d), BF16), pltpu.VMEM((tm, d), F32)],
        compiler_params=_params("parallel", "arbitrary"),
        name="ffn",
    )(x, x, x, g.reshape(1, d), w_gu, w_gu, conv_w, conv_b.reshape(1, f), w_down)


def _ple_kernel(x_ref, pa_ref, pb_ref, g_ref, wg_ref, wp_ref, o_ref, *, n1):
    x = x_ref[...]
    hn = _rms(x, g_ref[...]).astype(BF16)
    gate = _sigmoid(jnp.dot(hn, wg_ref[...], preferred_element_type=F32))
    p = jnp.where(pl.program_id(0) < n1, pa_ref[0], pb_ref[0])
    proj = jnp.dot(p.astype(BF16), wp_ref[...], preferred_element_type=F32)
    o_ref[...] = x + gate * proj


def _ple_final_kernel(x_ref, pa_ref, pb_ref, g_ref, wg_ref, wp_ref, gf_ref, oa_ref, ob_ref, *, n1):
    i = pl.program_id(0)
    x = x_ref[...]
    hn = _rms(x, g_ref[...]).astype(BF16)
    gate = _sigmoid(jnp.dot(hn, wg_ref[...], preferred_element_type=F32))
    p = jnp.where(i < n1, pa_ref[0], pb_ref[0])
    proj = jnp.dot(p.astype(BF16), wp_ref[...], preferred_element_type=F32)
    y = _rms(x + gate * proj, gf_ref[...])

    @pl.when(i < n1)
    def _():
        oa_ref[...] = y

    @pl.when(i >= n1)
    def _():
        ob_ref[...] = y


def ple_final(x, p, layer, g, w_gate, w_proj, g_final, tm=512):
    m, d = x.shape
    pa, pb = p
    n1 = pa.shape[1] // tm
    pd = pa.shape[2]
    return pl.pallas_call(
        functools.partial(_ple_final_kernel, n1=n1),
        out_shape=(jax.ShapeDtypeStruct((pa.shape[1], d), F32), jax.ShapeDtypeStruct((pb.shape[1], d), F32)),
        grid=(m // tm,),
        in_specs=[
            pl.BlockSpec((tm, d), lambda i: (i, 0)),
            pl.BlockSpec((1, tm, pd), lambda i: (layer, jnp.minimum(i, n1 - 1), 0)),
            pl.BlockSpec((1, tm, pd), lambda i: (layer, jnp.maximum(i - n1, 0), 0)),
            pl.BlockSpec((1, d), lambda i: (0, 0)),
            pl.BlockSpec((d, d), lambda i: (0, 0)),
            pl.BlockSpec((pd, d), lambda i: (0, 0)),
            pl.BlockSpec((1, d), lambda i: (0, 0)),
        ],
        out_specs=(pl.BlockSpec((tm, d), lambda i: (jnp.minimum(i, n1 - 1), 0)),
                   pl.BlockSpec((tm, d), lambda i: (jnp.maximum(i - n1, 0), 0))),
        compiler_params=_params("arbitrary"),
        name="ple_final",
    )(x, pa, pb, g.reshape(1, d), w_gate, w_proj, g_final.reshape(1, d))


def ple(x, p, layer, g, w_gate, w_proj, tm=512):
    m, d = x.shape
    pa, pb = p if isinstance(p, tuple) else (p, p)
    n1 = pa.shape[1] // tm
    pd = pa.shape[2]
    return pl.pallas_call(
        functools.partial(_ple_kernel, n1=n1),
        out_shape=jax.ShapeDtypeStruct((m, d), F32),
        grid=(m // tm,),
        in_specs=[
            pl.BlockSpec((tm, d), lambda i: (i, 0)),
            pl.BlockSpec((1, tm, pd), lambda i: (layer, jnp.minimum(i, n1 - 1), 0)),
            pl.BlockSpec((1, tm, pd), lambda i: (layer, jnp.maximum(i - n1, 0), 0)),
            pl.BlockSpec((1, d), lambda i: (0, 0)),
            pl.BlockSpec((d, d), lambda i: (0, 0)),
            pl.BlockSpec((pd, d), lambda i: (0, 0)),
        ],
        out_specs=pl.BlockSpec((tm, d), lambda i: (i, 0)),
        compiler_params=_params("parallel"),
        name="ple",
    )(x, pa, pb, g.reshape(1, d), w_gate, w_proj)


GDN_BLOCK = 256


def _conv_in_kernel(x_ref, xp_ref, xn_ref, g_ref, w_ref, cw_ref, o_ref, hn_ref, *, tm, seq, nq, nqk, scale):
    i = pl.program_id(0)
    j = pl.program_id(1)

    @pl.when(j == 0)
    def _():
        g = g_ref[...]
        prev_ok = jnp.where((i * tm) % seq != 0, 1.0, 0.0)
        next_ok = jnp.where(((i + 1) * tm) % seq != 0, 1.0, 0.0)
        hn_ref[0:HALO, :] = (_rms(xp_ref[...], g) * prev_ok).astype(BF16)
        hn_ref[HALO:HALO + tm, :] = _rms(x_ref[...], g).astype(BF16)
        hn_ref[HALO + tm:, :] = (_rms(xn_ref[...], g) * next_ok).astype(BF16)

    cw = cw_ref[...]
    w = w_ref[...]
    tn = w.shape[1]
    half = tm // 2
    rows = half + 2 * HALO
    for lo in (0, half):
        y = jnp.dot(hn_ref[lo:lo + rows, :], w, preferred_element_type=F32)
        y_prev = pltpu.roll(y, 1, 0)[HALO:HALO + half]
        y_next = pltpu.roll(y, rows - 1, 0)[HALO:HALO + half]
        c = cw[0:1] * y_prev + cw[1:2] * y[HALO:HALO + half] + cw[2:3] * y_next
        c = c * _sigmoid(c)
        for s in range(tn // 128):
            cs = c[:, s * 128:(s + 1) * 128]
            inv = lax.rsqrt(jnp.sum(cs * cs, axis=-1, keepdims=True) + EPS)
            f = jnp.where(j < nq, inv * scale, jnp.where(j < nqk, inv, 1.0))
            o_ref[lo:lo + half, s * 128:(s + 1) * 128] = (cs * f).astype(o_ref.dtype)


def conv_in(x, g, w, n, conv_w, seq, n_q, n_qk, scale, tm=512, tn=1024):
    m, d = x.shape
    hb = tm // HALO
    last = m // HALO - 1
    return pl.pallas_call(
        functools.partial(_conv_in_kernel, tm=tm, seq=seq, nq=n_q // tn, nqk=n_qk // tn, scale=scale),
        out_shape=jax.ShapeDtypeStruct((m, n), BF16),
        grid=(m // tm, n // tn),
        in_specs=[
            pl.BlockSpec((tm, d), lambda i, j: (i, 0)),
            pl.BlockSpec((HALO, d), lambda i, j: (jnp.maximum(i * hb - 1, 0), 0)),
            pl.BlockSpec((HALO, d), lambda i, j: (jnp.minimum((i + 1) * hb, last), 0)),
            pl.BlockSpec((1, d), lambda i, j: (0, 0)),
            pl.BlockSpec((d, tn), lambda i, j: (0, j)),
            pl.BlockSpec((3, tn), lambda i, j: (0, j)),
        ],
        out_specs=pl.BlockSpec((tm, tn), lambda i, j: (i, j)),
        scratch_shapes=[pltpu.VMEM((tm + 2 * HALO, d), BF16)],
        compiler_params=_params("parallel", "arbitrary"),
        name="gdn_conv_in",
    )(x, x, x, g.reshape(1, d), w, conv_w)


def _gdn_gates_kernel(x_ref, g_ref, w_ref, alog_ref, bias_ref, isg_ref, o_ref):
    hn = _rms(x_ref[...], g_ref[...]).astype(BF16)
    y = jnp.dot(hn, w_ref[...], preferred_element_type=F32)
    t = y + bias_ref[...]
    softplus = jnp.maximum(t, 0.0) + jnp.log1p(jnp.exp(-jnp.abs(t)))
    isg = isg_ref[...] > 0.5
    base = jnp.where(isg, -jnp.exp(alog_ref[...]) * softplus, 0.0)
    tm, n = base.shape
    lc = GDN_CHUNK
    pos = lax.broadcasted_iota(jnp.int32, (tm, n), 0) & (lc - 1)
    pre, suf = base, base
    k = 1
    while k < lc:
        pre = pre + jnp.where(pos >= k, pltpu.roll(pre, k, 0), 0.0)
        suf = suf + jnp.where(pos + k <= lc - 1, pltpu.roll(suf, tm - k, 0), 0.0)
        k *= 2
    is_bwd = lax.broadcasted_iota(jnp.int32, (1, n), 1) >= n // 2
    o_ref[:, 0:n] = jnp.where(isg, jnp.where(is_bwd, suf, pre), _sigmoid(y))
    o_ref[:, n:2 * n] = pre + suf - base


def gdn_gates(x, g, w, col0, a_log, dt_bias, tm=512):
    m, d = x.shape
    n = w.shape[1] - col0
    nh = a_log.shape[-1]
    zeros = jnp.zeros((2, 1, nh), F32)
    arrange = lambda t: jnp.concatenate([t.reshape(2, 1, nh).astype(F32), zeros], axis=1).reshape(1, n)
    isg = jnp.concatenate([jnp.ones((2, 1, nh), F32), zeros], axis=1).reshape(1, n)
    return pl.pallas_call(
        _gdn_gates_kernel,
        out_shape=jax.ShapeDtypeStruct((m, 2 * n), F32),
        grid=(m // tm,),
        in_specs=[
            pl.BlockSpec((tm, d), lambda i: (i, 0)),
            pl.BlockSpec((1, d), lambda i: (0, 0)),
            pl.BlockSpec((d, n), lambda i: (0, col0 // n)),
            pl.BlockSpec((1, n), lambda i: (0, 0)),
            pl.BlockSpec((1, n), lambda i: (0, 0)),
            pl.BlockSpec((1, n), lambda i: (0, 0)),
        ],
        out_specs=pl.BlockSpec((tm, 2 * n), lambda i: (i, 0)),
        compiler_params=_params("parallel"),
        name="gdn_gates",
    )(x, g.reshape(1, d), w, arrange(a_log), arrange(dt_bias), isg)


def _gdn_chains(q_ref, k_ref, v_ref, gc_ref, gr_ref, o_ref, rev):
    c_sz, lc = GDN_BLOCK, GDN_CHUNK
    d = 1 if rev else 0
    ii = lax.broadcasted_iota(jnp.int32, (c_sz, c_sz), 0)
    jj = lax.broadcasted_iota(jnp.int32, (c_sz, c_sz), 1)
    sh = lc.bit_length() - 1
    same = (ii >> sh) == (jj >> sh)
    incl = jnp.logical_and(same, (jj >= ii) if rev else (jj <= ii))
    eye = ii == jj
    q = q_ref[...]
    k = k_ref[...]
    kf = k.astype(F32)
    qf = q.astype(F32)
    gc = gc_ref[0, 0][:, 8 * d:8 * d + 8]
    gr = gr_ref[0, 0][8 * d:8 * d + 8, :]
    gam_c = gc[:, 0:2]
    gam_r = gr[0:2]
    end_r = gr[4:6]
    gram = lax.dot_general(k, k, (((1,), (1,)), ((), ())), preferred_element_type=F32)
    qk = lax.dot_general(q, k, (((1,), (1,)), ((), ())), preferred_element_type=F32)
    chains = []
    for hs in range(2):
        gcol = gam_c[:, hs:hs + 1]
        grow = gam_r[hs:hs + 1, :]
        bcol = gc[:, 2 + hs:3 + hs]
        dec = jnp.exp(jnp.where(incl, gcol - grow, NEG_INF))
        e_g = jnp.exp(gcol)
        vh = v_ref[:, hs * GDN_DV:(hs + 1) * GDN_DV].astype(F32)
        chains.append(dict(
            n=(-(gram * jnp.where(eye, 0.0, dec)) * bcol).astype(BF16),
            x=jnp.concatenate([kf * (bcol * e_g), vh * bcol], axis=1),
            attn=(qk * dec).astype(BF16),
            qe=qf * e_g,
            kd=kf * jnp.exp(gc[:, 4 + hs:5 + hs] - gcol),
            end=end_r[hs:hs + 1, :],
            o_ref=o_ref, hs=hs, idx=2 * d + hs, rev=rev))
    return chains


def _gdn_scan_kernel(qf_ref, kf_ref, vf_ref, gcf_ref, grf_ref,
                     qb_ref, kb_ref, vb_ref, gcb_ref, grb_ref, of_ref, ob_ref, s_ref):
    @pl.when(pl.program_id(2) == 0)
    def _():
        s_ref[...] = jnp.zeros_like(s_ref)

    c_sz, lc = GDN_BLOCK, GDN_CHUNK
    nchunk = c_sz // lc
    width = GDN_DK + GDN_DV
    chains = (_gdn_chains(qf_ref, kf_ref, vf_ref, gcf_ref, grf_ref, of_ref, False)
              + _gdn_chains(qb_ref, kb_ref, vb_ref, gcb_ref, grb_ref, ob_ref, True))
    nlev = lc.bit_length() - 1
    for lvl in range(nlev):
        for ch in chains:
            nb = ch['n']
            xb = ch['x'].astype(BF16)
            if lvl < nlev - 1:
                r = jnp.dot(nb, jnp.concatenate([xb, nb], axis=1), preferred_element_type=F32)
                ch['x'] = ch['x'] + r[:, :width]
                ch['n'] = r[:, width:].astype(BF16)
            else:
                ch['x'] = ch['x'] + jnp.dot(nb, xb, preferred_element_type=F32)
    irow = lax.broadcasted_iota(jnp.int32, (c_sz, 1), 0) >> nlev
    for ch in chains:
        wub = ch['x'].astype(BF16)
        awu = jnp.dot(ch['attn'], wub, preferred_element_type=F32)
        ch['qeff'] = (ch['qe'] - awu[:, :GDN_DK]).astype(BF16)
        ch['o_in'] = awu[:, GDN_DK:]
        ch['kwu'] = [lax.dot_general(jnp.where(irow == c, ch['kd'], 0.0).astype(BF16), wub,
                                     (((0,), (0,)), ((), ())), preferred_element_type=F32)
                     for c in range(nchunk)]
        ch['s'] = s_ref[ch['idx']]
    for step in range(nchunk):
        for ch in chains:
            c = nchunk - 1 - step if ch['rev'] else step
            r0 = c * lc
            s = ch['s']
            sb = s.astype(BF16)
            o_c = ch['o_in'][r0:r0 + lc] + jnp.dot(ch['qeff'][r0:r0 + lc], sb, preferred_element_type=F32)
            ch['o_ref'][r0:r0 + lc, ch['hs'] * GDN_DV:(ch['hs'] + 1) * GDN_DV] = o_c.astype(ch['o_ref'].dtype)
            kwu = ch['kwu'][c]
            e_end = jnp.exp(ch['end'][:, r0:r0 + 1])
            ch['s'] = (e_end * s - jnp.dot(kwu[:, :GDN_DK].astype(BF16), sb, preferred_element_type=F32)
                       + kwu[:, GDN_DK:])
    for ch in chains:
        s_ref[ch['idx']] = ch['s']


def gdn_scan(qkv, gcol, grow, bsz, seq):
    c_sz = GDN_BLOCK
    nb = seq // c_sz
    hq = GDN_QK_HEADS
    kcol = hq
    vcol = (2 * hq * GDN_DK) // (2 * GDN_DV)
    fwd = lambda b, h, c: c
    bwd = lambda b, h, c: nb - 1 - c

    def specs(pos):
        return [
            pl.BlockSpec((c_sz, GDN_DK), lambda b, h, c: (b * nb + pos(b, h, c), h)),
            pl.BlockSpec((c_sz, GDN_DK), lambda b, h, c: (b * nb + pos(b, h, c), kcol + h)),
            pl.BlockSpec((c_sz, 2 * GDN_DV), lambda b, h, c: (b * nb + pos(b, h, c), vcol + h)),
            pl.BlockSpec((1, 1, c_sz, 16), lambda b, h, c: (b, h, pos(b, h, c), 0)),
            pl.BlockSpec((1, 1, 16, c_sz), lambda b, h, c: (b, h, 0, pos(b, h, c))),
        ]

    out = jax.ShapeDtypeStruct((bsz * seq, GDN_V_HEADS * GDN_DV), BF16)
    return pl.pallas_call(
        _gdn_scan_kernel,
        out_shape=(out, out),
        grid=(bsz, hq, nb),
        in_specs=specs(fwd) + specs(bwd),
        out_specs=(
            pl.BlockSpec((c_sz, 2 * GDN_DV), lambda b, h, c: (b * nb + c, h)),
            pl.BlockSpec((c_sz, 2 * GDN_DV), lambda b, h, c: (b * nb + nb - 1 - c, h)),
        ),
        scratch_shapes=[pltpu.VMEM((4, GDN_DK, GDN_DV), F32)],
        compiler_params=_params("parallel", "parallel", "arbitrary"),
        name="gdn_scan",
    )(qkv, qkv, qkv, gcol, grow, qkv, qkv, qkv, gcol, grow)


def _gdn_out_kernel(of_ref, ob_ref, z_ref, gn_ref, w_ref, r_ref, o_ref):
    @pl.when(pl.program_id(1) == 0)
    def _():
        o_ref[...] = r_ref[...]

    gn = gn_ref[...]
    parts = []
    for h in range(of_ref.shape[1] // GDN_DV):
        sl = slice(h * GDN_DV, (h + 1) * GDN_DV)
        o = of_ref[:, sl].astype(F32) + ob_ref[:, sl].astype(F32)
        z = z_ref[:, sl].astype(F32)
        parts.append((_rms(o, gn) * (z * _sigmoid(z))).astype(BF16))
    o_ref[...] += jnp.dot(jnp.concatenate(parts, axis=1), w_ref[...], preferred_element_type=F32)


def gdn_out(o_f, o_b, z, out_norm, w_o, res, tm=512, tk=1024):
    m, k = o_f.shape
    n = w_o.shape[1]
    return pl.pallas_call(
        _gdn_out_kernel,
        out_shape=jax.ShapeDtypeStruct((m, n), F32),
        grid=(m // tm, k // tk),
        in_specs=[
            pl.BlockSpec((tm, tk), lambda i, j: (i, j)),
            pl.BlockSpec((tm, tk), lambda i, j: (i, j)),
            pl.BlockSpec((tm, tk), lambda i, j: (i, j)),
            pl.BlockSpec((1, GDN_DV), lambda i, j: (0, 0)),
            pl.BlockSpec((tk, n), lambda i, j: (j, 0)),
            pl.BlockSpec((tm, n), lambda i, j: (i, 0)),
        ],
        out_specs=pl.BlockSpec((tm, n), lambda i, j: (i, 0)),
        compiler_params=_params("parallel", "arbitrary"),
        name="gdn_out",
    )(o_f, o_b, z, out_norm.reshape(1, GDN_DV), w_o, res)


def gated_deltanet(x, gmix, w_in, conv_w, a_log, dt_bias, out_norm, w_o, bsz, seq):
    qk_w = GDN_QK_HEADS * GDN_DK
    v_w = GDN_V_HEADS * GDN_DV
    cw = 2 * qk_w + v_w
    w_in = w_in.astype(BF16)
    qkv = conv_in(x, gmix, w_in, cw, conv_w, seq, qk_w, 2 * qk_w, GDN_DK ** -0.5)
    z = norm_matmul(x, gmix, w_in, out_dtype=BF16, col0=cw, ncols=v_w)
    gb = gdn_gates(x, gmix, w_in, cw + v_w, a_log, dt_bias)
    nab = gb.shape[1] // 2
    ab = gb[:, :nab].reshape(bsz, seq, 2, 2, GDN_QK_HEADS, 2).transpose(0, 4, 1, 2, 3, 5)
    tot = gb[:, nab:].reshape(bsz, seq, 2, 2, GDN_QK_HEADS, 2)[:, :, :, 0].transpose(0, 3, 1, 2, 4)[..., None, :]
    gcol = jnp.concatenate([ab, tot, jnp.zeros_like(tot)], axis=4).reshape(bsz, GDN_QK_HEADS, seq, 16)
    grow = jnp.swapaxes(gcol, 2, 3)
    o_f, o_b = gdn_scan(qkv, gcol, grow, bsz, seq)
    return gdn_out(o_f, o_b, z, out_norm, w_o.astype(BF16), x)


def _na_bias_table(rpb, rows):
    win_r = min(NA_WIN_R, rows)
    cols = np.arange(GRID_W)
    col_start = np.clip(cols - NA_WIN_C // 2, 0, GRID_W - NA_WIN_C)
    col_valid = (cols[None, :] >= col_start[:, None]) & (cols[None, :] < col_start[:, None] + NA_WIN_C)
    dc_idx = np.clip(cols[None, :] - cols[:, None] + NA_WIN_C - 1, 0, 2 * NA_WIN_C - 2)
    bias_c = jnp.where(col_valid, rpb[:, :, dc_idx].astype(F32), NEG_INF)
    dr = np.arange(NA_WIN_R)[:, None] + np.arange(win_r)[None, :]
    tab = bias_c[:, dr]
    return jnp.transpose(tab, (0, 1, 3, 2, 4)).reshape(rpb.shape[0], NA_WIN_R, GRID_W, win_r * GRID_W)


def _na_kernel(q_ref, k_ref, v_ref, b_ref, o_ref, *, rows, win_r, scale):
    wk = win_r * GRID_W

    group = 16 if rows % 16 == 0 else 1

    def body(it, carry):
        rs = [it * group + i for i in range(group)]
        r0s = [jnp.clip(r - win_r // 2, 0, rows - win_r) for r in rs]
        scores = []
        for r, r0 in zip(rs, r0s):
            q = q_ref[pl.ds(pl.multiple_of(r * GRID_W, GRID_W), GRID_W), :]
            kw = k_ref[pl.ds(pl.multiple_of(r0 * GRID_W, GRID_W), wk), :]
            s = lax.dot_general(q, kw, (((1,), (1,)), ((), ())), preferred_element_type=F32) * scale
            scores.append(s + b_ref[0, r0 - r + NA_WIN_R - 1])
        probs, dens = [], []
        for s in scores:
            p = jnp.exp(s - jnp.max(s, axis=-1, keepdims=True))
            dens.append(jnp.sum(p, axis=-1, keepdims=True))
            probs.append(p.astype(BF16))
        for r, r0, p, den in zip(rs, r0s, probs, dens):
            vw = v_ref[pl.ds(pl.multiple_of(r0 * GRID_W, GRID_W), wk), :]
            o = jnp.dot(p, vw, preferred_element_type=F32) / den
            o_ref[pl.ds(pl.multiple_of(r * GRID_W, GRID_W), GRID_W), :] = o.astype(o_ref.dtype)
        return carry

    lax.fori_loop(0, rows // group, body, 0)


def na_attention(qkv, rpb, bsz, seq):
    d = qkv.shape[1] // 3
    dh = d // NA_HEADS
    rows = seq // GRID_W
    win_r = min(NA_WIN_R, rows)
    table = _na_bias_table(rpb, rows)
    blk = lambda off: pl.BlockSpec((seq, dh), lambda b, h: (b, off + h))
    return pl.pallas_call(
        functools.partial(_na_kernel, rows=rows, win_r=win_r, scale=dh ** -0.5),
        out_shape=jax.ShapeDtypeStruct((bsz * seq, d), BF16),
        grid=(bsz, NA_HEADS),
        in_specs=[blk(0), blk(NA_HEADS), blk(2 * NA_HEADS),
                  pl.BlockSpec((1, NA_WIN_R, GRID_W, win_r * GRID_W), lambda b, h: (h, 0, 0, 0))],
        out_specs=blk(0),
        compiler_params=_params("parallel", "parallel"),
        name="na_attention",
    )(qkv, qkv, qkv, table)


def _sg_out_kernel(u_ref, v_ref, gn_ref, ws_ref, bs_ref, w_ref, r_ref, o_ref, a_ref, *, tm):
    @pl.when(pl.program_id(1) == 0)
    def _():
        vn = _rms(v_ref[...].astype(F32), gn_ref[...]).astype(BF16)
        bs = bs_ref[...]
        gd = vn.shape[1] // SG_GROUPS
        for c in range(tm // SG_CHUNK):
            rs = slice(c * SG_CHUNK, (c + 1) * SG_CHUNK)
            for g in range(SG_GROUPS):
                cs = slice(g * gd, (g + 1) * gd)
                mixed = jnp.dot(ws_ref[g], vn[rs, cs], preferred_element_type=F32) + bs[:, g:g + 1]
                a_ref[rs, cs] = (u_ref[rs, cs].astype(F32) * mixed).astype(BF16)

    o_ref[...] = r_ref[...] + jnp.dot(a_ref[...], w_ref[...], preferred_element_type=F32)


def sg_out(uv, sg_norm, w_s, b_s, w_o, res, tm=512, tn=1024):
    m = uv.shape[0]
    width = uv.shape[1] // 2
    n = w_o.shape[1]
    return pl.pallas_call(
        functools.partial(_sg_out_kernel, tm=tm),
        out_shape=jax.ShapeDtypeStruct((m, n), F32),
        grid=(m // tm, n // tn),
        in_specs=[
            pl.BlockSpec((tm, width), lambda i, j: (i, 0)),
            pl.BlockSpec((tm, width), lambda i, j: (i, 1)),
            pl.BlockSpec((1, width), lambda i, j: (0, 0)),
            pl.BlockSpec(w_s.shape, lambda i, j: (0, 0, 0)),
            pl.BlockSpec((SG_CHUNK, SG_GROUPS), lambda i, j: (0, 0)),
            pl.BlockSpec((width, tn), lambda i, j: (0, j)),
            pl.BlockSpec((tm, tn), lambda i, j: (i, j)),
        ],
        out_specs=pl.BlockSpec((tm, tn), lambda i, j: (i, j)),
        scratch_shapes=[pltpu.VMEM((tm, width), BF16)],
        compiler_params=_params("parallel", "arbitrary"),
        name="sg_out",
    )(uv, uv, sg_norm.reshape(1, width), w_s.astype(BF16), b_s.T.astype(F32), w_o, res)


S5_L = 32
S5_W = S5_L * S5_GROUP_DIM


def _cexp(are, aim, dt, e):
    mag = jnp.exp(are * dt * e)
    ang = aim * dt * e
    return mag * jnp.cos(ang), mag * jnp.sin(ang)


def _s5_prep_kernel(arc_ref, aic_ref, arr_ref, air_ref, ldt_ref, btr_ref, bti_ref, ctr_ref, cti_ref,
                    k_ref, bm_ref, cm_ref, ap_ref, *, rev):
    hp = lax.Precision.HIGHEST
    ll, cg, w, p = S5_L, S5_GROUP_DIM, S5_W, S5_STATE
    sh = cg.bit_length() - 1
    dt = jnp.exp(ldt_ref[0])
    arc, aic = arc_ref[0], aic_ref[0]
    arr, air = arr_ref[0], air_ref[0]
    abr, abi = _cexp(arr, air, dt, 1.0)
    nr, ni = abr - 1.0, abi
    den = arr * arr + air * air
    cr, ci = (nr * arr + ni * air) / den, (ni * arr - nr * air) / den
    btr, bti = btr_ref[0], bti_ref[0]
    bbr, bbi = cr * btr - ci * bti, cr * bti + ci * btr
    lane = lax.broadcasted_iota(jnp.int32, (cg, w), 1)
    sel = jnp.where((lane & (cg - 1)) == lax.broadcasted_iota(jnp.int32, (cg, w), 0), 1.0, 0.0)
    cer = jnp.dot(ctr_ref[0], sel, precision=hp, preferred_element_type=F32)
    cei = jnp.dot(cti_ref[0], sel, precision=hp, preferred_element_type=F32)
    tl = (lax.broadcasted_iota(jnp.int32, (p, w), 1) >> sh).astype(F32)

    def cz(e):
        zr, zi = _cexp(arc, aic, dt, e)
        return cer * zr - cei * zi, cer * zi + cei * zr

    czr, czi = cz((ll - 1.0 - tl) if rev else tl)
    r = jnp.dot(bbr, czr, precision=hp, preferred_element_type=F32) - jnp.dot(bbi, czi, precision=hp,
                                                                                preferred_element_type=F32)
    lane_r = lax.broadcasted_iota(jnp.int32, (cg, w), 1)
    for s in range(ll):
        if rev:
            blk = jnp.where(lane_r < cg * (s + 1), pltpu.roll(r, (w - cg * (ll - 1 - s)) % w, 1), 0.0)
        else:
            blk = jnp.where(lane_r >= cg * s, pltpu.roll(r, cg * s, 1), 0.0)
        k_ref[0, s * cg:(s + 1) * cg, :] = blk.astype(k_ref.dtype)
    acr, aci = _cexp(arc, aic, dt, 1.0)
    c1r, c1i = czr * acr - czi * aci, czr * aci + czi * acr
    cm_ref[0, 0:p, :] = c1r.astype(cm_ref.dtype)
    cm_ref[0, p:2 * p, :] = (-c1i).astype(cm_ref.dtype)
    srow = (lax.broadcasted_iota(jnp.int32, (w, p), 0) >> sh).astype(F32)
    zr, zi = _cexp(arr, air, dt, srow if rev else (ll - 1.0 - srow))
    tbr, tbi = jnp.tile(bbr, (ll, 1)), jnp.tile(bbi, (ll, 1))
    bm_ref[0, :, 0:p] = (zr * tbr - zi * tbi).astype(bm_ref.dtype)
    bm_ref[0, :, p:2 * p] = (zr * tbi + zi * tbr).astype(bm_ref.dtype)
    ek = (ll << lax.broadcasted_iota(jnp.int32, (8, p), 0)).astype(F32)
    pr, pi = _cexp(arr, air, dt, ek)
    ap_ref[0, :, 0:p] = pr
    ap_ref[0, :, p:2 * p] = pi


def s5_prep(a_re, a_im, log_dt, b_re, b_im, c_re, c_im, rev):
    g, p = a_re.shape
    cg, w = S5_GROUP_DIM, S5_W
    col = lambda t: t.reshape(g, p, 1).astype(F32)
    row = lambda t: t.reshape(g, 1, p).astype(F32)
    tr = lambda t: jnp.swapaxes(t, 1, 2).astype(F32)
    spec = lambda s: pl.BlockSpec((1,) + s, lambda i: (i, 0, 0))
    return pl.pallas_call(
        functools.partial(_s5_prep_kernel, rev=rev),
        out_shape=(jax.ShapeDtypeStruct((g, w, w), BF16), jax.ShapeDtypeStruct((g, w, 2 * p), BF16),
                   jax.ShapeDtypeStruct((g, 2 * p, w), BF16), jax.ShapeDtypeStruct((g, 8, 2 * p), F32)),
        grid=(g,),
        in_specs=[spec((p, 1)), spec((p, 1)), spec((1, p)), spec((1, p)), spec((1, 1)),
                  spec((cg, p)), spec((cg, p)), spec((p, cg)), spec((p, cg))],
        out_specs=(spec((w, w)), spec((w, 2 * p)), spec((2 * p, w)), spec((8, 2 * p))),
        compiler_params=_params("parallel"),
        name="s5_prep",
    )(col(a_re), col(a_im), row(a_re), row(a_im), log_dt.reshape(g, 1, 1).astype(F32),
      tr(b_re), tr(b_im), tr(c_re), tr(c_im))


def _s5_chunk_scan(s, ap, nchunk, rev):
    n, w2 = s.shape
    p = w2 // 2
    m = lax.broadcasted_iota(jnp.int32, (n, w2), 0) & (nchunk - 1)
    lane = lax.broadcasted_iota(jnp.int32, (1, w2), 1)

    def shift(x, k):
        if rev:
            return jnp.where(m + k <= nchunk - 1, pltpu.roll(x, n - k, 0), 0.0)
        return jnp.where(m >= k, pltpu.roll(x, k, 0), 0.0)

    x = s
    k, lvl = 1, 0
    while k < nchunk:
        a = ap[lvl:lvl + 1, :]
        a1 = jnp.where(lane < p, a, pltpu.roll(a, p, 1))
        a2 = jnp.where(lane < p, -pltpu.roll(a, p, 1), a)
        xs = shift(x, k)
        x = x + a1 * xs + a2 * pltpu.roll(xs, p, 1)
        k, lvl = 2 * k, lvl + 1
    return shift(x, 1)


def _s5_main_kernel(u_ref, kf_ref, kb_ref, bf_ref, bb_ref, cf_ref, cb_ref, af_ref, ab_ref, y_ref, *, nchunk):
    u = u_ref[0]
    y = jnp.dot(u, kf_ref[0], preferred_element_type=F32) + jnp.dot(u, kb_ref[0], preferred_element_type=F32)
    for bm_ref, cm_ref, ap_ref, rev in ((bf_ref, cf_ref, af_ref, False), (bb_ref, cb_ref, ab_ref, True)):
        s = jnp.dot(u, bm_ref[0], preferred_element_type=F32)
        xin = _s5_chunk_scan(s, ap_ref[0], nchunk, rev)
        y = y + jnp.dot(xin.astype(BF16), cm_ref[0], preferred_element_type=F32)
    y_ref[0] = y.astype(y_ref.dtype)


def s5_main(u, prep_f, prep_b, nchunk):
    g, n, w = u.shape
    p2 = 2 * S5_STATE
    spec = lambda s: pl.BlockSpec((1,) + s, lambda i: (i, 0, 0))
    kf, bf, cf, af = prep_f
    kb, bb, cb, ab = prep_b
    return pl.pallas_call(
        functools.partial(_s5_main_kernel, nchunk=nchunk),
        out_shape=jax.ShapeDtypeStruct((g, n, w), BF16),
        grid=(g,),
        in_specs=[spec((n, w)), spec((w, w)), spec((w, w)), spec((w, p2)), spec((w, p2)),
                  spec((p2, w)), spec((p2, w)), spec((8, p2)), spec((8, p2))],
        out_specs=spec((n, w)),
        compiler_params=_params("parallel"),
        name="s5_main",
    )(u, kf, kb, bf, bb, cf, cb, af, ab)


def _s5_out_kernel(x_ref, g_ref, y_ref, d_ref, wa_ref, wb_ref, o_ref, inv_ref, acca_ref, accb_ref):
    k = pl.program_id(1)
    tk = y_ref.shape[1]

    @pl.when(k == 0)
    def _():
        x = x_ref[...]
        inv_ref[...] = jnp.broadcast_to(lax.rsqrt(jnp.mean(x * x, axis=-1, keepdims=True) + EPS), inv_ref.shape)
        acca_ref[...] = jnp.zeros_like(acca_ref)
        accb_ref[...] = jnp.zeros_like(accb_ref)

    xk = x_ref[:, pl.ds(pl.multiple_of(k * tk, tk), tk)]
    h = xk * inv_ref[:, 0:1] * g_ref[...]
    a = _gelu_tanh(y_ref[...].astype(F32) + d_ref[...] * h).astype(BF16)
    acca_ref[...] += jnp.dot(a, wa_ref[...], preferred_element_type=F32)
    accb_ref[...] += jnp.dot(a, wb_ref[...], preferred_element_type=F32)

    @pl.when(k == pl.num_programs(1) - 1)
    def _():
        o_ref[...] = x_ref[...] + acca_ref[...] * _sigmoid(accb_ref[...])


def s5_out(x, g, y, d_skip, w_glu, tm=512, tk=512):
    m, d = x.shape
    n = w_glu.shape[1] // 2
    return pl.pallas_call(
        _s5_out_kernel,
        out_shape=jax.ShapeDtypeStruct((m, n), F32),
        grid=(m // tm, d // tk),
        in_specs=[
            pl.BlockSpec((tm, d), lambda i, k: (i, 0)),
            pl.BlockSpec((1, tk), lambda i, k: (0, k)),
            pl.BlockSpec((tm, tk), lambda i, k: (i, k)),
            pl.BlockSpec((1, tk), lambda i, k: (0, k)),
            pl.BlockSpec((tk, n), lambda i, k: (k, 0)),
            pl.BlockSpec((tk, n), lambda i, k: (k, 1)),
        ],
        out_specs=pl.BlockSpec((tm, n), lambda i, k: (i, 0)),
        scratch_shapes=[pltpu.VMEM((tm, 128), F32), pltpu.VMEM((tm, n), F32), pltpu.VMEM((tm, n), F32)],
        compiler_params=_params("parallel", "arbitrary"),
        name="s5_out",
    )(x, g.reshape(1, d), y, d_skip.reshape(1, d), w_glu, w_glu)


def s5_mixer(x, gmix, a_re, a_im, log_dt, b_re, b_im, c_re, c_im, d_skip, w_glu, bsz, seq):
    m, d = x.shape
    groups = d // S5_GROUP_DIM
    nchunk = seq // S5_L
    h = rmsnorm(x, gmix, out_dtype=BF16)
    u = h.reshape(m // S5_L, S5_L, groups, S5_GROUP_DIM).transpose(2, 0, 1, 3).reshape(groups, m // S5_L, S5_W)
    prep_f = s5_prep(a_re[0], a_im[0], log_dt[0], b_re[0], b_im[0], c_re[0], c_im[0], rev=False)
    prep_b = s5_prep(a_re[1], a_im[1], log_dt[1], b_re[1], b_im[1], c_re[1], c_im[1], rev=True)
    y = s5_main(u, prep_f, prep_b, nchunk)
    y = y.reshape(groups, m // S5_L, S5_L, S5_GROUP_DIM).transpose(1, 2, 0, 3).reshape(m, d)
    return s5_out(x, gmix, y, d_skip, w_glu.astype(BF16))


def _trunk(x, p, w, bsz, seq, final_norm=None):
    depth = w['norm_mix'].shape[0]
    bf = lambda t: t.astype(BF16)
    w_gu, w_down = bf(w['ffn_w_gu']), bf(w['ffn_w_down'])
    for i in range(depth):
        kind, j = i % N_MIXERS, i // N_MIXERS
        gmix = w['norm_mix'][i]
        if kind == 0:
            qkv = norm_matmul(x, gmix, bf(w['na_w_qkv'][j]), out_dtype=BF16)
            att = na_attention(qkv, w['na_rpb'][j], bsz, seq)
            x = matmul_res(att, bf(w['na_w_o'][j]), x)
        elif kind == 1:
            uv = norm_matmul(x, gmix, bf(w['sg_w_in'][j]), act="gelu", out_dtype=BF16)
            x = sg_out(uv, w['sg_norm'][j], w['sg_w_s'][j], w['sg_b_s'][j], bf(w['sg_w_o'][j]), x)
        elif kind == 2:
            x = gated_deltanet(x, gmix, w['gdn_w_in'][j], w['gdn_conv_w'][j], w['gdn_a_log'][j],
                               w['gdn_dt_bias'][j], w['gdn_out_norm'][j], w['gdn_w_o'][j], bsz, seq)
        else:
            x = s5_mixer(x, gmix, w['s5_a_re'][j], w['s5_a_im'][j], w['s5_log_dt'][j], w['s5_b_re'][j],
                         w['s5_b_im'][j], w['s5_c_re'][j], w['s5_c_im'][j], w['s5_d'][j], w['s5_w_glu'][j], bsz, seq)
        x = ffn(x, w['norm_ffn'][i], w_gu, w['ffn_conv_w'][i], w['ffn_conv_b'][i], w_down, i, seq)
        if final_norm is not None and i == depth - 1:
            return ple_final(x, p, i, w['norm_ple'][i], bf(w['ple_w_gate'][i]), bf(w['ple_w_proj'][i]), final_norm)
        x = ple(x, p, i, w['norm_ple'][i], bf(w['ple_w_gate'][i]), bf(w['ple_w_proj'][i]))
    return x


def kernel(x_prompt, x_sample, p_prompt, p_sample, norm_mix, norm_ffn, norm_ple, final_norm, na_w_qkv, na_w_o, na_rpb, sg_w_in, sg_norm, sg_w_s, sg_b_s, sg_w_o, gdn_w_in, gdn_conv_w, gdn_a_log, gdn_dt_bias, gdn_out_norm, gdn_w_o, s5_a_re, s5_a_im, s5_log_dt, s5_b_re, s5_b_im, s5_c_re, s5_c_im, s5_d, s5_w_glu, ffn_w_gu, ffn_conv_w, ffn_conv_b, ffn_w_down, ple_w_proj, ple_w_gate):
    w = dict(norm_mix=norm_mix, norm_ffn=norm_ffn, norm_ple=norm_ple, final_norm=final_norm,
             na_w_qkv=na_w_qkv, na_w_o=na_w_o, na_rpb=na_rpb,
             sg_w_in=sg_w_in, sg_norm=sg_norm, sg_w_s=sg_w_s, sg_b_s=sg_b_s, sg_w_o=sg_w_o,
             gdn_w_in=gdn_w_in, gdn_conv_w=gdn_conv_w, gdn_a_log=gdn_a_log, gdn_dt_bias=gdn_dt_bias,
             gdn_out_norm=gdn_out_norm, gdn_w_o=gdn_w_o,
             s5_a_re=s5_a_re, s5_a_im=s5_a_im, s5_log_dt=s5_log_dt, s5_b_re=s5_b_re, s5_b_im=s5_b_im,
             s5_c_re=s5_c_re, s5_c_im=s5_c_im, s5_d=s5_d, s5_w_glu=s5_w_glu,
             ffn_w_gu=ffn_w_gu, ffn_conv_w=ffn_conv_w, ffn_conv_b=ffn_conv_b, ffn_w_down=ffn_w_down,
             ple_w_proj=ple_w_proj, ple_w_gate=ple_w_gate)
    b1, seq, d = x_prompt.shape
    b2 = x_sample.shape[0]
    bsz = b1 + b2
    depth = p_prompt.shape[0]
    x = (x_prompt.reshape(b1 * seq, d), x_sample.reshape(b2 * seq, d))
    p = (p_prompt.reshape(depth, b1 * seq, -1), p_sample.reshape(depth, b2 * seq, -1))
    y1, y2 = _trunk(x, p, w, bsz, seq, final_norm=final_norm)
    return (y1.reshape(b1, seq, d), y2.reshape(b2, seq, d))
```
